```python
import jax
import jax.numpy as jnp
from jax import lax
import numpy as np


D_MODEL = 2048
BATCH = 2
SEQ = 8192
DEPTH = 4
DEC_BATCH = 32
DEC_SEQ = 32
PAST_LEN = 4096

CHUNK = 64
GM_WIDTH = 1024
GM_GROUPS = 8
GM_CHUNK = 128
GM_GROUP_DIM = GM_WIDTH // GM_GROUPS
CV_WIDTH = 1024
CV_KERNEL = 31
N_HEADS = 16
QK_NOPE = 128
QK_ROPE = 64
V_DIM = 128
Q_RANK = 512
KV_RANK = 512
ROPE_THETA = 10000.0
Q_BLOCK = 128
ATTN_SCALE = (QK_NOPE + QK_ROPE) ** -0.5
N_BRANCH = 3
OFF_CV = 2 * GM_WIDTH
OFF_Q = OFF_CV + 2 * CV_WIDTH
OFF_KV = OFF_Q + Q_RANK
OFF_KR = OFF_KV + KV_RANK
OFF_G = OFF_KR + QK_ROPE
N_IN = OFF_G + N_BRANCH * D_MODEL
D_FF = 5632
N_EXPERTS = 8
TOP_K = 2
D_EXPERT = 5632
MOE_BLOCK = 128
N_DENSE = (DEPTH + 1) // 2
N_MOE = DEPTH // 2
ALPHA = (2 * DEPTH) ** 0.25
BETA = (8 * DEPTH) ** -0.25
LN_EPS = 1e-5
RMS_EPS = 1e-6

kernel_name = 'hybrid_gmlp_conformer_mla_streaming_step'


def layer_norm(x, g, b):
    xf = x.astype(jnp.float32)
    mu = jnp.mean(xf, -1, keepdims=True)
    var = jnp.mean(jnp.square(xf - mu), -1, keepdims=True)
    return ((xf - mu) * lax.rsqrt(var + LN_EPS) * g.astype(jnp.float32) + b.astype(jnp.float32)).astype(x.dtype)


def rms_norm(x, g):
    xf = x.astype(jnp.float32)
    ms = jnp.mean(jnp.square(xf), -1, keepdims=True)
    return (xf * lax.rsqrt(ms + RMS_EPS) * g.astype(jnp.float32)).astype(x.dtype)


def rope(x, pos):
    half = QK_ROPE // 2
    inv = ROPE_THETA ** (-jnp.arange(half, dtype=jnp.float32) / half)
    ang = pos.astype(jnp.float32)[:, None] * inv[None, :]
    ang = ang.reshape((ang.shape[0],) + (1,) * (x.ndim - 3) + (half,))
    cos = jnp.cos(ang).astype(x.dtype)
    sin = jnp.sin(ang).astype(x.dtype)
    x1, x2 = x[..., :half], x[..., half:]
    return jnp.concatenate([x1 * cos - x2 * sin, x1 * sin + x2 * cos], axis=-1)


def gmlp_branch(uv, ln_g, ln_b, w_s, b_s, w_p):
    uv = jax.nn.gelu(uv)
    u, v = jnp.split(uv, 2, axis=-1)
    v = layer_norm(v, ln_g, ln_b)
    B, L, _ = v.shape
    c = min(L, GM_CHUNK)
    vb = v.reshape(B, L // c, c, GM_GROUPS, GM_GROUP_DIM)
    causal = jnp.tril(jnp.ones((c, c), dtype=bool))
    ws = jnp.where(causal[None], w_s[:, :c, :c], jnp.zeros((), w_s.dtype))
    s = jnp.einsum('gij,bnjgd->bnigd', ws, vb) + jnp.transpose(b_s[:, :c])[:, :, None]
    s = s.reshape(B, L, GM_WIDTH)
    return (u * s) @ w_p, v


def conv_branch(ab, hist, w_dw, b_dw, ln_g, ln_b, w_pw):
    a, b = jnp.split(ab, 2, axis=-1)
    glu = a * jax.nn.sigmoid(b)
    full = jnp.concatenate([hist, glu], axis=1)
    y = lax.conv_general_dilated(full, w_dw[:, None, :], window_strides=(1,), padding='VALID',
                                 dimension_numbers=('NWC', 'WIO', 'NWC'),
                                 feature_group_count=CV_WIDTH) + b_dw
    y = jax.nn.silu(layer_norm(y, ln_g, ln_b))
    return y @ w_pw, full[:, -(CV_KERNEL - 1):]


def mla_project(cq, ckv_raw, kr_raw, pos, q_norm_g, kv_norm_g, w_uq):
    B, L, _ = cq.shape
    q = (rms_norm(cq, q_norm_g) @ w_uq).reshape(B, L, N_HEADS, QK_NOPE + QK_ROPE)
    q_nope, q_rope = q[..., :QK_NOPE], rope(q[..., QK_NOPE:], pos)
    ckv = rms_norm(ckv_raw, kv_norm_g)
    kr = rope(kr_raw, pos)
    return q_nope, q_rope, ckv, kr


def mla_attend_prompt(q_nope, q_rope, ckv, kr, w_uk, w_uv):
    B, S = ckv.shape[0], ckv.shape[1]
    k_nope = (ckv @ w_uk).reshape(B, S, N_HEADS, QK_NOPE)
    v = (ckv @ w_uv).reshape(B, S, N_HEADS, V_DIM)
    nb = S // Q_BLOCK
    qn_b = jnp.transpose(q_nope.reshape(B, nb, Q_BLOCK, N_HEADS, QK_NOPE), (1, 0, 2, 3, 4))
    qr_b = jnp.transpose(q_rope.reshape(B, nb, Q_BLOCK, N_HEADS, QK_ROPE), (1, 0, 2, 3, 4))
    key_pos = jnp.arange(S)
    neg = jnp.finfo(jnp.float32).min

    def one_block(args):
        qn, qr, i = args
        q_pos = i * Q_BLOCK + jnp.arange(Q_BLOCK)
        limit = (q_pos // CHUNK + 1) * CHUNK
        s = (jnp.einsum('bqhd,bkhd->bhqk', qn, k_nope)
             + jnp.einsum('bqhr,bkr->bhqk', qr, kr)).astype(jnp.float32) * ATTN_SCALE
        s = jnp.where(key_pos[None, :] < limit[:, None], s, neg)
        p = jax.nn.softmax(s, axis=-1).astype(v.dtype)
        return jnp.einsum('bhqk,bkhd->bqhd', p, v)

    o = lax.map(one_block, (qn_b, qr_b, jnp.arange(nb)))
    return jnp.transpose(o, (1, 0, 2, 3, 4)).reshape(B, S, N_HEADS * V_DIM)


def mla_attend_sample(q_nope, q_rope, ckv_new, kr_new, past_ckv, past_kr, w_uk, w_uv):
    B, L = q_nope.shape[0], q_nope.shape[1]
    ckv_all = jnp.concatenate([past_ckv, ckv_new], axis=1)
    kr_all = jnp.concatenate([past_kr, kr_new], axis=1)
    q_lat = jnp.einsum('bqhd,rhd->bqhr', q_nope, w_uk.reshape(KV_RANK, N_HEADS, QK_NOPE))
    s = (jnp.einsum('bqhr,bkr->bhqk', q_lat, ckv_all)
         + jnp.einsum('bqhe,bke->bhqk', q_rope, kr_all)).astype(jnp.float32) * ATTN_SCALE
    p = jax.nn.softmax(s, axis=-1).astype(ckv_all.dtype)
    o_lat = jnp.einsum('bhqk,bkr->bqhr', p, ckv_all)
    o = jnp.einsum('bqhr,rhd->bqhd', o_lat, w_uv.reshape(KV_RANK, N_HEADS, V_DIM))
    return o.reshape(B, L, N_HEADS * V_DIM)


def mixer_sublayer(x, pos, conv_hist, past_ckv, past_kr, lw):
    z = x @ lw['w_in']
    uv, ab, cq, ckv_raw, kr_raw, g = jnp.split(z, [OFF_CV, OFF_Q, OFF_KV, OFF_KR, OFF_G], axis=-1)
    y_a, v_rows = gmlp_branch(uv, lw['gm_ln_g'], lw['gm_ln_b'], lw['gm_w_s'], lw['gm_b_s'], lw['gm_w_p'])
    y_b, conv_state = conv_branch(ab, conv_hist, lw['cv_w_dw'], lw['cv_b_dw'], lw['cv_ln_g'], lw['cv_ln_b'], lw['cv_w_pw'])
    q_nope, q_rope, ckv, kr = mla_project(cq, ckv_raw, kr_raw, pos, lw['mla_q_norm_g'], lw['mla_kv_norm_g'], lw['mla_w_uq'])
    if past_ckv is None:
        o = mla_attend_prompt(q_nope, q_rope, ckv, kr, lw['mla_w_uk'], lw['mla_w_uv'])
    else:
        o = mla_attend_sample(q_nope, q_rope, ckv, kr, past_ckv, past_kr, lw['mla_w_uk'], lw['mla_w_uv'])
    y_c = o @ lw['mla_w_o']
    g_a, g_b, g_c = jnp.split(jax.nn.sigmoid(g + lw['b_gate']), N_BRANCH, axis=-1)
    merged = g_a * y_a + g_b * y_b + g_c * y_c
    return merged @ lw['w_out'], v_rows, conv_state, ckv, kr


def swiglu(x, w1, w3, w2):
    return (jax.nn.silu(x @ w1) * (x @ w3)) @ w2


def moe_ffn(x, router, w1, w3, w2):
    shape = x.shape
    x2 = x.reshape(-1, D_MODEL)
    T = x2.shape[0]
    logits = (x2 @ router).astype(jnp.float32)
    top_vals, top_idx = lax.top_k(logits, TOP_K)
    gates = jax.nn.softmax(top_vals, axis=-1)
    n_assign = T * TOP_K
    e_flat = top_idx.reshape(-1)
    t_flat = jnp.repeat(jnp.arange(T), TOP_K)
    g_flat = gates.reshape(-1)
    order = jnp.argsort(e_flat)
    e_sorted = e_flat[order]
    counts = jnp.bincount(e_flat, length=N_EXPERTS)
    padded = ((counts + MOE_BLOCK - 1) // MOE_BLOCK) * MOE_BLOCK
    pad_end = jnp.cumsum(padded)
    pad_start = pad_end - padded
    raw_start = jnp.cumsum(counts) - counts
    dest = pad_start[e_sorted] + jnp.arange(n_assign) - raw_start[e_sorted]
    n_slots = ((n_assign + MOE_BLOCK - 1) // MOE_BLOCK + N_EXPERTS) * MOE_BLOCK
    slot_tok = jnp.zeros((n_slots,), jnp.int32).at[dest].set(t_flat[order].astype(jnp.int32))
    slot_w = jnp.zeros((n_slots,), jnp.float32).at[dest].set(g_flat[order])
    n_blocks = n_slots // MOE_BLOCK
    block_e = jnp.minimum(jnp.searchsorted(pad_end, jnp.arange(n_blocks) * MOE_BLOCK, side='right'), N_EXPERTS - 1)

    def run_block(args):
        toks, e = args
        xb = x2[toks]
        return (jax.nn.silu(xb @ w1[e]) * (xb @ w3[e])) @ w2[e]

    out = lax.map(run_block, (slot_tok.reshape(n_blocks, MOE_BLOCK), block_e)).reshape(n_slots, D_MODEL)
    out = out * slot_w[:, None].astype(out.dtype)
    return jax.ops.segment_sum(out, slot_tok, num_segments=T).reshape(shape)


def setup_inputs(seed: int = 0) -> dict:
    key = jax.random.key(seed)
    ks = jax.random.split(key, 40)
    f32 = jnp.float32

    def nrm(k, shape, scale):
        return jax.random.normal(k, shape, f32) * scale

    def gain(k, shape):
        return 1.0 + nrm(k, shape, 0.02)

    return {
        'x_prompt': nrm(ks[0], (BATCH, SEQ, D_MODEL), 1.0),
        'x_sample': nrm(ks[1], (DEC_BATCH, DEC_SEQ, D_MODEL), 1.0),
        'cache_mla_latent': nrm(ks[2], (DEPTH, DEC_BATCH, PAST_LEN, KV_RANK), 1.0),
        'cache_mla_krope': nrm(ks[3], (DEPTH, DEC_BATCH, PAST_LEN, QK_ROPE), 1.0),
        'cache_conv': nrm(ks[4], (DEPTH, DEC_BATCH, CV_KERNEL - 1, CV_WIDTH), 0.5),
        'w_in': nrm(ks[5], (DEPTH, D_MODEL, N_IN), D_MODEL ** -0.5),
        'b_gate': nrm(ks[6], (DEPTH, N_BRANCH * D_MODEL), 0.02),
        'gm_ln_g': gain(ks[7], (DEPTH, GM_WIDTH)),
        'gm_ln_b': nrm(ks[8], (DEPTH, GM_WIDTH), 0.02),
        'gm_w_s': nrm(ks[9], (DEPTH, GM_GROUPS, GM_CHUNK, GM_CHUNK), 0.5 * GM_CHUNK ** -0.5),
        'gm_b_s': gain(ks[10], (DEPTH, GM_GROUPS, GM_CHUNK)),
        'gm_w_p': nrm(ks[11], (DEPTH, GM_WIDTH, D_MODEL), GM_WIDTH ** -0.5),
        'cv_w_dw': nrm(ks[12], (DEPTH, CV_KERNEL, CV_WIDTH), CV_KERNEL ** -0.5),
        'cv_b_dw': nrm(ks[13], (DEPTH, CV_WIDTH), 0.02),
        'cv_ln_g': gain(ks[14], (DEPTH, CV_WIDTH)),
        'cv_ln_b': nrm(ks[15], (DEPTH, CV_WIDTH), 0.02),
        'cv_w_pw': nrm(ks[16], (DEPTH, CV_WIDTH, D_MODEL), CV_WIDTH ** -0.5),
        'mla_q_norm_g': gain(ks[17], (DEPTH, Q_RANK)),
        'mla_kv_norm_g': gain(ks[18], (DEPTH, KV_RANK)),
        'mla_w_uq': nrm(ks[19], (DEPTH, Q_RANK, N_HEADS * (QK_NOPE + QK_ROPE)), Q_RANK ** -0.5),
        'mla_w_uk': nrm(ks[20], (DEPTH, KV_RANK, N_HEADS * QK_NOPE), KV_RANK ** -0.5),
        'mla_w_uv': nrm(ks[21], (DEPTH, KV_RANK, N_HEADS * V_DIM), KV_RANK ** -0.5),
        'mla_w_o': nrm(ks[22], (DEPTH, N_HEADS * V_DIM, D_MODEL), (N_HEADS * V_DIM) ** -0.5),
        'w_out': nrm(ks[23], (DEPTH, D_MODEL, D_MODEL), BETA * D_MODEL ** -0.5),
        'ln1_g': gain(ks[24], (DEPTH, D_MODEL)),
        'ln1_b': nrm(ks[25], (DEPTH, D_MODEL), 0.02),
        'ln2_g': gain(ks[26], (DEPTH, D_MODEL)),
        'ln2_b': nrm(ks[27], (DEPTH, D_MODEL), 0.02),
        'ffn_w1': nrm(ks[28], (N_DENSE, D_MODEL, D_FF), D_MODEL ** -0.5),
        'ffn_w3': nrm(ks[29], (N_DENSE, D_MODEL, D_FF), D_MODEL ** -0.5),
        'ffn_w2': nrm(ks[30], (N_DENSE, D_FF, D_MODEL), BETA * D_FF ** -0.5),
        'moe_router': nrm(ks[31], (N_MOE, D_MODEL, N_EXPERTS), D_MODEL ** -0.5),
        'moe_w1': nrm(ks[32], (N_MOE, N_EXPERTS, D_MODEL, D_EXPERT), D_MODEL ** -0.5),
        'moe_w3': nrm(ks[33], (N_MOE, N_EXPERTS, D_MODEL, D_EXPERT), D_MODEL ** -0.5),
        'moe_w2': nrm(ks[34], (N_MOE, N_EXPERTS, D_EXPERT, D_MODEL), BETA * D_EXPERT ** -0.5),
    }


def reference(x_prompt, x_sample, cache_mla_latent, cache_mla_krope, cache_conv, w_in, b_gate,
              gm_ln_g, gm_ln_b, gm_w_s, gm_b_s, gm_w_p, cv_w_dw, cv_b_dw, cv_ln_g, cv_ln_b, cv_w_pw,
              mla_q_norm_g, mla_kv_norm_g, mla_w_uq, mla_w_uk, mla_w_uv, mla_w_o, w_out,
              ln1_g, ln1_b, ln2_g, ln2_b, ffn_w1, ffn_w3, ffn_w2, moe_router, moe_w1, moe_w3, moe_w2):
    B, S, _ = x_prompt.shape
    L = x_sample.shape[1]
    past_len = cache_mla_latent.shape[2]
    pos_p = jnp.arange(S)
    pos_s = past_len + jnp.arange(L)
    hist_p = jnp.zeros((B, CV_KERNEL - 1, CV_WIDTH), x_prompt.dtype)
    xp, xs = x_prompt, x_sample
    lat_p, kr_p_l, conv_p_l, lat_s, kr_s_l, conv_s_l, v_s_l = [], [], [], [], [], [], []
    for l in range(DEPTH):
        lw = {
            'w_in': w_in[l], 'b_gate': b_gate[l],
            'gm_ln_g': gm_ln_g[l], 'gm_ln_b': gm_ln_b[l], 'gm_w_s': gm_w_s[l], 'gm_b_s': gm_b_s[l], 'gm_w_p': gm_w_p[l],
            'cv_w_dw': cv_w_dw[l], 'cv_b_dw': cv_b_dw[l], 'cv_ln_g': cv_ln_g[l], 'cv_ln_b': cv_ln_b[l], 'cv_w_pw': cv_w_pw[l],
            'mla_q_norm_g': mla_q_norm_g[l], 'mla_kv_norm_g': mla_kv_norm_g[l], 'mla_w_uq': mla_w_uq[l],
            'mla_w_uk': mla_w_uk[l], 'mla_w_uv': mla_w_uv[l], 'mla_w_o': mla_w_o[l], 'w_out': w_out[l],
        }
        mp, _, conv_p, ckv_p, kr_p = mixer_sublayer(xp, pos_p, hist_p, None, None, lw)
        ms, v_s, conv_s, ckv_s, kr_s = mixer_sublayer(xs, pos_s, cache_conv[l], cache_mla_latent[l], cache_mla_krope[l], lw)
        xp = layer_norm(ALPHA * xp + mp, ln1_g[l], ln1_b[l])
        xs = layer_norm(ALPHA * xs + ms, ln1_g[l], ln1_b[l])
        j = l // 2
        if l % 2 == 0:
            fp = swiglu(xp, ffn_w1[j], ffn_w3[j], ffn_w2[j])
            fs = swiglu(xs, ffn_w1[j], ffn_w3[j], ffn_w2[j])
        else:
            fp = moe_ffn(xp, moe_router[j], moe_w1[j], moe_w3[j], moe_w2[j])
            fs = moe_ffn(xs, moe_router[j], moe_w1[j], moe_w3[j], moe_w2[j])
        xp = layer_norm(ALPHA * xp + fp, ln2_g[l], ln2_b[l])
        xs = layer_norm(ALPHA * xs + fs, ln2_g[l], ln2_b[l])
        lat_p.append(ckv_p)
        kr_p_l.append(kr_p)
        conv_p_l.append(conv_p)
        lat_s.append(ckv_s)
        kr_s_l.append(kr_s)
        conv_s_l.append(conv_s)
        v_s_l.append(v_s)
    return (xp, xs, jnp.stack(lat_p), jnp.stack(kr_p_l), jnp.stack(conv_p_l),
            jnp.stack(lat_s), jnp.stack(kr_s_l), jnp.stack(conv_s_l), jnp.stack(v_s_l))
```

```python
import functools
import math

import jax
import jax.numpy as jnp
from jax import lax
from jax.experimental import pallas as pl
from jax.experimental.pallas import tpu as pltpu

F32 = jnp.float32
BF16 = jnp.bfloat16

CHUNK = 64
GM_GROUPS = 8
GM_CHUNK = 128
CV_KERNEL = 31
N_HEADS = 16
QK_NOPE = 128
QK_ROPE = 64
V_DIM = 128
ROPE_THETA = 10000.0
TOP_K = 2
LN_EPS = 1e-5
RMS_EPS = 1e-6
ATTN_SCALE = (QK_NOPE + QK_ROPE) ** -0.5

V7X_VMEM_LIMIT_BYTES = 56 * 1024 * 1024
LANES = 128
CONV_HALO = 32
MOE_TM = 512


def _cp(*sem):
    return pltpu.CompilerParams(dimension_semantics=sem,
                                vmem_limit_bytes=V7X_VMEM_LIMIT_BYTES)


def _tile(n, pref, mult=8):
    if n <= pref:
        return n
    for t in range(pref, 0, -1):
        if n % t == 0 and t % mult == 0:
            return t
    return n


def _ln(x, g, b):
    mu = jnp.mean(x, axis=-1, keepdims=True)
    xc = x - mu
    var = jnp.mean(xc * xc, axis=-1, keepdims=True)
    return xc * lax.rsqrt(var + LN_EPS) * g + b


def _rms(x, g):
    ms = jnp.mean(x * x, axis=-1, keepdims=True)
    return x * lax.rsqrt(ms + RMS_EPS) * g


def _dot(a, b):
    return jnp.dot(a, b, preferred_element_type=F32)


def _dot_nt(a, b):
    return lax.dot_general(a, b, (((1,), (1,)), ((), ())), preferred_element_type=F32)


def _mm_act_kernel(x_ref, w_ref, b_ref, o_ref, *, act):
    acc = _dot(x_ref[...], w_ref[...]) + b_ref[...]
    if act == "gelu":
        acc = jax.nn.gelu(acc)
    elif act == "sigmoid":
        acc = jax.nn.sigmoid(acc)
    o_ref[...] = acc.astype(o_ref.dtype)


def _mm_act(x, w, b, act, out_dtype, tm=1024, tn=1024):
    M, K = x.shape
    N = w.shape[1]
    tm, tn = _tile(M, tm), _tile(N, tn, LANES)
    return pl.pallas_call(
        functools.partial(_mm_act_kernel, act=act),
        grid=(M // tm, N // tn),
        in_specs=[pl.BlockSpec((tm, K), lambda i, j: (i, 0)),
                  pl.BlockSpec((K, tn), lambda i, j: (0, j)),
                  pl.BlockSpec((1, tn), lambda i, j: (0, j))],
        out_specs=pl.BlockSpec((tm, tn), lambda i, j: (i, j)),
        out_shape=jax.ShapeDtypeStruct((M, N), out_dtype),
        compiler_params=_cp("parallel", "arbitrary"),
        name="mm_" + act,
    )(x, w, b)


def _mm_glu_kernel(x_ref, wa_ref, wb_ref, o_ref):
    x = x_ref[...]
    o_ref[...] = _dot(x, wa_ref[...]) * jax.nn.sigmoid(_dot(x, wb_ref[...]))


def _mm_glu(x, wa, wb, tm=1024, tn=512):
    M, K = x.shape
    N = wa.shape[1]
    tm, tn = _tile(M, tm), _tile(N, tn, LANES)
    return pl.pallas_call(
        _mm_glu_kernel,
        grid=(M // tm, N // tn),
        in_specs=[pl.BlockSpec((tm, K), lambda i, j: (i, 0)),
                  pl.BlockSpec((K, tn), lambda i, j: (0, j)),
                  pl.BlockSpec((K, tn), lambda i, j: (0, j))],
        out_specs=pl.BlockSpec((tm, tn), lambda i, j: (i, j)),
        out_shape=jax.ShapeDtypeStruct((M, N), F32),
        compiler_params=_cp("parallel", "arbitrary"),
        name="mm_glu",
    )(x, wa, wb)


def _latent_kernel(x_ref, wq_ref, wkv_ref, wkr_ref, gq_ref, gkv_ref, cos_ref, sin_ref,
                   cq_ref, ckv_ref, kr_ref):
    x = x_ref[...]
    cq_ref[...] = _rms(_dot(x, wq_ref[...]), gq_ref[...]).astype(cq_ref.dtype)
    ckv_ref[...] = _rms(_dot(x, wkv_ref[...]), gkv_ref[...])
    r = _dot(x, wkr_ref[...])
    kr_ref[...] = r[:, :QK_ROPE] * cos_ref[...] + r[:, QK_ROPE:] * sin_ref[...]


def _latents(x, wq, wkv, wkr2, gq, gkv, cos, sin, tm=512):
    M, K = x.shape
    Rq, Rkv = wq.shape[1], wkv.shape[1]
    tm = _tile(M, tm)
    row = lambda i: (i, 0)
    full = lambda i: (0, 0)
    return pl.pallas_call(
        _latent_kernel,
        grid=(M // tm,),
        in_specs=[pl.BlockSpec((tm, K), row),
                  pl.BlockSpec((K, Rq), full), pl.BlockSpec((K, Rkv), full),
                  pl.BlockSpec((K, 2 * QK_ROPE), full),
                  pl.BlockSpec((1, Rq), full), pl.BlockSpec((1, Rkv), full),
                  pl.BlockSpec((tm, QK_ROPE), row), pl.BlockSpec((tm, QK_ROPE), row)],
        out_specs=[pl.BlockSpec((tm, Rq), row), pl.BlockSpec((tm, Rkv), row),
                   pl.BlockSpec((tm, QK_ROPE), row)],
        out_shape=[jax.ShapeDtypeStruct((M, Rq), BF16), jax.ShapeDtypeStruct((M, Rkv), F32),
                   jax.ShapeDtypeStruct((M, QK_ROPE), F32)],
        compiler_params=_cp("parallel"),
        name="latents",
    )(x, wq, wkv, wkr2, gq, gkv, cos, sin)


def _gmlp_kernel(uv_ref, ws_ref, bs_ref, g_ref, b_ref, us_ref, v_ref, *, n_chunks, groups):
    W = v_ref.shape[1]
    gd = W // groups
    vn = _ln(uv_ref[:, W:], g_ref[...], b_ref[...])
    v_ref[...] = vn
    vb = vn.astype(BF16)
    for c in range(n_chunks):
        r0 = c * GM_CHUNK
        for g in range(groups):
            c0 = g * gd
            s = _dot(ws_ref[0, g], vb[r0:r0 + GM_CHUNK, c0:c0 + gd]) + bs_ref[0, :, c0:c0 + gd]
            u = uv_ref[r0:r0 + GM_CHUNK, c0:c0 + gd]
            us_ref[r0:r0 + GM_CHUNK, c0:c0 + gd] = (u * s).astype(us_ref.dtype)


def _gmlp(uvg, ws2, bs2, g, b, n_prompt_rows, tm=512):
    M, W2 = uvg.shape
    W = W2 // 2
    tm = _tile(math.gcd(n_prompt_rows, M - n_prompt_rows), tm, GM_CHUNK)
    npt = n_prompt_rows // tm
    sel = lambda i: (jnp.minimum(i // npt, 1), 0, 0, 0)
    sel3 = lambda i: (jnp.minimum(i // npt, 1), 0, 0)
    row = lambda i: (i, 0)
    full = lambda i: (0, 0)
    return pl.pallas_call(
        functools.partial(_gmlp_kernel, n_chunks=tm // GM_CHUNK, groups=GM_GROUPS),
        grid=(M // tm,),
        in_specs=[pl.BlockSpec((tm, W2), row),
                  pl.BlockSpec((1, GM_GROUPS, GM_CHUNK, GM_CHUNK), sel),
                  pl.BlockSpec((1, GM_CHUNK, W), sel3),
                  pl.BlockSpec((1, W), full), pl.BlockSpec((1, W), full)],
        out_specs=[pl.BlockSpec((tm, W), row), pl.BlockSpec((tm, W), row)],
        out_shape=[jax.ShapeDtypeStruct((M, W), BF16), jax.ShapeDtypeStruct((M, W), F32)],
        compiler_params=_cp("parallel"),
        name="gmlp",
    )(uvg, ws2, bs2, g, b)


def _conv_kernel(cur_ref, halo_ref, w_ref, bdw_ref, g_ref, b_ref, o_ref, full_ref, acc_ref,
                 *, tm, zero_first):
    C = cur_ref.shape[2]
    halo = halo_ref[0]
    if zero_first:
        halo = jnp.where(pl.program_id(1) == 0, 0.0, halo)
    full_ref[0:CONV_HALO, :] = halo
    full_ref[CONV_HALO:CONV_HALO + tm, :] = cur_ref[0]
    off = CONV_HALO - (CV_KERNEL - 1)
    RB = 8

    def lane_block(c, carry):
        c0 = pl.multiple_of(c * LANES, LANES)
        wk = w_ref[:, pl.ds(c0, LANES)]
        wrows = [jnp.broadcast_to(wk[k:k + 1, :], (RB, LANES)) for k in range(CV_KERNEL)]
        for rb in range(tm // RB):
            acc = jnp.zeros((RB, LANES), F32)
            for k in range(CV_KERNEL):
                acc = acc + full_ref[pl.ds(rb * RB + off + k, RB), pl.ds(c0, LANES)] * wrows[k]
            acc_ref[pl.ds(rb * RB, RB), pl.ds(c0, LANES)] = acc
        return carry

    lax.fori_loop(0, C // LANES, lane_block, 0)
    y = _ln(acc_ref[...] + bdw_ref[...], g_ref[...], b_ref[...])
    o_ref[0] = (y * jax.nn.sigmoid(y)).astype(o_ref.dtype)


def _conv(glu3, halo3, w_dw, b_dw, g, b, zero_first, tm=256):
    n_seq, L, C = glu3.shape
    tm = _tile(L, tm, CONV_HALO)
    hb = tm // CONV_HALO
    if zero_first:
        halo_map = lambda s, i: (s, jnp.maximum(i * hb - 1, 0), 0)
    else:
        halo_map = lambda s, i: (s, 0, 0)
    full = lambda s, i: (0, 0)
    return pl.pallas_call(
        functools.partial(_conv_kernel, tm=tm, zero_first=zero_first),
        grid=(n_seq, L // tm),
        in_specs=[pl.BlockSpec((1, tm, C), lambda s, i: (s, i, 0)),
                  pl.BlockSpec((1, CONV_HALO, C), halo_map),
                  pl.BlockSpec((CV_KERNEL, C), full),
                  pl.BlockSpec((1, C), full), pl.BlockSpec((1, C), full), pl.BlockSpec((1, C), full)],
        out_specs=pl.BlockSpec((1, tm, C), lambda s, i: (s, i, 0)),
        out_shape=jax.ShapeDtypeStruct((n_seq, L, C), BF16),
        scratch_shapes=[pltpu.VMEM((CONV_HALO + tm, C), F32), pltpu.VMEM((tm, C), F32)],
        compiler_params=_cp("parallel", "arbitrary"),
        name="conv_prompt" if zero_first else "conv_sample",
    )(glu3, halo3, w_dw, b_dw, g, b)


def _qproj_kernel(cq_ref, w_ref, cos_ref, sin_ref, q_ref, *, hg, nb, L):
    cq = cq_ref[...]
    cos, sin = cos_ref[...], sin_ref[...]
    for h in range(hg):
        r = _dot(cq, w_ref[h])
        nope = r[:, :QK_NOPE] * ATTN_SCALE
        rp = (r[:, QK_NOPE:QK_NOPE + QK_ROPE] * cos + r[:, QK_NOPE + QK_ROPE:] * sin) * ATTN_SCALE
        if nb == 1:
            q_ref[0, h, :, :QK_NOPE] = nope.astype(q_ref.dtype)
            q_ref[0, h, :, QK_NOPE:] = rp.astype(q_ref.dtype)
        else:
            q_ref[:, h, :, :QK_NOPE] = nope.reshape(nb, L, QK_NOPE).astype(q_ref.dtype)
            q_ref[:, h, :, QK_NOPE:] = rp.reshape(nb, L, QK_ROPE).astype(q_ref.dtype)


def _qproj(cqn, wq3, cos, sin, row0, n_seq, L, hg=4, tm=512):
    R = cqn.shape[1]
    H = wq3.shape[0]
    hg = _tile(H, hg, 1)
    dq = QK_NOPE + QK_ROPE
    if L >= tm:
        tm = _tile(L, tm)
        nb, lt = 1, L // tm
        out_spec = pl.BlockSpec((1, hg, tm, dq), lambda i, h: (i // lt, h, i % lt, 0))
    else:
        nb = _tile(n_seq, max(tm // L, 1), 1)
        tm = nb * L
        out_spec = pl.BlockSpec((nb, hg, L, dq), lambda i, h: (i, h, 0, 0))
    assert row0 % tm == 0
    r0 = row0 // tm
    row = lambda i, h: (r0 + i, 0)
    return pl.pallas_call(
        functools.partial(_qproj_kernel, hg=hg, nb=nb, L=L),
        grid=(n_seq * L // tm, H // hg),
        in_specs=[pl.BlockSpec((tm, R), row),
                  pl.BlockSpec((hg, R, wq3.shape[2]), lambda i, h: (h, 0, 0)),
                  pl.BlockSpec((tm, QK_ROPE), row), pl.BlockSpec((tm, QK_ROPE), row)],
        out_specs=out_spec,
        out_shape=jax.ShapeDtypeStruct((n_seq, H, L, dq), BF16),
        compiler_params=_cp("parallel", "arbitrary"),
        name="qproj",
    )(cqn, wq3, cos, sin)


def _kvproj_kernel(ckv_ref, kr_ref, w_ref, k_ref, v_ref, *, hg):
    ckv = ckv_ref[...].astype(BF16)
    kr = kr_ref[...].astype(BF16)
    for h in range(hg):
        r = _dot(ckv, w_ref[h])
        k_ref[0, h, :, :QK_NOPE] = r[:, :QK_NOPE].astype(k_ref.dtype)
        k_ref[0, h, :, QK_NOPE:] = kr
        v_ref[0, h] = r[:, QK_NOPE:].astype(v_ref.dtype)


def _kvproj(ckv, kr, wkv3, n_seq, L, hg=4, tm=512):
    R = ckv.shape[1]
    H = wkv3.shape[0]
    hg = _tile(H, hg, 1)
    tm = _tile(L, tm)
    lt = L // tm
    dq = QK_NOPE + QK_ROPE
    row = lambda i, h: (i, 0)
    omap = lambda i, h: (i // lt, h, i % lt, 0)
    return pl.pallas_call(
        functools.partial(_kvproj_kernel, hg=hg),
        grid=(n_seq * lt, H // hg),
        in_specs=[pl.BlockSpec((tm, R), row), pl.BlockSpec((tm, QK_ROPE), row),
                  pl.BlockSpec((hg, R, wkv3.shape[2]), lambda i, h: (h, 0, 0))],
        out_specs=[pl.BlockSpec((1, hg, tm, dq), omap), pl.BlockSpec((1, hg, tm, V_DIM), omap)],
        out_shape=[jax.ShapeDtypeStruct((n_seq, H, L, dq), BF16),
                   jax.ShapeDtypeStruct((n_seq, H, L, V_DIM), BF16)],
        compiler_params=_cp("parallel", "arbitrary"),
        name="kvproj",
    )(ckv, kr, wkv3)


def _attn_prompt_kernel(q_ref, k_ref, v_ref, o_ref, *, tq):
    qi = pl.program_id(2)
    q = q_ref[0, 0]

    def step(k, v, carry, mask):
        m, l, acc = carry
        s = _dot_nt(q, k)
        if mask is not None:
            s = jnp.where(mask, s, -jnp.inf)
        m_new = jnp.maximum(m, jnp.max(s, axis=-1, keepdims=True))
        alpha = jnp.exp(m - m_new)
        p = jnp.exp(s - m_new)
        l = alpha * l + jnp.sum(p, axis=-1, keepdims=True)
        acc = alpha * acc + _dot(p.astype(BF16), v)
        return m_new, l, acc

    def body(j, carry):
        j0 = pl.multiple_of(j * tq, tq)
        return step(k_ref[0, 0, pl.ds(j0, tq), :], v_ref[0, 0, pl.ds(j0, tq), :], carry, None)

    init = (jnp.full((tq, 1), -jnp.inf, F32), jnp.zeros((tq, 1), F32), jnp.zeros((tq, V_DIM), F32))
    carry = lax.fori_loop(0, qi, body, init)
    d0 = pl.multiple_of(qi * tq, tq)
    rows = lax.broadcasted_iota(jnp.int32, (tq, tq), 0) // CHUNK
    cols = lax.broadcasted_iota(jnp.int32, (tq, tq), 1) // CHUNK
    m, l, acc = step(k_ref[0, 0, pl.ds(d0, tq), :], v_ref[0, 0, pl.ds(d0, tq), :], carry, cols <= rows)
    o_ref[0] = (acc / l).astype(o_ref.dtype)


def _attn_prompt(q, k, v, tq=256):
    B, H, S, dq = q.shape
    tq = _tile(S, tq, CHUNK)
    return pl.pallas_call(
        functools.partial(_attn_prompt_kernel, tq=tq),
        grid=(B, H, S // tq),
        in_specs=[pl.BlockSpec((1, 1, tq, dq), lambda b, h, i: (b, h, i, 0)),
                  pl.BlockSpec((1, 1, S, dq), lambda b, h, i: (b, h, 0, 0)),
                  pl.BlockSpec((1, 1, S, V_DIM), lambda b, h, i: (b, h, 0, 0))],
        out_specs=pl.BlockSpec((1, tq, V_DIM), lambda b, h, i: (b, i, h)),
        out_shape=jax.ShapeDtypeStruct((B, S, H * V_DIM), BF16),
        compiler_params=_cp("parallel", "parallel", "arbitrary"),
        name="attn_prompt",
    )(q, k, v)


def _attn_sample_kernel(q_ref, plat_ref, pkr_ref, nlat_ref, nkr_ref, w_ref, o_ref, ql_ref, qr_ref,
                        *, H, L, kc):
    P = plat_ref.shape[2]
    for h in range(H):
        qh = q_ref[0, h]
        ql_ref[h * L:(h + 1) * L, :] = _dot_nt(qh[:, :QK_NOPE], w_ref[h, :, :QK_NOPE]).astype(BF16)
        qr_ref[h * L:(h + 1) * L, :] = qh[:, QK_NOPE:]
    ql = ql_ref[...]
    qr = qr_ref[...]

    def step(lat, kr, carry):
        m, l, acc = carry
        s = _dot_nt(ql, lat) + _dot_nt(qr, kr)
        m_new = jnp.maximum(m, jnp.max(s, axis=-1, keepdims=True))
        alpha = jnp.exp(m - m_new)
        p = jnp.exp(s - m_new)
        l = alpha * l + jnp.sum(p, axis=-1, keepdims=True)
        acc = alpha * acc + _dot(p.astype(BF16), lat)
        return m_new, l, acc

    R = ql.shape[1]
    carry = (jnp.full((H * L, 1), -jnp.inf, F32), jnp.zeros((H * L, 1), F32), jnp.zeros((H * L, R), F32))
    for c in range(P // kc):
        carry = step(plat_ref[0, 0, c * kc:(c + 1) * kc, :].astype(BF16),
                     pkr_ref[0, 0, c * kc:(c + 1) * kc, :].astype(BF16), carry)
    m, l, acc = step(nlat_ref[...].astype(BF16), nkr_ref[...].astype(BF16), carry)
    ol = (acc / l).astype(BF16)
    for h in range(H):
        o_ref[:, h * V_DIM:(h + 1) * V_DIM] = _dot(ol[h * L:(h + 1) * L, :], w_ref[h, :, QK_NOPE:]).astype(o_ref.dtype)


def _attn_sample(q, cache_lat, cache_kr, layer, ckv, kr, wkv3, row0):
    n_req, H, L, dq = q.shape
    P, R = cache_lat.shape[2], cache_lat.shape[3]
    kc = _tile(P, 1024)
    assert row0 % L == 0
    r0 = row0 // L
    return pl.pallas_call(
        functools.partial(_attn_sample_kernel, H=H, L=L, kc=kc),
        grid=(n_req,),
        in_specs=[pl.BlockSpec((1, H, L, dq), lambda b: (b, 0, 0, 0)),
                  pl.BlockSpec((1, 1, P, R), lambda b: (layer, b, 0, 0)),
                  pl.BlockSpec((1, 1, P, QK_ROPE), lambda b: (layer, b, 0, 0)),
                  pl.BlockSpec((L, R), lambda b: (r0 + b, 0)),
                  pl.BlockSpec((L, QK_ROPE), lambda b: (r0 + b, 0)),
                  pl.BlockSpec(wkv3.shape, lambda b: (0, 0, 0))],
        out_specs=pl.BlockSpec((L, H * V_DIM), lambda b: (b, 0)),
        out_shape=jax.ShapeDtypeStruct((n_req * L, H * V_DIM), BF16),
        scratch_shapes=[pltpu.VMEM((H * L, R), BF16), pltpu.VMEM((H * L, QK_ROPE), BF16)],
        compiler_params=_cp("parallel"),
        name="attn_sample",
    )(q, cache_lat, cache_kr, ckv, kr, wkv3)


def _merge_kernel(us_ref, yb_ref, o_ref, wp_ref, wpw_ref, wo_ref, ga_ref, gb_ref, gc_ref, out_ref):
    ya = _dot(us_ref[...], wp_ref[...])
    yb = _dot(yb_ref[...], wpw_ref[...])
    yc = _dot(o_ref[...], wo_ref[...])
    out = ga_ref[...].astype(F32) * ya + gb_ref[...].astype(F32) * yb + gc_ref[...].astype(F32) * yc
    out_ref[...] = out.astype(out_ref.dtype)


def _merge(us, yb, o, wp, wpw, wo, gates, tm=1024, tn=512):
    M = us.shape[0]
    D = wp.shape[1]
    tm, tn = _tile(M, tm), _tile(D, tn, LANES)
    nj = D // tn
    row = lambda i, j: (i, 0)
    col = lambda i, j: (0, j)
    return pl.pallas_call(
        _merge_kernel,
        grid=(M // tm, nj),
        in_specs=[pl.BlockSpec((tm, us.shape[1]), row), pl.BlockSpec((tm, yb.shape[1]), row),
                  pl.BlockSpec((tm, o.shape[1]), row),
                  pl.BlockSpec((wp.shape[0], tn), col), pl.BlockSpec((wpw.shape[0], tn), col),
                  pl.BlockSpec((wo.shape[0], tn), col),
                  pl.BlockSpec((tm, tn), lambda i, j: (i, j)),
                  pl.BlockSpec((tm, tn), lambda i, j: (i, nj + j)),
                  pl.BlockSpec((tm, tn), lambda i, j: (i, 2 * nj + j))],
        out_specs=pl.BlockSpec((tm, tn), lambda i, j: (i, j)),
        out_shape=jax.ShapeDtypeStruct((M, D), BF16),
        compiler_params=_cp("parallel", "arbitrary"),
        name="merge",
    )(us, yb, o, wp, wpw, wo, gates, gates, gates)


def _outproj_kernel(m_ref, w_ref, x_ref, g_ref, b_ref, o_ref, ob_ref, *, alpha):
    y = _ln(alpha * x_ref[...] + _dot(m_ref[...], w_ref[...]), g_ref[...], b_ref[...])
    o_ref[...] = y
    ob_ref[...] = y.astype(ob_ref.dtype)


def _outproj_ln(merged, w, x, g, b, alpha, tm=512):
    M, D = x.shape
    tm = _tile(M, tm)
    row = lambda i: (i, 0)
    full = lambda i: (0, 0)
    return pl.pallas_call(
        functools.partial(_outproj_kernel, alpha=alpha),
        grid=(M // tm,),
        in_specs=[pl.BlockSpec((tm, merged.shape[1]), row), pl.BlockSpec(w.shape, full),
                  pl.BlockSpec((tm, D), row), pl.BlockSpec((1, D), full), pl.BlockSpec((1, D), full)],
        out_specs=[pl.BlockSpec((tm, D), row), pl.BlockSpec((tm, D), row)],
        out_shape=[jax.ShapeDtypeStruct((M, D), F32), jax.ShapeDtypeStruct((M, D), BF16)],
        compiler_params=_cp("parallel"),
        name="outproj_ln",
    )(merged, w, x, g, b)


def _ffn_dense_kernel(eid_ref, nu_ref, x_ref, w1_ref, w3_ref, w2_ref, r_ref, g_ref, b_ref,
                      o_ref, ob_ref, acc_ref, *, alpha):
    f = pl.program_id(1)

    @pl.when(f == 0)
    def _():
        acc_ref[...] = jnp.zeros_like(acc_ref)

    x = x_ref[...]
    h = jax.nn.silu(_dot(x, w1_ref[0])) * _dot(x, w3_ref[0])
    acc_ref[...] += _dot(h.astype(BF16), w2_ref[0])

    @pl.when(f == pl.num_programs(1) - 1)
    def _():
        y = _ln(alpha * r_ref[...] + acc_ref[...], g_ref[...], b_ref[...])
        o_ref[...] = y
        ob_ref[...] = y.astype(ob_ref.dtype)


def _ffn_moe_kernel(eid_ref, nu_ref, x_ref, w1_ref, w3_ref, w2_ref, o_ref, xb_ref):
    i = pl.program_id(0)
    f = pl.program_id(1)

    @pl.when(f == 0)
    def _():
        o_ref[...] = jnp.zeros_like(o_ref)
        xb_ref[...] = x_ref[...].astype(BF16)

    @pl.when(i < nu_ref[0])
    def _():
        x = xb_ref[...]
        h = jax.nn.silu(_dot(x, w1_ref[0])) * _dot(x, w3_ref[0])
        o_ref[...] += _dot(h.astype(BF16), w2_ref[0])


def _ffn(x, eid, n_used, w1, w3, w2, tm, tf=512, dense=None):
    M, D = x.shape
    Fdim = w1.shape[2]
    tf = _tile(Fdim, tf, LANES)
    last = lambda i, nu: jnp.minimum(i, nu[0] - 1)
    xmap = lambda i, f, e, nu: (last(i, nu), 0)
    w13 = lambda i, f, e, nu: (e[last(i, nu)], 0, jnp.where(i < nu[0], f, 0))
    w2m = lambda i, f, e, nu: (e[last(i, nu)], jnp.where(i < nu[0], f, 0), 0)
    row = lambda i, f, e, nu: (i, 0)
    full = lambda i, f, e, nu: (0, 0)
    in_specs = [pl.BlockSpec((tm, D), xmap),
                pl.BlockSpec((1, D, tf), w13), pl.BlockSpec((1, D, tf), w13),
                pl.BlockSpec((1, tf, D), w2m)]
    if dense is not None:
        resid, g, b, alpha = dense
        kern = functools.partial(_ffn_dense_kernel, alpha=alpha)
        in_specs += [pl.BlockSpec((tm, D), row), pl.BlockSpec((1, D), full), pl.BlockSpec((1, D), full)]
        out_specs = [pl.BlockSpec((tm, D), row), pl.BlockSpec((tm, D), row)]
        out_shape = [jax.ShapeDtypeStruct((M, D), F32), jax.ShapeDtypeStruct((M, D), BF16)]
        scratch = [pltpu.VMEM((tm, D), F32)]
        args = (x, w1, w3, w2, resid, g, b)
    else:
        kern = _ffn_moe_kernel
        out_specs = pl.BlockSpec((tm, D), row)
        out_shape = jax.ShapeDtypeStruct((M, D), F32)
        scratch = [pltpu.VMEM((tm, D), BF16)]
        args = (x, w1, w3, w2)
    return pl.pallas_call(
        kern,
        grid_spec=pltpu.PrefetchScalarGridSpec(
            num_scalar_prefetch=2, grid=(M // tm, Fdim // tf),
            in_specs=in_specs, out_specs=out_specs, scratch_shapes=scratch),
        out_shape=out_shape,
        compiler_params=_cp("arbitrary", "arbitrary"),
        name="ffn_dense" if dense is not None else "ffn_moe",
    )(eid, n_used, *args)


def _router_kernel(x_ref, r_ref, o_ref, *, n_experts):
    logits = jnp.dot(x_ref[...], r_ref[...], preferred_element_type=F32,
                     precision=lax.Precision.HIGHEST)
    col = lax.broadcasted_iota(jnp.int32, logits.shape, 1)
    lg = jnp.where(col < n_experts, logits, -jnp.inf)
    m1 = jnp.max(lg, axis=-1, keepdims=True)
    i1 = jnp.min(jnp.where(lg == m1, col, LANES), axis=-1, keepdims=True)
    lg2 = jnp.where(col == i1, -jnp.inf, lg)
    m2 = jnp.max(lg2, axis=-1, keepdims=True)
    i2 = jnp.min(jnp.where(lg2 == m2, col, LANES), axis=-1, keepdims=True)
    e = jnp.exp(m2 - m1)
    g1 = 1.0 / (1.0 + e)
    g2 = e * g1
    out = jnp.where(col == 0, i1.astype(F32),
                    jnp.where(col == 1, i2.astype(F32),
                              jnp.where(col == 2, g1, jnp.where(col == 3, g2, 0.0))))
    o_ref[...] = out


def _router(x, router_pad, n_experts, tm=512):
    M, D = x.shape
    tm = _tile(M, tm)
    return pl.pallas_call(
        functools.partial(_router_kernel, n_experts=n_experts),
        grid=(M // tm,),
        in_specs=[pl.BlockSpec((tm, D), lambda i: (i, 0)), pl.BlockSpec((D, LANES), lambda i: (0, 0))],
        out_specs=pl.BlockSpec((tm, LANES), lambda i: (i, 0)),
        out_shape=jax.ShapeDtypeStruct((M, LANES), F32),
        compiler_params=_cp("parallel"),
        name="router",
    )(x, router_pad)


def _gather_kernel(idx_ref, src_ref, o_ref, sem, *, tm):
    def row_copy(r, t):
        return pltpu.make_async_copy(src_ref.at[pl.ds(t, 1), :], o_ref.at[pl.ds(r, 1), :], sem)

    def start(r, carry):
        row_copy(r, idx_ref[0, 0, r]).start()
        return carry

    def wait(r, carry):
        row_copy(r, 0).wait()
        return carry

    lax.fori_loop(0, tm, start, 0)
    lax.fori_loop(0, tm, wait, 0)


def _gather_rows(src, idx, tm=256):
    M = idx.shape[0]
    D = src.shape[1]
    tm = _tile(M, tm)
    idx3 = idx.reshape(M // tm, 1, tm)
    return pl.pallas_call(
        functools.partial(_gather_kernel, tm=tm),
        grid=(M // tm,),
        in_specs=[pl.BlockSpec((1, 1, tm), lambda i: (i, 0, 0), memory_space=pltpu.SMEM),
                  pl.BlockSpec(memory_space=pl.ANY)],
        out_specs=pl.BlockSpec((tm, D), lambda i: (i, 0)),
        out_shape=jax.ShapeDtypeStruct((M, D), src.dtype),
        scratch_shapes=[pltpu.SemaphoreType.DMA(())],
        compiler_params=_cp("arbitrary"),
        name="gather_rows",
    )(idx3, src)


def _combine_kernel(x_ref, y1_ref, y2_ref, r_ref, g_ref, b_ref, o_ref, ob_ref, *, alpha):
    r = r_ref[...]
    y = alpha * x_ref[...] + r[:, 2:3] * y1_ref[...] + r[:, 3:4] * y2_ref[...]
    y = _ln(y, g_ref[...], b_ref[...])
    o_ref[...] = y
    ob_ref[...] = y.astype(ob_ref.dtype)


def _combine(x, yg, route, g, b, alpha, tm=512):
    M, D = x.shape
    tm = _tile(M, tm)
    nt = M // tm
    row = lambda i: (i, 0)
    full = lambda i: (0, 0)
    return pl.pallas_call(
        functools.partial(_combine_kernel, alpha=alpha),
        grid=(nt,),
        in_specs=[pl.BlockSpec((tm, D), row), pl.BlockSpec((tm, D), row),
                  pl.BlockSpec((tm, D), lambda i: (nt + i, 0)),
                  pl.BlockSpec((tm, LANES), row), pl.BlockSpec((1, D), full), pl.BlockSpec((1, D), full)],
        out_specs=[pl.BlockSpec((tm, D), row), pl.BlockSpec((tm, D), row)],
        out_shape=[jax.ShapeDtypeStruct((M, D), F32), jax.ShapeDtypeStruct((M, D), BF16)],
        compiler_params=_cp("parallel"),
        name="combine",
    )(x, yg, yg, route, g, b)


def _moe_plan(route, n_experts, tm):
    T = route.shape[0]
    n_assign = T * TOP_K
    e_flat = route[:, :TOP_K].astype(jnp.int32).reshape(-1)
    onehot = (e_flat[:, None] == jnp.arange(n_experts, dtype=jnp.int32)[None, :]).astype(jnp.int32)
    csum = jnp.cumsum(onehot, axis=0)
    rank = jnp.take_along_axis(csum, e_flat[:, None], axis=1)[:, 0] - 1
    counts = csum[-1]
    padded = ((counts + tm - 1) // tm) * tm
    pad_end = jnp.cumsum(padded)
    pad_start = pad_end - padded
    dest = (pad_start[e_flat] + rank).astype(jnp.int32)
    n_blocks = -(-n_assign // tm) + n_experts
    n_slots = n_blocks * tm
    slot_tok = jnp.zeros((n_slots,), jnp.int32).at[dest].set(jnp.arange(n_assign, dtype=jnp.int32) // TOP_K)
    block_e = jnp.minimum(jnp.searchsorted(pad_end, jnp.arange(n_blocks, dtype=jnp.int32) * tm, side="right"),
                          n_experts - 1).astype(jnp.int32)
    n_used = (pad_end[-1:] // tm).astype(jnp.int32)
    dest_kt = dest.reshape(T, TOP_K).T.reshape(-1)
    return slot_tok, block_e, n_used, dest_kt


def _rot_half_cols(w):
    half = w.shape[-1] // 2
    return jnp.concatenate([-w[..., half:], w[..., :half]], axis=-1)


def kernel(x_prompt, x_sample, cache_mla_latent, cache_mla_krope, cache_conv, w_in, b_gate, gm_ln_g, gm_ln_b, gm_w_s, gm_b_s, gm_w_p, cv_w_dw, cv_b_dw, cv_ln_g, cv_ln_b, cv_w_pw, mla_q_norm_g, mla_kv_norm_g, mla_w_uq, mla_w_uk, mla_w_uv, mla_w_o, w_out, ln1_g, ln1_b, ln2_g, ln2_b, ffn_w1, ffn_w3, ffn_w2, moe_router, moe_w1, moe_w3, moe_w2):
    B, S, D = x_prompt.shape
    NB, L, _ = x_sample.shape
    depth = w_in.shape[0]
    past = cache_mla_latent.shape[2]
    GW = gm_ln_g.shape[1]
    CW = cv_ln_g.shape[1]
    RQ = mla_q_norm_g.shape[1]
    RKV = mla_kv_norm_g.shape[1]
    H = N_HEADS
    n_experts = moe_router.shape[2]
    alpha = float((2 * depth) ** 0.25)
    TP, TS = B * S, NB * L
    T = TP + TS
    off_cv = 2 * GW
    off_q = off_cv + 2 * CW
    off_kv = off_q + RQ
    off_kr = off_kv + RKV
    off_g = off_kr + QK_ROPE
    assert L <= GM_CHUNK and GM_CHUNK % L == 0 and S % GM_CHUNK == 0 and TS % GM_CHUNK == 0
    assert L >= CV_KERNEL - 1 and L % 8 == 0 and CV_KERNEL - 1 <= CONV_HALO

    w_uv = w_in[:, :, :off_cv].astype(BF16)
    w_a = w_in[:, :, off_cv:off_cv + CW].astype(BF16)
    w_b = w_in[:, :, off_cv + CW:off_q].astype(BF16)
    w_q = w_in[:, :, off_q:off_kv].astype(BF16)
    w_kv = w_in[:, :, off_kv:off_kr].astype(BF16)
    w_kr = w_in[:, :, off_kr:off_g]
    w_kr2 = jnp.concatenate([w_kr, _rot_half_cols(w_kr)], axis=-1).astype(BF16)
    w_g = w_in[:, :, off_g:].astype(BF16)
    zero_uv = jnp.zeros((1, off_cv), F32)

    uq = mla_w_uq.reshape(depth, RQ, H, QK_NOPE + QK_ROPE)
    uq_rope = uq[..., QK_NOPE:]
    wq3 = jnp.concatenate([uq[..., :QK_NOPE], uq_rope, _rot_half_cols(uq_rope)], axis=-1)
    wq3 = jnp.transpose(wq3, (0, 2, 1, 3)).astype(BF16)
    wkv3 = jnp.concatenate([mla_w_uk.reshape(depth, RKV, H, QK_NOPE),
                            mla_w_uv.reshape(depth, RKV, H, V_DIM)], axis=-1)
    wkv3 = jnp.transpose(wkv3, (0, 2, 1, 3)).astype(BF16)

    causal = jnp.tril(jnp.ones((GM_CHUNK, GM_CHUNK), bool))
    ws_p = jnp.where(causal, gm_w_s, 0.0)
    reps = GM_CHUNK // L
    ws_l = jnp.where(causal[:L, :L], gm_w_s[:, :, :L, :L], 0.0)
    ws_s = jnp.einsum("ab,lgij->lgaibj", jnp.eye(reps, dtype=F32), ws_l).reshape(depth, GM_GROUPS, GM_CHUNK, GM_CHUNK)
    ws2 = jnp.stack([ws_p, ws_s], axis=1).astype(BF16)
    gd = GW // GM_GROUPS
    bs_p = jnp.repeat(jnp.transpose(gm_b_s, (0, 2, 1)), gd, axis=2)
    bs_s = jnp.repeat(jnp.tile(jnp.transpose(gm_b_s[:, :, :L], (0, 2, 1)), (1, reps, 1)), gd, axis=2)
    bs2 = jnp.stack([bs_p, bs_s], axis=1)

    gm_w_p_b = gm_w_p.astype(BF16)
    cv_w_pw_b = cv_w_pw.astype(BF16)
    w_o_b = mla_w_o.astype(BF16)
    w_out_b = w_out.astype(BF16)
    ffn_w1_b, ffn_w3_b, ffn_w2_b = ffn_w1.astype(BF16), ffn_w3.astype(BF16), ffn_w2.astype(BF16)
    moe_w1_b, moe_w3_b, moe_w2_b = moe_w1.astype(BF16), moe_w3.astype(BF16), moe_w2.astype(BF16)
    router_pad = jnp.pad(moe_router, ((0, 0), (0, 0), (0, LANES - n_experts)))

    half = QK_ROPE // 2
    inv = ROPE_THETA ** (-jnp.arange(half, dtype=F32) / half)
    pos = jnp.concatenate([jnp.tile(jnp.arange(S), B), jnp.tile(past + jnp.arange(L), NB)]).astype(F32)
    ang = pos[:, None] * inv[None, :]
    cos = jnp.tile(jnp.cos(ang), (1, 2))
    sin = jnp.tile(jnp.sin(ang), (1, 2))

    hist_s = jnp.pad(cache_conv, ((0, 0), (0, 0), (CONV_HALO - (CV_KERNEL - 1), 0), (0, 0)))

    x = jnp.concatenate([x_prompt.reshape(TP, D), x_sample.reshape(TS, D)], axis=0)
    xb = x.astype(BF16)
    row2 = lambda a: a.reshape(1, -1)

    tm_dense = _tile(T, 512)
    eid_dense = jnp.zeros((T // tm_dense,), jnp.int32)
    nu_dense = jnp.full((1,), T // tm_dense, jnp.int32)

    outs = {k: [] for k in ("lat_p", "kr_p", "conv_p", "lat_s", "kr_s", "conv_s", "v_s")}
    for l in range(depth):
        uvg = _mm_act(xb, w_uv[l], zero_uv, "gelu", F32)
        glu = _mm_glu(xb, w_a[l], w_b[l])
        gates = _mm_act(xb, w_g[l], row2(b_gate[l]), "sigmoid", BF16)
        cqn, ckv, kr = _latents(xb, w_q[l], w_kv[l], w_kr2[l], row2(mla_q_norm_g[l]),
                                row2(mla_kv_norm_g[l]), cos, sin)

        us, v_ln = _gmlp(uvg, ws2[l], bs2[l], row2(gm_ln_g[l]), row2(gm_ln_b[l]), TP)

        cv_args = (cv_w_dw[l], row2(cv_b_dw[l]), row2(cv_ln_g[l]), row2(cv_ln_b[l]))
        glu_p = glu[:TP].reshape(B, S, CW)
        glu_s = glu[TP:].reshape(NB, L, CW)
        yb_p = _conv(glu_p, glu_p, *cv_args, zero_first=True)
        yb_s = _conv(glu_s, hist_s[l], *cv_args, zero_first=False)
        ybv = jnp.concatenate([yb_p.reshape(TP, CW), yb_s.reshape(TS, CW)], axis=0)

        q_p = _qproj(cqn, wq3[l], cos, sin, 0, B, S)
        q_s = _qproj(cqn, wq3[l], cos, sin, TP, NB, L)
        k_p, v_p = _kvproj(ckv, kr, wkv3[l], B, S)
        o_p = _attn_prompt(q_p, k_p, v_p)
        o_s = _attn_sample(q_s, cache_mla_latent, cache_mla_krope, l, ckv, kr, wkv3[l], TP)
        o = jnp.concatenate([o_p.reshape(TP, H * V_DIM), o_s], axis=0)

        merged = _merge(us, ybv, o, gm_w_p_b[l], cv_w_pw_b[l], w_o_b[l], gates)
        x, xb = _outproj_ln(merged, w_out_b[l], x, row2(ln1_g[l]), row2(ln1_b[l]), alpha)

        j = l // 2
        if l % 2 == 0:
            x, xb = _ffn(xb, eid_dense, nu_dense, ffn_w1_b[j][None], ffn_w3_b[j][None], ffn_w2_b[j][None],
                         tm_dense, dense=(x, row2(ln2_g[l]), row2(ln2_b[l]), alpha))
        else:
            route = _router(x, router_pad[j], n_experts)
            slot_tok, block_e, n_used, dest_kt = _moe_plan(route, n_experts, MOE_TM)
            xg = _gather_rows(x, slot_tok)
            y = _ffn(xg, block_e, n_used, moe_w1_b[j], moe_w3_b[j], moe_w2_b[j], MOE_TM)
            yg = _gather_rows(y, dest_kt)
            x, xb = _combine(x, yg, route, row2(ln2_g[l]), row2(ln2_b[l]), alpha)

        outs["lat_p"].append(ckv[:TP].reshape(B, S, RKV))
        outs["kr_p"].append(kr[:TP].reshape(B, S, QK_ROPE))
        outs["conv_p"].append(glu_p[:, S - (CV_KERNEL - 1):])
        outs["lat_s"].append(ckv[TP:].reshape(NB, L, RKV))
        outs["kr_s"].append(kr[TP:].reshape(NB, L, QK_ROPE))
        outs["conv_s"].append(glu_s[:, L - (CV_KERNEL - 1):])
        outs["v_s"].append(v_ln[TP:].reshape(NB, L, GW))

    return (x[:TP].reshape(B, S, D), x[TP:].reshape(NB, L, D),
            jnp.stack(outs["lat_p"]), jnp.stack(outs["kr_p"]), jnp.stack(outs["conv_p"]),
            jnp.stack(outs["lat_s"]), jnp.stack(outs["kr_s"]), jnp.stack(outs["conv_s"]),
            jnp.stack(outs["v_s"]))
```

```python
import functools
import math

import jax
import jax.numpy as jnp
from jax import lax
from jax.experimental import pallas as pl
from jax.experimental.pallas import tpu as pltpu

F32 = jnp.float32
BF16 = jnp.bfloat16

CHUNK = 64
GM_GROUPS = 8
GM_CHUNK = 128
CV_KERNEL = 31
N_HEADS = 16
QK_NOPE = 128
QK_ROPE = 64
V_DIM = 128
V_ONES = 16
ROPE_THETA = 10000.0
TOP_K = 2
LN_EPS = 1e-5
RMS_EPS = 1e-6
ATTN_SCALE = (QK_NOPE + QK_ROPE) ** -0.5

V7X_VMEM_LIMIT_BYTES = 56 * 1024 * 1024
LANES = 128
CONV_HALO = 32
MOE_TM = 512


def _cp(*sem):
    return pltpu.CompilerParams(dimension_semantics=sem,
                                vmem_limit_bytes=V7X_VMEM_LIMIT_BYTES)


def _tile(n, pref, mult=8):
    if n <= pref:
        return n
    for t in range(pref, 0, -1):
        if n % t == 0 and t % mult == 0:
            return t
    return n


def _ln(x, g, b):
    mu = jnp.mean(x, axis=-1, keepdims=True)
    xc = x - mu
    var = jnp.mean(xc * xc, axis=-1, keepdims=True)
    return xc * lax.rsqrt(var + LN_EPS) * g + b


def _rms(x, g):
    ms = jnp.mean(x * x, axis=-1, keepdims=True)
    return x * lax.rsqrt(ms + RMS_EPS) * g


def _dot(a, b):
    return jnp.dot(a, b, preferred_element_type=F32)


def _dot_nt(a, b):
    return lax.dot_general(a, b, (((1,), (1,)), ((), ())), preferred_element_type=F32)


def _mm_act_kernel(x_ref, w_ref, b_ref, o_ref, *, act):
    acc = _dot(x_ref[...], w_ref[...]) + b_ref[...]
    if act == "gelu":
        acc = jax.nn.gelu(acc)
    elif act == "sigmoid":
        acc = jax.nn.sigmoid(acc)
    o_ref[...] = acc.astype(o_ref.dtype)


def _mm_act(x, w, b, act, out_dtype, tm=1024, tn=1024):
    M, K = x.shape
    N = w.shape[1]
    tm, tn = _tile(M, tm), _tile(N, tn, LANES)
    return pl.pallas_call(
        functools.partial(_mm_act_kernel, act=act),
        grid=(M // tm, N // tn),
        in_specs=[pl.BlockSpec((tm, K), lambda i, j: (i, 0)),
                  pl.BlockSpec((K, tn), lambda i, j: (0, j)),
                  pl.BlockSpec((1, tn), lambda i, j: (0, j))],
        out_specs=pl.BlockSpec((tm, tn), lambda i, j: (i, j)),
        out_shape=jax.ShapeDtypeStruct((M, N), out_dtype),
        compiler_params=_cp("parallel", "arbitrary"),
        name="mm_" + act,
    )(x, w, b)


def _mm_glu_kernel(x_ref, wa_ref, wb_ref, o_ref):
    x = x_ref[...]
    o_ref[...] = _dot(x, wa_ref[...]) * jax.nn.sigmoid(_dot(x, wb_ref[...]))


def _mm_glu(x, wa, wb, tm=1024, tn=512):
    M, K = x.shape
    N = wa.shape[1]
    tm, tn = _tile(M, tm), _tile(N, tn, LANES)
    return pl.pallas_call(
        _mm_glu_kernel,
        grid=(M // tm, N // tn),
        in_specs=[pl.BlockSpec((tm, K), lambda i, j: (i, 0)),
                  pl.BlockSpec((K, tn), lambda i, j: (0, j)),
                  pl.BlockSpec((K, tn), lambda i, j: (0, j))],
        out_specs=pl.BlockSpec((tm, tn), lambda i, j: (i, j)),
        out_shape=jax.ShapeDtypeStruct((M, N), F32),
        compiler_params=_cp("parallel", "arbitrary"),
        name="mm_glu",
    )(x, wa, wb)


def _latent_kernel(x_ref, wq_ref, wkv_ref, wkr_ref, gq_ref, gkv_ref, cos_ref, sin_ref,
                   cq_ref, ckv_ref, kr_ref):
    x = x_ref[...]
    cq_ref[...] = _rms(_dot(x, wq_ref[...]), gq_ref[...]).astype(cq_ref.dtype)
    ckv_ref[...] = _rms(_dot(x, wkv_ref[...]), gkv_ref[...])
    r = _dot(x, wkr_ref[...])
    kr_ref[...] = r[:, :QK_ROPE] * cos_ref[...] + r[:, QK_ROPE:] * sin_ref[...]


def _latents(x, wq, wkv, wkr2, gq, gkv, cos, sin, tm=512):
    M, K = x.shape
    Rq, Rkv = wq.shape[1], wkv.shape[1]
    tm = _tile(M, tm)
    row = lambda i: (i, 0)
    full = lambda i: (0, 0)
    return pl.pallas_call(
        _latent_kernel,
        grid=(M // tm,),
        in_specs=[pl.BlockSpec((tm, K), row),
                  pl.BlockSpec((K, Rq), full), pl.BlockSpec((K, Rkv), full),
                  pl.BlockSpec((K, 2 * QK_ROPE), full),
                  pl.BlockSpec((1, Rq), full), pl.BlockSpec((1, Rkv), full),
                  pl.BlockSpec((tm, QK_ROPE), row), pl.BlockSpec((tm, QK_ROPE), row)],
        out_specs=[pl.BlockSpec((tm, Rq), row), pl.BlockSpec((tm, Rkv), row),
                   pl.BlockSpec((tm, QK_ROPE), row)],
        out_shape=[jax.ShapeDtypeStruct((M, Rq), BF16), jax.ShapeDtypeStruct((M, Rkv), F32),
                   jax.ShapeDtypeStruct((M, QK_ROPE), F32)],
        compiler_params=_cp("parallel"),
        name="latents",
    )(x, wq, wkv, wkr2, gq, gkv, cos, sin)


def _gmlp_kernel(uv_ref, ws_ref, bs_ref, g_ref, b_ref, us_ref, v_ref, *, n_chunks, groups):
    W = v_ref.shape[1]
    gd = W // groups
    vn = _ln(uv_ref[:, W:], g_ref[...], b_ref[...])
    v_ref[...] = vn
    vb = vn.astype(BF16)
    for c in range(n_chunks):
        r0 = c * GM_CHUNK
        for g in range(groups):
            c0 = g * gd
            s = _dot(ws_ref[0, g], vb[r0:r0 + GM_CHUNK, c0:c0 + gd]) + bs_ref[0, :, c0:c0 + gd]
            u = uv_ref[r0:r0 + GM_CHUNK, c0:c0 + gd]
            us_ref[r0:r0 + GM_CHUNK, c0:c0 + gd] = (u * s).astype(us_ref.dtype)


def _gmlp(uvg, ws2, bs2, g, b, n_prompt_rows, tm=512):
    M, W2 = uvg.shape
    W = W2 // 2
    tm = _tile(math.gcd(n_prompt_rows, M - n_prompt_rows), tm, GM_CHUNK)
    npt = n_prompt_rows // tm
    sel = lambda i: (jnp.minimum(i // npt, 1), 0, 0, 0)
    sel3 = lambda i: (jnp.minimum(i // npt, 1), 0, 0)
    row = lambda i: (i, 0)
    full = lambda i: (0, 0)
    return pl.pallas_call(
        functools.partial(_gmlp_kernel, n_chunks=tm // GM_CHUNK, groups=GM_GROUPS),
        grid=(M // tm,),
        in_specs=[pl.BlockSpec((tm, W2), row),
                  pl.BlockSpec((1, GM_GROUPS, GM_CHUNK, GM_CHUNK), sel),
                  pl.BlockSpec((1, GM_CHUNK, W), sel3),
                  pl.BlockSpec((1, W), full), pl.BlockSpec((1, W), full)],
        out_specs=[pl.BlockSpec((tm, W), row), pl.BlockSpec((tm, W), row)],
        out_shape=[jax.ShapeDtypeStruct((M, W), BF16), jax.ShapeDtypeStruct((M, W), F32)],
        compiler_params=_cp("parallel"),
        name="gmlp",
    )(uvg, ws2, bs2, g, b)


def _conv_kernel(cur_ref, halo_ref, w_ref, bdw_ref, g_ref, b_ref, o_ref, full_ref, acc_ref,
                 *, tm, zero_first):
    C = cur_ref.shape[2]
    halo = halo_ref[0]
    if zero_first:
        halo = jnp.where(pl.program_id(1) == 0, 0.0, halo)
    full_ref[0:CONV_HALO, :] = halo
    full_ref[CONV_HALO:CONV_HALO + tm, :] = cur_ref[0]
    off = CONV_HALO - (CV_KERNEL - 1)
    RB = 8

    def lane_block(c, carry):
        c0 = pl.multiple_of(c * LANES, LANES)
        wk = w_ref[:, pl.ds(c0, LANES)]
        wrows = [jnp.broadcast_to(wk[k:k + 1, :], (RB, LANES)) for k in range(CV_KERNEL)]
        for rb in range(tm // RB):
            acc = jnp.zeros((RB, LANES), F32)
            for k in range(CV_KERNEL):
                acc = acc + full_ref[pl.ds(rb * RB + off + k, RB), pl.ds(c0, LANES)] * wrows[k]
            acc_ref[pl.ds(rb * RB, RB), pl.ds(c0, LANES)] = acc
        return carry

    lax.fori_loop(0, C // LANES, lane_block, 0)
    y = _ln(acc_ref[...] + bdw_ref[...], g_ref[...], b_ref[...])
    o_ref[0] = (y * jax.nn.sigmoid(y)).astype(o_ref.dtype)


def _conv(glu3, halo3, w_dw, b_dw, g, b, zero_first, tm=256):
    n_seq, L, C = glu3.shape
    tm = _tile(L, tm, CONV_HALO)
    hb = tm // CONV_HALO
    if zero_first:
        halo_map = lambda s, i: (s, jnp.maximum(i * hb - 1, 0), 0)
    else:
        halo_map = lambda s, i: (s, 0, 0)
    full = lambda s, i: (0, 0)
    return pl.pallas_call(
        functools.partial(_conv_kernel, tm=tm, zero_first=zero_first),
        grid=(n_seq, L // tm),
        in_specs=[pl.BlockSpec((1, tm, C), lambda s, i: (s, i, 0)),
                  pl.BlockSpec((1, CONV_HALO, C), halo_map),
                  pl.BlockSpec((CV_KERNEL, C), full),
                  pl.BlockSpec((1, C), full), pl.BlockSpec((1, C), full), pl.BlockSpec((1, C), full)],
        out_specs=pl.BlockSpec((1, tm, C), lambda s, i: (s, i, 0)),
        out_shape=jax.ShapeDtypeStruct((n_seq, L, C), BF16),
        scratch_shapes=[pltpu.VMEM((CONV_HALO + tm, C), F32), pltpu.VMEM((tm, C), F32)],
        compiler_params=_cp("parallel", "arbitrary"),
        name="conv_prompt" if zero_first else "conv_sample",
    )(glu3, halo3, w_dw, b_dw, g, b)


def _qproj_kernel(cq_ref, w_ref, cos_ref, sin_ref, q_ref, *, hg, nb, L):
    cq = cq_ref[...]
    cos, sin = cos_ref[...], sin_ref[...]
    for h in range(hg):
        r = _dot(cq, w_ref[h])
        nope = r[:, :QK_NOPE] * ATTN_SCALE
        rp = (r[:, QK_NOPE:QK_NOPE + QK_ROPE] * cos + r[:, QK_NOPE + QK_ROPE:] * sin) * ATTN_SCALE
        if nb == 1:
            q_ref[0, h, :, :QK_NOPE] = nope.astype(q_ref.dtype)
            q_ref[0, h, :, QK_NOPE:] = rp.astype(q_ref.dtype)
        else:
            q_ref[:, h, :, :QK_NOPE] = nope.reshape(nb, L, QK_NOPE).astype(q_ref.dtype)
            q_ref[:, h, :, QK_NOPE:] = rp.reshape(nb, L, QK_ROPE).astype(q_ref.dtype)


def _qproj(cqn, wq3, cos, sin, row0, n_seq, L, hg=4, tm=512):
    R = cqn.shape[1]
    H = wq3.shape[0]
    hg = _tile(H, hg, 1)
    dq = QK_NOPE + QK_ROPE
    if L >= tm:
        tm = _tile(L, tm)
        nb, lt = 1, L // tm
        out_spec = pl.BlockSpec((1, hg, tm, dq), lambda i, h: (i // lt, h, i % lt, 0))
    else:
        nb = _tile(n_seq, max(tm // L, 1), 1)
        tm = nb * L
        out_spec = pl.BlockSpec((nb, hg, L, dq), lambda i, h: (i, h, 0, 0))
    assert row0 % tm == 0
    r0 = row0 // tm
    row = lambda i, h: (r0 + i, 0)
    return pl.pallas_call(
        functools.partial(_qproj_kernel, hg=hg, nb=nb, L=L),
        grid=(n_seq * L // tm, H // hg),
        in_specs=[pl.BlockSpec((tm, R), row),
                  pl.BlockSpec((hg, R, wq3.shape[2]), lambda i, h: (h, 0, 0)),
                  pl.BlockSpec((tm, QK_ROPE), row), pl.BlockSpec((tm, QK_ROPE), row)],
        out_specs=out_spec,
        out_shape=jax.ShapeDtypeStruct((n_seq, H, L, dq), BF16),
        compiler_params=_cp("parallel", "arbitrary"),
        name="qproj",
    )(cqn, wq3, cos, sin)


def _qproj_t_kernel(cq_ref, w_ref, cos_ref, sin_ref, q_ref, *, hg, scale):
    cq = cq_ref[...]
    cos, sin = cos_ref[...], sin_ref[...]
    for h in range(hg):
        r = _dot_nt(w_ref[h], cq)
        rp = r[QK_NOPE:QK_NOPE + QK_ROPE] * cos + r[QK_NOPE + QK_ROPE:] * sin
        q_ref[0, h, :QK_NOPE, :] = (r[:QK_NOPE] * scale).astype(q_ref.dtype)
        q_ref[0, h, QK_NOPE:, :] = (rp * scale).astype(q_ref.dtype)


def _qproj_t(cqn, wq3t, cos_t, sin_t, n_seq, L, scale, hg=4, tm=512):
    R = cqn.shape[1]
    H = wq3t.shape[0]
    hg = _tile(H, hg, 1)
    tm = _tile(L, tm, LANES)
    lt = L // tm
    dq = QK_NOPE + QK_ROPE
    return pl.pallas_call(
        functools.partial(_qproj_t_kernel, hg=hg, scale=scale),
        grid=(n_seq * lt, H // hg),
        in_specs=[pl.BlockSpec((tm, R), lambda i, h: (i, 0)),
                  pl.BlockSpec((hg, wq3t.shape[1], R), lambda i, h: (h, 0, 0)),
                  pl.BlockSpec((QK_ROPE, tm), lambda i, h: (0, i)),
                  pl.BlockSpec((QK_ROPE, tm), lambda i, h: (0, i))],
        out_specs=pl.BlockSpec((1, hg, dq, tm), lambda i, h: (i // lt, h, 0, i % lt)),
        out_shape=jax.ShapeDtypeStruct((n_seq, H, dq, L), BF16),
        compiler_params=_cp("parallel", "arbitrary"),
        name="qproj_t",
    )(cqn, wq3t, cos_t, sin_t)


def _kvproj_kernel(ckv_ref, kr_ref, wk_ref, wvt_ref, k_ref, vt_ref, *, hg):
    ckv = ckv_ref[...].astype(BF16)
    kr = kr_ref[...].astype(BF16)
    for h in range(hg):
        k_ref[0, h, :, :QK_NOPE] = _dot(ckv, wk_ref[h, :, :QK_NOPE]).astype(k_ref.dtype)
        k_ref[0, h, :, QK_NOPE:] = kr
        vt_ref[0, h, :V_DIM, :] = _dot_nt(wvt_ref[h], ckv).astype(vt_ref.dtype)
        vt_ref[0, h, V_DIM:, :] = jnp.ones((V_ONES, ckv.shape[0]), vt_ref.dtype)


def _kvproj(ckv, kr, wkv3, wvt3, n_seq, L, hg=4, tm=512):
    R = ckv.shape[1]
    H = wkv3.shape[0]
    hg = _tile(H, hg, 1)
    tm = _tile(L, tm, LANES)
    lt = L // tm
    dq = QK_NOPE + QK_ROPE
    row = lambda i, h: (i, 0)
    wmap = lambda i, h: (h, 0, 0)
    return pl.pallas_call(
        functools.partial(_kvproj_kernel, hg=hg),
        grid=(n_seq * lt, H // hg),
        in_specs=[pl.BlockSpec((tm, R), row), pl.BlockSpec((tm, QK_ROPE), row),
                  pl.BlockSpec((hg, R, wkv3.shape[2]), wmap), pl.BlockSpec((hg, V_DIM, R), wmap)],
        out_specs=[pl.BlockSpec((1, hg, tm, dq), lambda i, h: (i // lt, h, i % lt, 0)),
                   pl.BlockSpec((1, hg, V_DIM + V_ONES, tm), lambda i, h: (i // lt, h, 0, i % lt))],
        out_shape=[jax.ShapeDtypeStruct((n_seq, H, L, dq), BF16),
                   jax.ShapeDtypeStruct((n_seq, H, V_DIM + V_ONES, L), BF16)],
        compiler_params=_cp("parallel", "arbitrary"),
        name="kvproj",
    )(ckv, kr, wkv3, wvt3)


def _attn_prompt_kernel(qt_ref, k_ref, vt_ref, o_ref, s_ref, p_ref, acc_ref, *, tq, tk, hg):
    qi = pl.program_id(2)
    d0 = pl.multiple_of(qi * tq, tq)

    def scores(g, j0, slot):
        s = _dot(k_ref[0, g, pl.ds(j0, tk), :], qt_ref[0, g])
        s_ref[slot, g] = s
        return jnp.max(s, axis=0, keepdims=True)

    def softmax(g, slot, mx, m, mask=None):
        s = s_ref[slot, g]
        if mask is not None:
            s = jnp.where(mask, s, -jnp.inf)
            mx = jnp.max(s, axis=0, keepdims=True)
        m_new = jnp.maximum(m, mx)
        p_ref[slot, g] = jnp.exp2(s - m_new).astype(BF16)
        return jnp.exp2(m - m_new), m_new

    def accumulate(g, j0, slot, alpha):
        acc_ref[g] = alpha * acc_ref[g] + _dot(vt_ref[0, g, :, pl.ds(j0, tk)], p_ref[slot, g])

    def step(j0, slot, state, mask=None, last=False):
        jp = pl.multiple_of(jnp.maximum(j0 - tk, 0), tk)
        new = []
        for g in range(hg):
            mx, a_prev, m = state[g]
            mx_next = mx if last else scores(g, pl.multiple_of(j0 + tk, tk), 1 - slot)
            accumulate(g, jp, 1 - slot, a_prev)
            alpha, m = softmax(g, slot, mx, m, mask)
            new.append((mx_next, alpha, m))
        return tuple(new)

    def body(i, state):
        j0 = pl.multiple_of(2 * i * tk, tk)
        state = step(j0, 0, state)
        return step(pl.multiple_of(j0 + tk, tk), 1, state)

    p_ref[1] = jnp.zeros_like(p_ref[1])
    acc_ref[...] = jnp.zeros_like(acc_ref)
    init = tuple((scores(g, 0, 0), jnp.ones((1, tq), F32), jnp.full((1, tq), -jnp.inf, F32))
                 for g in range(hg))
    state = lax.fori_loop(0, qi * (tq // (2 * tk)), body, init)

    key_chunk = lax.broadcasted_iota(jnp.int32, (tk, tq), 0) // CHUNK
    qry_chunk = lax.broadcasted_iota(jnp.int32, (tk, tq), 1) // CHUNK
    state = step(d0, 0, state, mask=key_chunk <= qry_chunk)
    state = step(pl.multiple_of(d0 + tk, tk), 1, state, mask=key_chunk + tk // CHUNK <= qry_chunk, last=True)
    for g in range(hg):
        accumulate(g, pl.multiple_of(d0 + tk, tk), 1, state[g][1])
        o = acc_ref[g, :V_DIM, :] / acc_ref[g, V_DIM:V_DIM + 1, :]
        o_ref[0, :, g * V_DIM:(g + 1) * V_DIM] = o.T.astype(o_ref.dtype)


def _attn_prompt(qt, k, vt, tq=512, hg=2):
    B, H, dq, S = qt.shape
    tq = _tile(S, tq, 2 * LANES)
    tk = tq // 2
    hg = _tile(H, hg, 1)
    return pl.pallas_call(
        functools.partial(_attn_prompt_kernel, tq=tq, tk=tk, hg=hg),
        grid=(B, H // hg, S // tq),
        in_specs=[pl.BlockSpec((1, hg, dq, tq), lambda b, h, i: (b, h, 0, i)),
                  pl.BlockSpec((1, hg, S, dq), lambda b, h, i: (b, h, 0, 0)),
                  pl.BlockSpec((1, hg, vt.shape[2], S), lambda b, h, i: (b, h, 0, 0))],
        out_specs=pl.BlockSpec((1, tq, hg * V_DIM), lambda b, h, i: (b, i, h)),
        out_shape=jax.ShapeDtypeStruct((B, S, H * V_DIM), BF16),
        scratch_shapes=[pltpu.VMEM((2, hg, tk, tq), F32), pltpu.VMEM((2, hg, tk, tq), BF16),
                        pltpu.VMEM((hg, vt.shape[2], tq), F32)],
        compiler_params=_cp("parallel", "parallel", "arbitrary"),
        name="attn_prompt",
    )(qt, k, vt)


def _attn_sample_kernel(q_ref, plat_ref, pkr_ref, nlat_ref, nkr_ref, w_ref, o_ref, ql_ref, qr_ref,
                        *, H, L, kc):
    P = plat_ref.shape[2]
    for h in range(H):
        qh = q_ref[0, h]
        ql_ref[h * L:(h + 1) * L, :] = _dot_nt(qh[:, :QK_NOPE], w_ref[h, :, :QK_NOPE]).astype(BF16)
        qr_ref[h * L:(h + 1) * L, :] = qh[:, QK_NOPE:]
    ql = ql_ref[...]
    qr = qr_ref[...]

    def step(lat, kr, carry):
        m, l, acc = carry
        s = _dot_nt(ql, lat) + _dot_nt(qr, kr)
        m_new = jnp.maximum(m, jnp.max(s, axis=-1, keepdims=True))
        alpha = jnp.exp(m - m_new)
        p = jnp.exp(s - m_new)
        l = alpha * l + jnp.sum(p, axis=-1, keepdims=True)
        acc = alpha * acc + _dot(p.astype(BF16), lat)
        return m_new, l, acc

    R = ql.shape[1]
    carry = (jnp.full((H * L, 1), -jnp.inf, F32), jnp.zeros((H * L, 1), F32), jnp.zeros((H * L, R), F32))
    for c in range(P // kc):
        carry = step(plat_ref[0, 0, c * kc:(c + 1) * kc, :].astype(BF16),
                     pkr_ref[0, 0, c * kc:(c + 1) * kc, :].astype(BF16), carry)
    m, l, acc = step(nlat_ref[...].astype(BF16), nkr_ref[...].astype(BF16), carry)
    ol = (acc / l).astype(BF16)
    for h in range(H):
        o_ref[:, h * V_DIM:(h + 1) * V_DIM] = _dot(ol[h * L:(h + 1) * L, :], w_ref[h, :, QK_NOPE:]).astype(o_ref.dtype)


def _attn_sample(q, cache_lat, cache_kr, layer, ckv, kr, wkv3, row0):
    n_req, H, L, dq = q.shape
    P, R = cache_lat.shape[2], cache_lat.shape[3]
    kc = _tile(P, 1024)
    assert row0 % L == 0
    r0 = row0 // L
    return pl.pallas_call(
        functools.partial(_attn_sample_kernel, H=H, L=L, kc=kc),
        grid=(n_req,),
        in_specs=[pl.BlockSpec((1, H, L, dq), lambda b: (b, 0, 0, 0)),
                  pl.BlockSpec((1, 1, P, R), lambda b: (layer, b, 0, 0)),
                  pl.BlockSpec((1, 1, P, QK_ROPE), lambda b: (layer, b, 0, 0)),
                  pl.BlockSpec((L, R), lambda b: (r0 + b, 0)),
                  pl.BlockSpec((L, QK_ROPE), lambda b: (r0 + b, 0)),
                  pl.BlockSpec(wkv3.shape, lambda b: (0, 0, 0))],
        out_specs=pl.BlockSpec((L, H * V_DIM), lambda b: (b, 0)),
        out_shape=jax.ShapeDtypeStruct((n_req * L, H * V_DIM), BF16),
        scratch_shapes=[pltpu.VMEM((H * L, R), BF16), pltpu.VMEM((H * L, QK_ROPE), BF16)],
        compiler_params=_cp("parallel"),
        name="attn_sample",
    )(q, cache_lat, cache_kr, ckv, kr, wkv3)


def _merge_kernel(us_ref, yb_ref, o_ref, wp_ref, wpw_ref, wo_ref, ga_ref, gb_ref, gc_ref, out_ref):
    ya = _dot(us_ref[...], wp_ref[...])
    yb = _dot(yb_ref[...], wpw_ref[...])
    yc = _dot(o_ref[...], wo_ref[...])
    out = ga_ref[...].astype(F32) * ya + gb_ref[...].astype(F32) * yb + gc_ref[...].astype(F32) * yc
    out_ref[...] = out.astype(out_ref.dtype)


def _merge(us, yb, o, wp, wpw, wo, gates, tm=1024, tn=512):
    M = us.shape[0]
    D = wp.shape[1]
    tm, tn = _tile(M, tm), _tile(D, tn, LANES)
    nj = D // tn
    row = lambda i, j: (i, 0)
    col = lambda i, j: (0, j)
    return pl.pallas_call(
        _merge_kernel,
        grid=(M // tm, nj),
        in_specs=[pl.BlockSpec((tm, us.shape[1]), row), pl.BlockSpec((tm, yb.shape[1]), row),
                  pl.BlockSpec((tm, o.shape[1]), row),
                  pl.BlockSpec((wp.shape[0], tn), col), pl.BlockSpec((wpw.shape[0], tn), col),
                  pl.BlockSpec((wo.shape[0], tn), col),
                  pl.BlockSpec((tm, tn), lambda i, j: (i, j)),
                  pl.BlockSpec((tm, tn), lambda i, j: (i, nj + j)),
                  pl.BlockSpec((tm, tn), lambda i, j: (i, 2 * nj + j))],
        out_specs=pl.BlockSpec((tm, tn), lambda i, j: (i, j)),
        out_shape=jax.ShapeDtypeStruct((M, D), BF16),
        compiler_params=_cp("parallel", "arbitrary"),
        name="merge",
    )(us, yb, o, wp, wpw, wo, gates, gates, gates)


def _outproj_kernel(m_ref, w_ref, x_ref, g_ref, b_ref, o_ref, ob_ref, *, alpha):
    y = _ln(alpha * x_ref[...] + _dot(m_ref[...], w_ref[...]), g_ref[...], b_ref[...])
    o_ref[...] = y
    ob_ref[...] = y.astype(ob_ref.dtype)


def _outproj_ln(merged, w, x, g, b, alpha, tm=512):
    M, D = x.shape
    tm = _tile(M, tm)
    row = lambda i: (i, 0)
    full = lambda i: (0, 0)
    return pl.pallas_call(
        functools.partial(_outproj_kernel, alpha=alpha),
        grid=(M // tm,),
        in_specs=[pl.BlockSpec((tm, merged.shape[1]), row), pl.BlockSpec(w.shape, full),
                  pl.BlockSpec((tm, D), row), pl.BlockSpec((1, D), full), pl.BlockSpec((1, D), full)],
        out_specs=[pl.BlockSpec((tm, D), row), pl.BlockSpec((tm, D), row)],
        out_shape=[jax.ShapeDtypeStruct((M, D), F32), jax.ShapeDtypeStruct((M, D), BF16)],
        compiler_params=_cp("parallel"),
        name="outproj_ln",
    )(merged, w, x, g, b)


def _ffn_dense_kernel(eid_ref, nu_ref, x_ref, w1_ref, w3_ref, w2_ref, r_ref, g_ref, b_ref,
                      o_ref, ob_ref, acc_ref, *, alpha):
    f = pl.program_id(1)

    @pl.when(f == 0)
    def _():
        acc_ref[...] = jnp.zeros_like(acc_ref)

    x = x_ref[...]
    h = jax.nn.silu(_dot(x, w1_ref[0])) * _dot(x, w3_ref[0])
    acc_ref[...] += _dot(h.astype(BF16), w2_ref[0])

    @pl.when(f == pl.num_programs(1) - 1)
    def _():
        y = _ln(alpha * r_ref[...] + acc_ref[...], g_ref[...], b_ref[...])
        o_ref[...] = y
        ob_ref[...] = y.astype(ob_ref.dtype)


def _ffn_moe_kernel(eid_ref, nu_ref, x_ref, w1_ref, w3_ref, w2_ref, o_ref, xb_ref):
    i = pl.program_id(0)
    f = pl.program_id(1)

    @pl.when(f == 0)
    def _():
        o_ref[...] = jnp.zeros_like(o_ref)
        xb_ref[...] = x_ref[...].astype(BF16)

    @pl.when(i < nu_ref[0])
    def _():
        x = xb_ref[...]
        h = jax.nn.silu(_dot(x, w1_ref[0])) * _dot(x, w3_ref[0])
        o_ref[...] += _dot(h.astype(BF16), w2_ref[0])


def _ffn(x, eid, n_used, w1, w3, w2, tm, tf=512, dense=None):
    M, D = x.shape
    Fdim = w1.shape[2]
    tf = _tile(Fdim, tf, LANES)
    last = lambda i, nu: jnp.minimum(i, nu[0] - 1)
    xmap = lambda i, f, e, nu: (last(i, nu), 0)
    w13 = lambda i, f, e, nu: (e[last(i, nu)], 0, jnp.where(i < nu[0], f, 0))
    w2m = lambda i, f, e, nu: (e[last(i, nu)], jnp.where(i < nu[0], f, 0), 0)
    row = lambda i, f, e, nu: (i, 0)
    full = lambda i, f, e, nu: (0, 0)
    in_specs = [pl.BlockSpec((tm, D), xmap),
                pl.BlockSpec((1, D, tf), w13), pl.BlockSpec((1, D, tf), w13),
                pl.BlockSpec((1, tf, D), w2m)]
    if dense is not None:
        resid, g, b, alpha = dense
        kern = functools.partial(_ffn_dense_kernel, alpha=alpha)
        in_specs += [pl.BlockSpec((tm, D), row), pl.BlockSpec((1, D), full), pl.BlockSpec((1, D), full)]
        out_specs = [pl.BlockSpec((tm, D), row), pl.BlockSpec((tm, D), row)]
        out_shape = [jax.ShapeDtypeStruct((M, D), F32), jax.ShapeDtypeStruct((M, D), BF16)]
        scratch = [pltpu.VMEM((tm, D), F32)]
        args = (x, w1, w3, w2, resid, g, b)
    else:
        kern = _ffn_moe_kernel
        out_specs = pl.BlockSpec((tm, D), row)
        out_shape = jax.ShapeDtypeStruct((M, D), F32)
        scratch = [pltpu.VMEM((tm, D), BF16)]
        args = (x, w1, w3, w2)
    return pl.pallas_call(
        kern,
        grid_spec=pltpu.PrefetchScalarGridSpec(
            num_scalar_prefetch=2, grid=(M // tm, Fdim // tf),
            in_specs=in_specs, out_specs=out_specs, scratch_shapes=scratch),
        out_shape=out_shape,
        compiler_params=_cp("arbitrary", "arbitrary"),
        name="ffn_dense" if dense is not None else "ffn_moe",
    )(eid, n_used, *args)


def _router_kernel(x_ref, r_ref, o_ref, *, n_experts):
    logits = jnp.dot(x_ref[...], r_ref[...], preferred_element_type=F32,
                     precision=lax.Precision.HIGHEST)
    col = lax.broadcasted_iota(jnp.int32, logits.shape, 1)
    lg = jnp.where(col < n_experts, logits, -jnp.inf)
    m1 = jnp.max(lg, axis=-1, keepdims=True)
    i1 = jnp.min(jnp.where(lg == m1, col, LANES), axis=-1, keepdims=True)
    lg2 = jnp.where(col == i1, -jnp.inf, lg)
    m2 = jnp.max(lg2, axis=-1, keepdims=True)
    i2 = jnp.min(jnp.where(lg2 == m2, col, LANES), axis=-1, keepdims=True)
    e = jnp.exp(m2 - m1)
    g1 = 1.0 / (1.0 + e)
    g2 = e * g1
    out = jnp.where(col == 0, i1.astype(F32),
                    jnp.where(col == 1, i2.astype(F32),
                              jnp.where(col == 2, g1, jnp.where(col == 3, g2, 0.0))))
    o_ref[...] = out


def _router(x, router_pad, n_experts, tm=512):
    M, D = x.shape
    tm = _tile(M, tm)
    return pl.pallas_call(
        functools.partial(_router_kernel, n_experts=n_experts),
        grid=(M // tm,),
        in_specs=[pl.BlockSpec((tm, D), lambda i: (i, 0)), pl.BlockSpec((D, LANES), lambda i: (0, 0))],
        out_specs=pl.BlockSpec((tm, LANES), lambda i: (i, 0)),
        out_shape=jax.ShapeDtypeStruct((M, LANES), F32),
        compiler_params=_cp("parallel"),
        name="router",
    )(x, router_pad)


def _gather_kernel(idx_ref, src_ref, o_ref, sem, *, tm):
    def row_copy(r, t):
        return pltpu.make_async_copy(src_ref.at[pl.ds(t, 1), :], o_ref.at[pl.ds(r, 1), :], sem)

    def start(r, carry):
        row_copy(r, idx_ref[0, 0, r]).start()
        return carry

    def wait(r, carry):
        row_copy(r, 0).wait()
        return carry

    lax.fori_loop(0, tm, start, 0)
    lax.fori_loop(0, tm, wait, 0)


def _gather_rows(src, idx, tm=256):
    M = idx.shape[0]
    D = src.shape[1]
    tm = _tile(M, tm)
    idx3 = idx.reshape(M // tm, 1, tm)
    return pl.pallas_call(
        functools.partial(_gather_kernel, tm=tm),
        grid=(M // tm,),
        in_specs=[pl.BlockSpec((1, 1, tm), lambda i: (i, 0, 0), memory_space=pltpu.SMEM),
                  pl.BlockSpec(memory_space=pl.ANY)],
        out_specs=pl.BlockSpec((tm, D), lambda i: (i, 0)),
        out_shape=jax.ShapeDtypeStruct((M, D), src.dtype),
        scratch_shapes=[pltpu.SemaphoreType.DMA(())],
        compiler_params=_cp("arbitrary"),
        name="gather_rows",
    )(idx3, src)


def _combine_kernel(x_ref, y1_ref, y2_ref, r_ref, g_ref, b_ref, o_ref, ob_ref, *, alpha):
    r = r_ref[...]
    y = alpha * x_ref[...] + r[:, 2:3] * y1_ref[...] + r[:, 3:4] * y2_ref[...]
    y = _ln(y, g_ref[...], b_ref[...])
    o_ref[...] = y
    ob_ref[...] = y.astype(ob_ref.dtype)


def _combine(x, yg, route, g, b, alpha, tm=512):
    M, D = x.shape
    tm = _tile(M, tm)
    nt = M // tm
    row = lambda i: (i, 0)
    full = lambda i: (0, 0)
    return pl.pallas_call(
        functools.partial(_combine_kernel, alpha=alpha),
        grid=(nt,),
        in_specs=[pl.BlockSpec((tm, D), row), pl.BlockSpec((tm, D), row),
                  pl.BlockSpec((tm, D), lambda i: (nt + i, 0)),
                  pl.BlockSpec((tm, LANES), row), pl.BlockSpec((1, D), full), pl.BlockSpec((1, D), full)],
        out_specs=[pl.BlockSpec((tm, D), row), pl.BlockSpec((tm, D), row)],
        out_shape=[jax.ShapeDtypeStruct((M, D), F32), jax.ShapeDtypeStruct((M, D), BF16)],
        compiler_params=_cp("parallel"),
        name="combine",
    )(x, yg, yg, route, g, b)


def _moe_plan(route, n_experts, tm):
    T = route.shape[0]
    n_assign = T * TOP_K
    e_flat = route[:, :TOP_K].astype(jnp.int32).reshape(-1)
    onehot = (e_flat[:, None] == jnp.arange(n_experts, dtype=jnp.int32)[None, :]).astype(jnp.int32)
    csum = jnp.cumsum(onehot, axis=0)
    rank = jnp.take_along_axis(csum, e_flat[:, None], axis=1)[:, 0] - 1
    counts = csum[-1]
    padded = ((counts + tm - 1) // tm) * tm
    pad_end = jnp.cumsum(padded)
    pad_start = pad_end - padded
    dest = (pad_start[e_flat] + rank).astype(jnp.int32)
    n_blocks = -(-n_assign // tm) + n_experts
    n_slots = n_blocks * tm
    slot_tok = jnp.zeros((n_slots,), jnp.int32).at[dest].set(jnp.arange(n_assign, dtype=jnp.int32) // TOP_K)
    block_e = jnp.minimum(jnp.searchsorted(pad_end, jnp.arange(n_blocks, dtype=jnp.int32) * tm, side="right"),
                          n_experts - 1).astype(jnp.int32)
    n_used = (pad_end[-1:] // tm).astype(jnp.int32)
    dest_kt = dest.reshape(T, TOP_K).T.reshape(-1)
    return slot_tok, block_e, n_used, dest_kt


def _rot_half_cols(w):
    half = w.shape[-1] // 2
    return jnp.concatenate([-w[..., half:], w[..., :half]], axis=-1)


def kernel(x_prompt, x_sample, cache_mla_latent, cache_mla_krope, cache_conv, w_in, b_gate, gm_ln_g, gm_ln_b, gm_w_s, gm_b_s, gm_w_p, cv_w_dw, cv_b_dw, cv_ln_g, cv_ln_b, cv_w_pw, mla_q_norm_g, mla_kv_norm_g, mla_w_uq, mla_w_uk, mla_w_uv, mla_w_o, w_out, ln1_g, ln1_b, ln2_g, ln2_b, ffn_w1, ffn_w3, ffn_w2, moe_router, moe_w1, moe_w3, moe_w2):
    B, S, D = x_prompt.shape
    NB, L, _ = x_sample.shape
    depth = w_in.shape[0]
    past = cache_mla_latent.shape[2]
    GW = gm_ln_g.shape[1]
    CW = cv_ln_g.shape[1]
    RQ = mla_q_norm_g.shape[1]
    RKV = mla_kv_norm_g.shape[1]
    H = N_HEADS
    n_experts = moe_router.shape[2]
    alpha = float((2 * depth) ** 0.25)
    TP, TS = B * S, NB * L
    T = TP + TS
    off_cv = 2 * GW
    off_q = off_cv + 2 * CW
    off_kv = off_q + RQ
    off_kr = off_kv + RKV
    off_g = off_kr + QK_ROPE
    assert L <= GM_CHUNK and GM_CHUNK % L == 0 and S % GM_CHUNK == 0 and TS % GM_CHUNK == 0
    assert L >= CV_KERNEL - 1 and L % 8 == 0 and CV_KERNEL - 1 <= CONV_HALO

    w_uv = w_in[:, :, :off_cv].astype(BF16)
    w_a = w_in[:, :, off_cv:off_cv + CW].astype(BF16)
    w_b = w_in[:, :, off_cv + CW:off_q].astype(BF16)
    w_q = w_in[:, :, off_q:off_kv].astype(BF16)
    w_kv = w_in[:, :, off_kv:off_kr].astype(BF16)
    w_kr = w_in[:, :, off_kr:off_g]
    w_kr2 = jnp.concatenate([w_kr, _rot_half_cols(w_kr)], axis=-1).astype(BF16)
    w_g = w_in[:, :, off_g:].astype(BF16)
    zero_uv = jnp.zeros((1, off_cv), F32)

    uq = mla_w_uq.reshape(depth, RQ, H, QK_NOPE + QK_ROPE)
    uq_rope = uq[..., QK_NOPE:]
    wq3 = jnp.concatenate([uq[..., :QK_NOPE], uq_rope, _rot_half_cols(uq_rope)], axis=-1)
    wq3 = jnp.transpose(wq3, (0, 2, 1, 3)).astype(BF16)
    wkv3 = jnp.concatenate([mla_w_uk.reshape(depth, RKV, H, QK_NOPE),
                            mla_w_uv.reshape(depth, RKV, H, V_DIM)], axis=-1)
    wkv3 = jnp.transpose(wkv3, (0, 2, 1, 3)).astype(BF16)
    wq3t = jnp.swapaxes(wq3, 2, 3)
    wvt3 = jnp.swapaxes(wkv3[..., QK_NOPE:], 2, 3)

    causal = jnp.tril(jnp.ones((GM_CHUNK, GM_CHUNK), bool))
    ws_p = jnp.where(causal, gm_w_s, 0.0)
    reps = GM_CHUNK // L
    ws_l = jnp.where(causal[:L, :L], gm_w_s[:, :, :L, :L], 0.0)
    ws_s = jnp.einsum("ab,lgij->lgaibj", jnp.eye(reps, dtype=F32), ws_l).reshape(depth, GM_GROUPS, GM_CHUNK, GM_CHUNK)
    ws2 = jnp.stack([ws_p, ws_s], axis=1).astype(BF16)
    gd = GW // GM_GROUPS
    bs_p = jnp.repeat(jnp.transpose(gm_b_s, (0, 2, 1)), gd, axis=2)
    bs_s = jnp.repeat(jnp.tile(jnp.transpose(gm_b_s[:, :, :L], (0, 2, 1)), (1, reps, 1)), gd, axis=2)
    bs2 = jnp.stack([bs_p, bs_s], axis=1)

    gm_w_p_b = gm_w_p.astype(BF16)
    cv_w_pw_b = cv_w_pw.astype(BF16)
    w_o_b = mla_w_o.astype(BF16)
    w_out_b = w_out.astype(BF16)
    ffn_w1_b, ffn_w3_b, ffn_w2_b = ffn_w1.astype(BF16), ffn_w3.astype(BF16), ffn_w2.astype(BF16)
    moe_w1_b, moe_w3_b, moe_w2_b = moe_w1.astype(BF16), moe_w3.astype(BF16), moe_w2.astype(BF16)
    router_pad = jnp.pad(moe_router, ((0, 0), (0, 0), (0, LANES - n_experts)))

    half = QK_ROPE // 2
    inv = ROPE_THETA ** (-jnp.arange(half, dtype=F32) / half)
    pos = jnp.concatenate([jnp.tile(jnp.arange(S), B), jnp.tile(past + jnp.arange(L), NB)]).astype(F32)
    ang = pos[:, None] * inv[None, :]
    cos = jnp.tile(jnp.cos(ang), (1, 2))
    sin = jnp.tile(jnp.sin(ang), (1, 2))
    cos_t, sin_t = cos[:TP].T, sin[:TP].T
    scale_log2e = ATTN_SCALE * math.log2(math.e)

    hist_s = jnp.pad(cache_conv, ((0, 0), (0, 0), (CONV_HALO - (CV_KERNEL - 1), 0), (0, 0)))

    x = jnp.concatenate([x_prompt.reshape(TP, D), x_sample.reshape(TS, D)], axis=0)
    xb = x.astype(BF16)
    row2 = lambda a: a.reshape(1, -1)

    tm_dense = _tile(T, 512)
    eid_dense = jnp.zeros((T // tm_dense,), jnp.int32)
    nu_dense = jnp.full((1,), T // tm_dense, jnp.int32)

    outs = {k: [] for k in ("lat_p", "kr_p", "conv_p", "lat_s", "kr_s", "conv_s", "v_s")}
    for l in range(depth):
        uvg = _mm_act(xb, w_uv[l], zero_uv, "gelu", F32)
        glu = _mm_glu(xb, w_a[l], w_b[l])
        gates = _mm_act(xb, w_g[l], row2(b_gate[l]), "sigmoid", BF16)
        cqn, ckv, kr = _latents(xb, w_q[l], w_kv[l], w_kr2[l], row2(mla_q_norm_g[l]),
                                row2(mla_kv_norm_g[l]), cos, sin)

        us, v_ln = _gmlp(uvg, ws2[l], bs2[l], row2(gm_ln_g[l]), row2(gm_ln_b[l]), TP)

        cv_args = (cv_w_dw[l], row2(cv_b_dw[l]), row2(cv_ln_g[l]), row2(cv_ln_b[l]))
        glu_p = glu[:TP].reshape(B, S, CW)
        glu_s = glu[TP:].reshape(NB, L, CW)
        yb_p = _conv(glu_p, glu_p, *cv_args, zero_first=True)
        yb_s = _conv(glu_s, hist_s[l], *cv_args, zero_first=False)
        ybv = jnp.concatenate([yb_p.reshape(TP, CW), yb_s.reshape(TS, CW)], axis=0)

        qt_p = _qproj_t(cqn, wq3t[l], cos_t, sin_t, B, S, scale_log2e)
        q_s = _qproj(cqn, wq3[l], cos, sin, TP, NB, L)
        k_p, vt_p = _kvproj(ckv, kr, wkv3[l], wvt3[l], B, S)
        o_p = _attn_prompt(qt_p, k_p, vt_p)
        o_s = _attn_sample(q_s, cache_mla_latent, cache_mla_krope, l, ckv, kr, wkv3[l], TP)
        o = jnp.concatenate([o_p.reshape(TP, H * V_DIM), o_s], axis=0)

        merged = _merge(us, ybv, o, gm_w_p_b[l], cv_w_pw_b[l], w_o_b[l], gates)
        x, xb = _outproj_ln(merged, w_out_b[l], x, row2(ln1_g[l]), row2(ln1_b[l]), alpha)

        j = l // 2
        if l % 2 == 0:
            x, xb = _ffn(xb, eid_dense, nu_dense, ffn_w1_b[j][None], ffn_w3_b[j][None], ffn_w2_b[j][None],
                         tm_dense, dense=(x, row2(ln2_g[l]), row2(ln2_b[l]), alpha))
        else:
            route = _router(x, router_pad[j], n_experts)
            slot_tok, block_e, n_used, dest_kt = _moe_plan(route, n_experts, MOE_TM)
            xg = _gather_rows(x, slot_tok)
            y = _ffn(xg, block_e, n_used, moe_w1_b[j], moe_w3_b[j], moe_w2_b[j], MOE_TM)
            yg = _gather_rows(y, dest_kt)
            x, xb = _combine(x, yg, route, row2(ln2_g[l]), row2(ln2_b[l]), alpha)

        outs["lat_p"].append(ckv[:TP].reshape(B, S, RKV))
        outs["kr_p"].append(kr[:TP].reshape(B, S, QK_ROPE))
        outs["conv_p"].append(glu_p[:, S - (CV_KERNEL - 1):])
        outs["lat_s"].append(ckv[TP:].reshape(NB, L, RKV))
        outs["kr_s"].append(kr[TP:].reshape(NB, L, QK_ROPE))
        outs["conv_s"].append(glu_s[:, L - (CV_KERNEL - 1):])
        outs["v_s"].append(v_ln[TP:].reshape(NB, L, GW))

    return (x[:TP].reshape(B, S, D), x[TP:].reshape(NB, L, D),
            jnp.stack(outs["lat_p"]), jnp.stack(outs["kr_p"]), jnp.stack(outs["conv_p"]),
            jnp.stack(outs["lat_s"]), jnp.stack(outs["kr_s"]), jnp.stack(outs["conv_s"]),
            jnp.stack(outs["v_s"]))
```

```python
import functools
import math

import jax
import jax.numpy as jnp
from jax import lax
from jax.experimental import pallas as pl
from jax.experimental.pallas import tpu as pltpu

F32 = jnp.float32
BF16 = jnp.bfloat16

CHUNK = 64
GM_GROUPS = 8
GM_CHUNK = 128
CV_KERNEL = 31
N_HEADS = 16
QK_NOPE = 128
QK_ROPE = 64
V_DIM = 128
QK_PAD = 256
V_ONES = 16
ROPE_THETA = 10000.0
TOP_K = 2
LN_EPS = 1e-5
RMS_EPS = 1e-6
ATTN_SCALE = (QK_NOPE + QK_ROPE) ** -0.5

V7X_VMEM_LIMIT_BYTES = 56 * 1024 * 1024
LANES = 128
SUBLANES = 8
CONV_HALO = 32
MOE_TM = 512
GATHER_UNROLL = 8


def _cp(*sem):
    return pltpu.CompilerParams(dimension_semantics=sem,
                                vmem_limit_bytes=V7X_VMEM_LIMIT_BYTES)


def _tile(n, pref, mult=8):
    if n <= pref:
        return n
    for t in range(pref, 0, -1):
        if n % t == 0 and t % mult == 0:
            return t
    return n


def _ln(x, g, b):
    mu = jnp.mean(x, axis=-1, keepdims=True)
    xc = x - mu
    var = jnp.mean(xc * xc, axis=-1, keepdims=True)
    return xc * lax.rsqrt(var + LN_EPS) * g + b


def _rms(x, g):
    ms = jnp.mean(x * x, axis=-1, keepdims=True)
    return x * lax.rsqrt(ms + RMS_EPS) * g


def _dot(a, b):
    return jnp.dot(a, b, preferred_element_type=F32)


def _dot_nt(a, b):
    return lax.dot_general(a, b, (((1,), (1,)), ((), ())), preferred_element_type=F32)


def _mm_act_kernel(x_ref, w_ref, b_ref, o_ref, *, act):
    acc = _dot(x_ref[...], w_ref[...]) + b_ref[...]
    if act == "gelu":
        acc = jax.nn.gelu(acc)
    elif act == "sigmoid":
        acc = jax.nn.sigmoid(acc)
    o_ref[...] = acc.astype(o_ref.dtype)


def _mm_act(x, w, b, act, out_dtype, tm=1024, tn=1024):
    M, K = x.shape
    N = w.shape[1]
    tm, tn = _tile(M, tm), _tile(N, tn, LANES)
    return pl.pallas_call(
        functools.partial(_mm_act_kernel, act=act),
        grid=(M // tm, N // tn),
        in_specs=[pl.BlockSpec((tm, K), lambda i, j: (i, 0)),
                  pl.BlockSpec((K, tn), lambda i, j: (0, j)),
                  pl.BlockSpec((1, tn), lambda i, j: (0, j))],
        out_specs=pl.BlockSpec((tm, tn), lambda i, j: (i, j)),
        out_shape=jax.ShapeDtypeStruct((M, N), out_dtype),
        compiler_params=_cp("parallel", "arbitrary"),
        name="mm_" + act,
    )(x, w, b)


def _mm_glu_kernel(x_ref, wa_ref, wb_ref, o_ref):
    x = x_ref[...]
    o_ref[...] = _dot(x, wa_ref[...]) * jax.nn.sigmoid(_dot(x, wb_ref[...]))


def _mm_glu(x, wa, wb, tm=1024, tn=512):
    M, K = x.shape
    N = wa.shape[1]
    tm, tn = _tile(M, tm), _tile(N, tn, LANES)
    return pl.pallas_call(
        _mm_glu_kernel,
        grid=(M // tm, N // tn),
        in_specs=[pl.BlockSpec((tm, K), lambda i, j: (i, 0)),
                  pl.BlockSpec((K, tn), lambda i, j: (0, j)),
                  pl.BlockSpec((K, tn), lambda i, j: (0, j))],
        out_specs=pl.BlockSpec((tm, tn), lambda i, j: (i, j)),
        out_shape=jax.ShapeDtypeStruct((M, N), F32),
        compiler_params=_cp("parallel", "arbitrary"),
        name="mm_glu",
    )(x, wa, wb)


def _latent_kernel(x_ref, wq_ref, wkv_ref, wkr_ref, gq_ref, gkv_ref, cos_ref, sin_ref,
                   cq_ref, ckv_ref, kr_ref):
    x = x_ref[...]
    cq_ref[...] = _rms(_dot(x, wq_ref[...]), gq_ref[...]).astype(cq_ref.dtype)
    ckv_ref[...] = _rms(_dot(x, wkv_ref[...]), gkv_ref[...])
    r = _dot(x, wkr_ref[...])
    kr_ref[...] = r[:, :QK_ROPE] * cos_ref[...] + r[:, QK_ROPE:] * sin_ref[...]


def _latents(x, wq, wkv, wkr2, gq, gkv, cos, sin, tm=512):
    M, K = x.shape
    Rq, Rkv = wq.shape[1], wkv.shape[1]
    tm = _tile(M, tm)
    row = lambda i: (i, 0)
    full = lambda i: (0, 0)
    return pl.pallas_call(
        _latent_kernel,
        grid=(M // tm,),
        in_specs=[pl.BlockSpec((tm, K), row),
                  pl.BlockSpec((K, Rq), full), pl.BlockSpec((K, Rkv), full),
                  pl.BlockSpec((K, 2 * QK_ROPE), full),
                  pl.BlockSpec((1, Rq), full), pl.BlockSpec((1, Rkv), full),
                  pl.BlockSpec((tm, QK_ROPE), row), pl.BlockSpec((tm, QK_ROPE), row)],
        out_specs=[pl.BlockSpec((tm, Rq), row), pl.BlockSpec((tm, Rkv), row),
                   pl.BlockSpec((tm, QK_ROPE), row)],
        out_shape=[jax.ShapeDtypeStruct((M, Rq), BF16), jax.ShapeDtypeStruct((M, Rkv), F32),
                   jax.ShapeDtypeStruct((M, QK_ROPE), F32)],
        compiler_params=_cp("parallel"),
        name="latents",
    )(x, wq, wkv, wkr2, gq, gkv, cos, sin)


def _gmlp_kernel(uv_ref, ws_ref, bs_ref, g_ref, b_ref, us_ref, v_ref, *, n_chunks, groups):
    W = v_ref.shape[1]
    gd = W // groups
    vn = _ln(uv_ref[:, W:], g_ref[...], b_ref[...])
    v_ref[...] = vn
    vb = vn.astype(BF16)
    for c in range(n_chunks):
        r0 = c * GM_CHUNK
        for g in range(groups):
            c0 = g * gd
            s = _dot(ws_ref[0, g], vb[r0:r0 + GM_CHUNK, c0:c0 + gd]) + bs_ref[0, :, c0:c0 + gd]
            u = uv_ref[r0:r0 + GM_CHUNK, c0:c0 + gd]
            us_ref[r0:r0 + GM_CHUNK, c0:c0 + gd] = (u * s).astype(us_ref.dtype)


def _gmlp(uvg, ws2, bs2, g, b, n_prompt_rows, tm=512):
    M, W2 = uvg.shape
    W = W2 // 2
    tm = _tile(math.gcd(n_prompt_rows, M - n_prompt_rows), tm, GM_CHUNK)
    npt = n_prompt_rows // tm
    sel = lambda i: (jnp.minimum(i // npt, 1), 0, 0, 0)
    sel3 = lambda i: (jnp.minimum(i // npt, 1), 0, 0)
    row = lambda i: (i, 0)
    full = lambda i: (0, 0)
    return pl.pallas_call(
        functools.partial(_gmlp_kernel, n_chunks=tm // GM_CHUNK, groups=GM_GROUPS),
        grid=(M // tm,),
        in_specs=[pl.BlockSpec((tm, W2), row),
                  pl.BlockSpec((1, GM_GROUPS, GM_CHUNK, GM_CHUNK), sel),
                  pl.BlockSpec((1, GM_CHUNK, W), sel3),
                  pl.BlockSpec((1, W), full), pl.BlockSpec((1, W), full)],
        out_specs=[pl.BlockSpec((tm, W), row), pl.BlockSpec((tm, W), row)],
        out_shape=[jax.ShapeDtypeStruct((M, W), BF16), jax.ShapeDtypeStruct((M, W), F32)],
        compiler_params=_cp("parallel"),
        name="gmlp",
    )(uvg, ws2, bs2, g, b)


def _conv_kernel(cur_ref, halo_ref, w_ref, bdw_ref, g_ref, b_ref, o_ref, full_ref, sh_ref, acc_ref,
                 *, tm, zero_first):
    C = cur_ref.shape[1]
    halo = halo_ref[...].reshape(CONV_HALO, C)
    if zero_first:
        halo = jnp.where(pl.program_id(1) == 0, 0.0, halo)
    full_ref[0:CONV_HALO, :] = halo
    full_ref[CONV_HALO:CONV_HALO + tm, :] = cur_ref[...]
    off = CONV_HALO - (CV_KERNEL - 1)
    ns = sh_ref.shape[1]

    def lane_block(c, carry):
        c0 = pl.multiple_of(c * LANES, LANES)
        sh_ref[0] = full_ref[:, pl.ds(c0, LANES)]
        for r in range(1, SUBLANES):
            sh_ref[r, :ns - SUBLANES, :] = full_ref[pl.ds(r, ns - SUBLANES), pl.ds(c0, LANES)]
        wk = w_ref[:, pl.ds(c0, LANES)]
        wrows = [jnp.broadcast_to(wk[k:k + 1, :], (SUBLANES, LANES)) for k in range(CV_KERNEL)]
        for rb in range(tm // SUBLANES):
            acc = jnp.zeros((SUBLANES, LANES), F32)
            for k in range(CV_KERNEL):
                a, r = divmod(off + k, SUBLANES)
                acc = acc + sh_ref[r, pl.ds((rb + a) * SUBLANES, SUBLANES), :] * wrows[k]
            acc_ref[pl.ds(rb * SUBLANES, SUBLANES), pl.ds(c0, LANES)] = acc
        return carry

    lax.fori_loop(0, C // LANES, lane_block, 0)
    y = _ln(acc_ref[...] + bdw_ref[...], g_ref[...], b_ref[...])
    o_ref[...] = (y * jax.nn.sigmoid(y)).astype(o_ref.dtype)


def _conv(glu, halo_src, row0, n_seq, L, w_dw, b_dw, g, b, zero_first, tm=256):
    C = glu.shape[1]
    tm = _tile(L, tm, CONV_HALO)
    lt = L // tm
    hb = tm // CONV_HALO
    assert row0 % tm == 0
    r0 = row0 // tm
    if zero_first:
        halo_spec = pl.BlockSpec((CONV_HALO, C), lambda s, i: (jnp.maximum((r0 + s * lt + i) * hb - 1, 0), 0))
    else:
        halo_spec = pl.BlockSpec((1, CONV_HALO, C), lambda s, i: (s, 0, 0))
    full = lambda s, i: (0, 0)
    return pl.pallas_call(
        functools.partial(_conv_kernel, tm=tm, zero_first=zero_first),
        grid=(n_seq, lt),
        in_specs=[pl.BlockSpec((tm, C), lambda s, i: (r0 + s * lt + i, 0)),
                  halo_spec,
                  pl.BlockSpec((CV_KERNEL, C), full),
                  pl.BlockSpec((1, C), full), pl.BlockSpec((1, C), full), pl.BlockSpec((1, C), full)],
        out_specs=pl.BlockSpec((tm, C), lambda s, i: (s * lt + i, 0)),
        out_shape=jax.ShapeDtypeStruct((n_seq * L, C), BF16),
        scratch_shapes=[pltpu.VMEM((CONV_HALO + tm, C), F32),
                        pltpu.VMEM((SUBLANES, CONV_HALO + tm, LANES), F32),
                        pltpu.VMEM((tm, C), F32)],
        compiler_params=_cp("parallel", "arbitrary"),
        name="conv_prompt" if zero_first else "conv_sample",
    )(glu, halo_src, w_dw, b_dw, g, b)


def _qproj_kernel(cq_ref, w_ref, cos_ref, sin_ref, q_ref, *, hg, nb, L):
    cq = cq_ref[...]
    cos, sin = cos_ref[...], sin_ref[...]
    for h in range(hg):
        r = _dot(cq, w_ref[h])
        nope = r[:, :QK_NOPE] * ATTN_SCALE
        rp = (r[:, QK_NOPE:QK_NOPE + QK_ROPE] * cos + r[:, QK_NOPE + QK_ROPE:] * sin) * ATTN_SCALE
        if nb == 1:
            q_ref[0, h, :, :QK_NOPE] = nope.astype(q_ref.dtype)
            q_ref[0, h, :, QK_NOPE:] = rp.astype(q_ref.dtype)
        else:
            q_ref[:, h, :, :QK_NOPE] = nope.reshape(nb, L, QK_NOPE).astype(q_ref.dtype)
            q_ref[:, h, :, QK_NOPE:] = rp.reshape(nb, L, QK_ROPE).astype(q_ref.dtype)


def _qproj(cqn, wq3, cos, sin, row0, n_seq, L, hg=4, tm=512):
    R = cqn.shape[1]
    H = wq3.shape[0]
    hg = _tile(H, hg, 1)
    dq = QK_NOPE + QK_ROPE
    if L >= tm:
        tm = _tile(L, tm)
        nb, lt = 1, L // tm
        out_spec = pl.BlockSpec((1, hg, tm, dq), lambda i, h: (i // lt, h, i % lt, 0))
    else:
        nb = _tile(n_seq, max(tm // L, 1), 1)
        tm = nb * L
        out_spec = pl.BlockSpec((nb, hg, L, dq), lambda i, h: (i, h, 0, 0))
    assert row0 % tm == 0
    r0 = row0 // tm
    row = lambda i, h: (r0 + i, 0)
    return pl.pallas_call(
        functools.partial(_qproj_kernel, hg=hg, nb=nb, L=L),
        grid=(n_seq * L // tm, H // hg),
        in_specs=[pl.BlockSpec((tm, R), row),
                  pl.BlockSpec((hg, R, wq3.shape[2]), lambda i, h: (h, 0, 0)),
                  pl.BlockSpec((tm, QK_ROPE), row), pl.BlockSpec((tm, QK_ROPE), row)],
        out_specs=out_spec,
        out_shape=jax.ShapeDtypeStruct((n_seq, H, L, dq), BF16),
        compiler_params=_cp("parallel", "arbitrary"),
        name="qproj",
    )(cqn, wq3, cos, sin)


def _qproj_t_kernel(cq_ref, w_ref, cos_ref, sin_ref, q_ref, *, hg, scale):
    cq = cq_ref[...]
    cos, sin = cos_ref[...], sin_ref[...]
    for h in range(hg):
        r = _dot_nt(w_ref[h], cq)
        rp = r[QK_NOPE:QK_NOPE + QK_ROPE] * cos + r[QK_NOPE + QK_ROPE:] * sin
        q_ref[0, h, :QK_NOPE, :] = (r[:QK_NOPE] * scale).astype(q_ref.dtype)
        q_ref[0, h, QK_NOPE:QK_NOPE + QK_ROPE, :] = (rp * scale).astype(q_ref.dtype)
        q_ref[0, h, QK_NOPE + QK_ROPE:, :] = jnp.zeros((QK_PAD - QK_NOPE - QK_ROPE, cq.shape[0]), q_ref.dtype)


def _qproj_t(cqn, wq3t, cos_t, sin_t, n_seq, L, scale, hg=4, tm=512):
    R = cqn.shape[1]
    H = wq3t.shape[0]
    hg = _tile(H, hg, 1)
    tm = _tile(L, tm, LANES)
    lt = L // tm
    dq = QK_PAD
    return pl.pallas_call(
        functools.partial(_qproj_t_kernel, hg=hg, scale=scale),
        grid=(n_seq * lt, H // hg),
        in_specs=[pl.BlockSpec((tm, R), lambda i, h: (i, 0)),
                  pl.BlockSpec((hg, wq3t.shape[1], R), lambda i, h: (h, 0, 0)),
                  pl.BlockSpec((QK_ROPE, tm), lambda i, h: (0, i)),
                  pl.BlockSpec((QK_ROPE, tm), lambda i, h: (0, i))],
        out_specs=pl.BlockSpec((1, hg, dq, tm), lambda i, h: (i // lt, h, 0, i % lt)),
        out_shape=jax.ShapeDtypeStruct((n_seq, H, dq, L), BF16),
        compiler_params=_cp("parallel", "arbitrary"),
        name="qproj_t",
    )(cqn, wq3t, cos_t, sin_t)


def _kvproj_kernel(ckv_ref, kr_ref, wk_ref, wvt_ref, k_ref, vt_ref, *, hg):
    ckv = ckv_ref[...].astype(BF16)
    kr = kr_ref[...].astype(BF16)
    for h in range(hg):
        k_ref[0, h, :, :QK_NOPE] = _dot(ckv, wk_ref[h, :, :QK_NOPE]).astype(k_ref.dtype)
        k_ref[0, h, :, QK_NOPE:QK_NOPE + QK_ROPE] = kr
        k_ref[0, h, :, QK_NOPE + QK_ROPE:] = jnp.zeros((kr.shape[0], QK_PAD - QK_NOPE - QK_ROPE), k_ref.dtype)
        vt_ref[0, h, :V_DIM, :] = _dot_nt(wvt_ref[h], ckv).astype(vt_ref.dtype)
        vt_ref[0, h, V_DIM:, :] = jnp.ones((V_ONES, ckv.shape[0]), vt_ref.dtype)


def _kvproj(ckv, kr, wkv3, wvt3, n_seq, L, hg=4, tm=512):
    R = ckv.shape[1]
    H = wkv3.shape[0]
    hg = _tile(H, hg, 1)
    tm = _tile(L, tm, LANES)
    lt = L // tm
    dq = QK_PAD
    row = lambda i, h: (i, 0)
    wmap = lambda i, h: (h, 0, 0)
    return pl.pallas_call(
        functools.partial(_kvproj_kernel, hg=hg),
        grid=(n_seq * lt, H // hg),
        in_specs=[pl.BlockSpec((tm, R), row), pl.BlockSpec((tm, QK_ROPE), row),
                  pl.BlockSpec((hg, R, wkv3.shape[2]), wmap), pl.BlockSpec((hg, V_DIM, R), wmap)],
        out_specs=[pl.BlockSpec((1, hg, tm, dq), lambda i, h: (i // lt, h, i % lt, 0)),
                   pl.BlockSpec((1, hg, V_DIM + V_ONES, tm), lambda i, h: (i // lt, h, 0, i % lt))],
        out_shape=[jax.ShapeDtypeStruct((n_seq, H, L, dq), BF16),
                   jax.ShapeDtypeStruct((n_seq, H, V_DIM + V_ONES, L), BF16)],
        compiler_params=_cp("parallel", "arbitrary"),
        name="kvproj",
    )(ckv, kr, wkv3, wvt3)


def _attn_prompt_kernel(qt_ref, k_ref, vt_ref, o_ref, s_ref, p_ref, acc_ref, *, tq, tk, hg):
    qi = pl.program_id(2)
    d0 = pl.multiple_of(qi * tq, tq)

    def scores(g, j0, slot):
        s = _dot(k_ref[0, g, pl.ds(j0, tk), :], qt_ref[0, g])
        s_ref[slot, g] = s
        return jnp.max(s, axis=0, keepdims=True)

    def softmax(g, slot, mx, m, mask=None):
        s = s_ref[slot, g]
        if mask is not None:
            s = jnp.where(mask, s, -jnp.inf)
            mx = jnp.max(s, axis=0, keepdims=True)
        m_new = jnp.maximum(m, mx)
        p_ref[slot, g] = jnp.exp2(s - m_new).astype(BF16)
        return jnp.exp2(m - m_new), m_new

    def accumulate(g, j0, slot, alpha):
        acc_ref[g] = alpha * acc_ref[g] + _dot(vt_ref[0, g, :, pl.ds(j0, tk)], p_ref[slot, g])

    def step(j0, slot, state, mask=None, last=False):
        jp = pl.multiple_of(jnp.maximum(j0 - tk, 0), tk)
        new = []
        for g in range(hg):
            mx, a_prev, m = state[g]
            mx_next = mx if last else scores(g, pl.multiple_of(j0 + tk, tk), 1 - slot)
            accumulate(g, jp, 1 - slot, a_prev)
            alpha, m = softmax(g, slot, mx, m, mask)
            new.append((mx_next, alpha, m))
        return tuple(new)

    def body(i, state):
        j0 = pl.multiple_of(2 * i * tk, tk)
        state = step(j0, 0, state)
        return step(pl.multiple_of(j0 + tk, tk), 1, state)

    p_ref[1] = jnp.zeros_like(p_ref[1])
    acc_ref[...] = jnp.zeros_like(acc_ref)
    init = tuple((scores(g, 0, 0), jnp.ones((1, tq), F32), jnp.full((1, tq), -jnp.inf, F32))
                 for g in range(hg))
    state = lax.fori_loop(0, qi * (tq // (2 * tk)), body, init)

    key_chunk = lax.broadcasted_iota(jnp.int32, (tk, tq), 0) // CHUNK
    qry_chunk = lax.broadcasted_iota(jnp.int32, (tk, tq), 1) // CHUNK
    state = step(d0, 0, state, mask=key_chunk <= qry_chunk)
    state = step(pl.multiple_of(d0 + tk, tk), 1, state, mask=key_chunk + tk // CHUNK <= qry_chunk, last=True)
    for g in range(hg):
        accumulate(g, pl.multiple_of(d0 + tk, tk), 1, state[g][1])
        o = acc_ref[g, :V_DIM, :] / acc_ref[g, V_DIM:V_DIM + 1, :]
        o_ref[:, g * V_DIM:(g + 1) * V_DIM] = o.T.astype(o_ref.dtype)


def _attn_prompt(qt, k, vt, tq=512, hg=2):
    B, H, dq, S = qt.shape
    tq = _tile(S, tq, 2 * LANES)
    tk = tq // 2
    hg = _tile(H, hg, 1)
    return pl.pallas_call(
        functools.partial(_attn_prompt_kernel, tq=tq, tk=tk, hg=hg),
        grid=(B, H // hg, S // tq),
        in_specs=[pl.BlockSpec((1, hg, dq, tq), lambda b, h, i: (b, h, 0, i)),
                  pl.BlockSpec((1, hg, S, dq), lambda b, h, i: (b, h, 0, 0)),
                  pl.BlockSpec((1, hg, vt.shape[2], S), lambda b, h, i: (b, h, 0, 0))],
        out_specs=pl.BlockSpec((tq, hg * V_DIM), lambda b, h, i: (b * (S // tq) + i, h)),
        out_shape=jax.ShapeDtypeStruct((B * S, H * V_DIM), BF16),
        scratch_shapes=[pltpu.VMEM((2, hg, tk, tq), F32), pltpu.VMEM((2, hg, tk, tq), BF16),
                        pltpu.VMEM((hg, vt.shape[2], tq), F32)],
        compiler_params=_cp("parallel", "parallel", "arbitrary"),
        name="attn_prompt",
    )(qt, k, vt)


def _attn_sample_kernel(q_ref, plat_ref, pkr_ref, nlat_ref, nkr_ref, w_ref, o_ref, ql_ref, qr_ref,
                        *, H, L, kc):
    P = plat_ref.shape[2]
    for h in range(H):
        qh = q_ref[0, h]
        ql_ref[h * L:(h + 1) * L, :] = _dot_nt(qh[:, :QK_NOPE], w_ref[h, :, :QK_NOPE]).astype(BF16)
        qr_ref[h * L:(h + 1) * L, :] = qh[:, QK_NOPE:]
    ql = ql_ref[...]
    qr = qr_ref[...]

    def step(lat, kr, carry):
        m, l, acc = carry
        s = _dot_nt(ql, lat) + _dot_nt(qr, kr)
        m_new = jnp.maximum(m, jnp.max(s, axis=-1, keepdims=True))
        alpha = jnp.exp(m - m_new)
        p = jnp.exp(s - m_new)
        l = alpha * l + jnp.sum(p, axis=-1, keepdims=True)
        acc = alpha * acc + _dot(p.astype(BF16), lat)
        return m_new, l, acc

    R = ql.shape[1]
    carry = (jnp.full((H * L, 1), -jnp.inf, F32), jnp.zeros((H * L, 1), F32), jnp.zeros((H * L, R), F32))
    for c in range(P // kc):
        carry = step(plat_ref[0, 0, c * kc:(c + 1) * kc, :].astype(BF16),
                     pkr_ref[0, 0, c * kc:(c + 1) * kc, :].astype(BF16), carry)
    m, l, acc = step(nlat_ref[...].astype(BF16), nkr_ref[...].astype(BF16), carry)
    ol = (acc / l).astype(BF16)
    for h in range(H):
        o_ref[:, h * V_DIM:(h + 1) * V_DIM] = _dot(ol[h * L:(h + 1) * L, :], w_ref[h, :, QK_NOPE:]).astype(o_ref.dtype)


def _attn_sample(q, cache_lat, cache_kr, layer, ckv, kr, wkv3, row0):
    n_req, H, L, dq = q.shape
    P, R = cache_lat.shape[2], cache_lat.shape[3]
    kc = _tile(P, 1024)
    assert row0 % L == 0
    r0 = row0 // L
    return pl.pallas_call(
        functools.partial(_attn_sample_kernel, H=H, L=L, kc=kc),
        grid=(n_req,),
        in_specs=[pl.BlockSpec((1, H, L, dq), lambda b: (b, 0, 0, 0)),
                  pl.BlockSpec((1, 1, P, R), lambda b: (layer, b, 0, 0)),
                  pl.BlockSpec((1, 1, P, QK_ROPE), lambda b: (layer, b, 0, 0)),
                  pl.BlockSpec((L, R), lambda b: (r0 + b, 0)),
                  pl.BlockSpec((L, QK_ROPE), lambda b: (r0 + b, 0)),
                  pl.BlockSpec(wkv3.shape, lambda b: (0, 0, 0))],
        out_specs=pl.BlockSpec((L, H * V_DIM), lambda b: (b, 0)),
        out_shape=jax.ShapeDtypeStruct((n_req * L, H * V_DIM), BF16),
        scratch_shapes=[pltpu.VMEM((H * L, R), BF16), pltpu.VMEM((H * L, QK_ROPE), BF16)],
        compiler_params=_cp("parallel"),
        name="attn_sample",
    )(q, cache_lat, cache_kr, ckv, kr, wkv3)


def _merge_kernel(x_ref, us_ref, ybp_ref, ybs_ref, op_ref, os_ref, wp_ref, wpw_ref, wo_ref,
                  wga_ref, wgb_ref, wgc_ref, ba_ref, bb_ref, bc_ref, out_ref, *, npt):
    prompt = pl.program_id(0) < npt
    x = x_ref[...]
    yb_in = jnp.where(prompt, ybp_ref[...], ybs_ref[...])
    o_in = jnp.where(prompt, op_ref[...], os_ref[...])
    out = jax.nn.sigmoid(_dot(x, wga_ref[...]) + ba_ref[...]) * _dot(us_ref[...], wp_ref[...])
    out += jax.nn.sigmoid(_dot(x, wgb_ref[...]) + bb_ref[...]) * _dot(yb_in, wpw_ref[...])
    out += jax.nn.sigmoid(_dot(x, wgc_ref[...]) + bc_ref[...]) * _dot(o_in, wo_ref[...])
    out_ref[...] = out.astype(out_ref.dtype)


def _merge(xb, us, yb_p, yb_s, o_p, o_s, wp, wpw, wo, wg, bg, tm=512, tn=512):
    M = us.shape[0]
    D = wp.shape[1]
    TP, TS = yb_p.shape[0], yb_s.shape[0]
    tm, tn = _tile(math.gcd(TP, TS), tm), _tile(D, tn, LANES)
    npt = TP // tm
    nj = D // tn
    row = lambda i, j: (i, 0)
    prow = lambda i, j: (jnp.minimum(i, npt - 1), 0)
    srow = lambda i, j: (jnp.maximum(i - npt, 0), 0)
    col = lambda i, j: (0, j)
    gcol = lambda g: (lambda i, j: (0, g * nj + j))
    return pl.pallas_call(
        functools.partial(_merge_kernel, npt=npt),
        grid=(M // tm, nj),
        in_specs=[pl.BlockSpec((tm, xb.shape[1]), row), pl.BlockSpec((tm, us.shape[1]), row),
                  pl.BlockSpec((tm, yb_p.shape[1]), prow), pl.BlockSpec((tm, yb_s.shape[1]), srow),
                  pl.BlockSpec((tm, o_p.shape[1]), prow), pl.BlockSpec((tm, o_s.shape[1]), srow),
                  pl.BlockSpec((wp.shape[0], tn), col), pl.BlockSpec((wpw.shape[0], tn), col),
                  pl.BlockSpec((wo.shape[0], tn), col),
                  pl.BlockSpec((wg.shape[0], tn), gcol(0)), pl.BlockSpec((wg.shape[0], tn), gcol(1)),
                  pl.BlockSpec((wg.shape[0], tn), gcol(2)),
                  pl.BlockSpec((1, tn), gcol(0)), pl.BlockSpec((1, tn), gcol(1)), pl.BlockSpec((1, tn), gcol(2))],
        out_specs=pl.BlockSpec((tm, tn), lambda i, j: (i, j)),
        out_shape=jax.ShapeDtypeStruct((M, D), BF16),
        compiler_params=_cp("parallel", "arbitrary"),
        name="merge",
    )(xb, us, yb_p, yb_s, o_p, o_s, wp, wpw, wo, wg, wg, wg, bg, bg, bg)


def _outproj_kernel(m_ref, w_ref, x_ref, g_ref, b_ref, o_ref, ob_ref, *, alpha):
    y = _ln(alpha * x_ref[...] + _dot(m_ref[...], w_ref[...]), g_ref[...], b_ref[...])
    o_ref[...] = y
    ob_ref[...] = y.astype(ob_ref.dtype)


def _outproj_ln(merged, w, x, g, b, alpha, tm=512):
    M, D = x.shape
    tm = _tile(M, tm)
    row = lambda i: (i, 0)
    full = lambda i: (0, 0)
    return pl.pallas_call(
        functools.partial(_outproj_kernel, alpha=alpha),
        grid=(M // tm,),
        in_specs=[pl.BlockSpec((tm, merged.shape[1]), row), pl.BlockSpec(w.shape, full),
                  pl.BlockSpec((tm, D), row), pl.BlockSpec((1, D), full), pl.BlockSpec((1, D), full)],
        out_specs=[pl.BlockSpec((tm, D), row), pl.BlockSpec((tm, D), row)],
        out_shape=[jax.ShapeDtypeStruct((M, D), F32), jax.ShapeDtypeStruct((M, D), BF16)],
        compiler_params=_cp("parallel"),
        name="outproj_ln",
    )(merged, w, x, g, b)


def _ffn_dense_kernel(eid_ref, nu_ref, x_ref, w1_ref, w3_ref, w2_ref, r_ref, g_ref, b_ref,
                      o_ref, ob_ref, acc_ref, *, alpha):
    f = pl.program_id(1)

    @pl.when(f == 0)
    def _():
        acc_ref[...] = jnp.zeros_like(acc_ref)

    x = x_ref[...]
    h = jax.nn.silu(_dot(x, w1_ref[0])) * _dot(x, w3_ref[0])
    acc_ref[...] += _dot(h.astype(BF16), w2_ref[0])

    @pl.when(f == pl.num_programs(1) - 1)
    def _():
        y = _ln(alpha * r_ref[...] + acc_ref[...], g_ref[...], b_ref[...])
        o_ref[...] = y
        ob_ref[...] = y.astype(ob_ref.dtype)


def _ffn_moe_kernel(eid_ref, nu_ref, x_ref, w1_ref, w3_ref, w2_ref, o_ref, xb_ref):
    i = pl.program_id(0)
    f = pl.program_id(1)

    @pl.when(f == 0)
    def _():
        o_ref[...] = jnp.zeros_like(o_ref)
        xb_ref[...] = x_ref[...].astype(BF16)

    @pl.when(i < nu_ref[0])
    def _():
        x = xb_ref[...]
        h = jax.nn.silu(_dot(x, w1_ref[0])) * _dot(x, w3_ref[0])
        o_ref[...] += _dot(h.astype(BF16), w2_ref[0])


def _ffn(x, eid, n_used, w1, w3, w2, tm, tf=512, dense=None):
    M, D = x.shape
    Fdim = w1.shape[2]
    tf = _tile(Fdim, tf, LANES)
    last = lambda i, nu: jnp.minimum(i, nu[0] - 1)
    xmap = lambda i, f, e, nu: (last(i, nu), 0)
    w13 = lambda i, f, e, nu: (e[last(i, nu)], 0, jnp.where(i < nu[0], f, 0))
    w2m = lambda i, f, e, nu: (e[last(i, nu)], jnp.where(i < nu[0], f, 0), 0)
    row = lambda i, f, e, nu: (i, 0)
    full = lambda i, f, e, nu: (0, 0)
    in_specs = [pl.BlockSpec((tm, D), xmap),
                pl.BlockSpec((1, D, tf), w13), pl.BlockSpec((1, D, tf), w13),
                pl.BlockSpec((1, tf, D), w2m)]
    if dense is not None:
        resid, g, b, alpha = dense
        kern = functools.partial(_ffn_dense_kernel, alpha=alpha)
        in_specs += [pl.BlockSpec((tm, D), row), pl.BlockSpec((1, D), full), pl.BlockSpec((1, D), full)]
        out_specs = [pl.BlockSpec((tm, D), row), pl.BlockSpec((tm, D), row)]
        out_shape = [jax.ShapeDtypeStruct((M, D), F32), jax.ShapeDtypeStruct((M, D), BF16)]
        scratch = [pltpu.VMEM((tm, D), F32)]
        args = (x, w1, w3, w2, resid, g, b)
    else:
        kern = _ffn_moe_kernel
        out_specs = pl.BlockSpec((tm, D), row)
        out_shape = jax.ShapeDtypeStruct((M, D), F32)
        scratch = [pltpu.VMEM((tm, D), BF16)]
        args = (x, w1, w3, w2)
    return pl.pallas_call(
        kern,
        grid_spec=pltpu.PrefetchScalarGridSpec(
            num_scalar_prefetch=2, grid=(M // tm, Fdim // tf),
            in_specs=in_specs, out_specs=out_specs, scratch_shapes=scratch),
        out_shape=out_shape,
        compiler_params=_cp("arbitrary", "arbitrary"),
        name="ffn_dense" if dense is not None else "ffn_moe",
    )(eid, n_used, *args)


def _router_kernel(x_ref, r_ref, o_ref, *, n_experts):
    logits = jnp.dot(x_ref[...], r_ref[...], preferred_element_type=F32,
                     precision=lax.Precision.HIGHEST)
    col = lax.broadcasted_iota(jnp.int32, logits.shape, 1)
    lg = jnp.where(col < n_experts, logits, -jnp.inf)
    m1 = jnp.max(lg, axis=-1, keepdims=True)
    i1 = jnp.min(jnp.where(lg == m1, col, LANES), axis=-1, keepdims=True)
    lg2 = jnp.where(col == i1, -jnp.inf, lg)
    m2 = jnp.max(lg2, axis=-1, keepdims=True)
    i2 = jnp.min(jnp.where(lg2 == m2, col, LANES), axis=-1, keepdims=True)
    e = jnp.exp(m2 - m1)
    g1 = 1.0 / (1.0 + e)
    g2 = e * g1
    out = jnp.where(col == 0, i1.astype(F32),
                    jnp.where(col == 1, i2.astype(F32),
                              jnp.where(col == 2, g1, jnp.where(col == 3, g2, 0.0))))
    o_ref[...] = out


def _router(x, router_pad, n_experts, tm=512):
    M, D = x.shape
    tm = _tile(M, tm)
    return pl.pallas_call(
        functools.partial(_router_kernel, n_experts=n_experts),
        grid=(M // tm,),
        in_specs=[pl.BlockSpec((tm, D), lambda i: (i, 0)), pl.BlockSpec((D, LANES), lambda i: (0, 0))],
        out_specs=pl.BlockSpec((tm, LANES), lambda i: (i, 0)),
        out_shape=jax.ShapeDtypeStruct((M, LANES), F32),
        compiler_params=_cp("parallel"),
        name="router",
    )(x, router_pad)


def _gather_kernel(idx_ref, src_ref, o_ref, sem, *, tm):
    def row_copy(r, t):
        return pltpu.make_async_copy(src_ref.at[pl.ds(t, 1), :], o_ref.at[pl.ds(r, 1), :], sem)

    def start(c, carry):
        for u in range(GATHER_UNROLL):
            r = c * GATHER_UNROLL + u
            row_copy(r, idx_ref[0, 0, r]).start(priority=u % 2)
        return carry

    def wait(c, carry):
        for u in range(GATHER_UNROLL):
            row_copy(c * GATHER_UNROLL + u, 0).wait()
        return carry

    lax.fori_loop(0, tm // GATHER_UNROLL, start, 0)
    lax.fori_loop(0, tm // GATHER_UNROLL, wait, 0)


def _gather_rows(src, idx, tm=256):
    M = idx.shape[0]
    D = src.shape[1]
    tm = _tile(M, tm)
    idx3 = idx.reshape(M // tm, 1, tm)
    return pl.pallas_call(
        functools.partial(_gather_kernel, tm=tm),
        grid=(M // tm,),
        in_specs=[pl.BlockSpec((1, 1, tm), lambda i: (i, 0, 0), memory_space=pltpu.SMEM),
                  pl.BlockSpec(memory_space=pl.ANY)],
        out_specs=pl.BlockSpec((tm, D), lambda i: (i, 0)),
        out_shape=jax.ShapeDtypeStruct((M, D), src.dtype),
        scratch_shapes=[pltpu.SemaphoreType.DMA(())],
        compiler_params=_cp("arbitrary"),
        name="gather_rows",
    )(idx3, src)


def _combine_kernel(x_ref, y1_ref, y2_ref, r_ref, g_ref, b_ref, o_ref, ob_ref, *, alpha):
    r = r_ref[...]
    y = alpha * x_ref[...] + r[:, 2:3] * y1_ref[...] + r[:, 3:4] * y2_ref[...]
    y = _ln(y, g_ref[...], b_ref[...])
    o_ref[...] = y
    ob_ref[...] = y.astype(ob_ref.dtype)


def _combine(x, yg, route, g, b, alpha, tm=512):
    M, D = x.shape
    tm = _tile(M, tm)
    nt = M // tm
    row = lambda i: (i, 0)
    full = lambda i: (0, 0)
    return pl.pallas_call(
        functools.partial(_combine_kernel, alpha=alpha),
        grid=(nt,),
        in_specs=[pl.BlockSpec((tm, D), row), pl.BlockSpec((tm, D), row),
                  pl.BlockSpec((tm, D), lambda i: (nt + i, 0)),
                  pl.BlockSpec((tm, LANES), row), pl.BlockSpec((1, D), full), pl.BlockSpec((1, D), full)],
        out_specs=[pl.BlockSpec((tm, D), row), pl.BlockSpec((tm, D), row)],
        out_shape=[jax.ShapeDtypeStruct((M, D), F32), jax.ShapeDtypeStruct((M, D), BF16)],
        compiler_params=_cp("parallel"),
        name="combine",
    )(x, yg, yg, route, g, b)


def _moe_plan(route, n_experts, tm):
    T = route.shape[0]
    n_assign = T * TOP_K
    e_flat = route[:, :TOP_K].astype(jnp.int32).reshape(-1)
    onehot = (e_flat[:, None] == jnp.arange(n_experts, dtype=jnp.int32)[None, :]).astype(jnp.int32)
    csum = jnp.cumsum(onehot, axis=0)
    rank = jnp.take_along_axis(csum, e_flat[:, None], axis=1)[:, 0] - 1
    counts = csum[-1]
    padded = ((counts + tm - 1) // tm) * tm
    pad_end = jnp.cumsum(padded)
    pad_start = pad_end - padded
    dest = (pad_start[e_flat] + rank).astype(jnp.int32)
    n_blocks = -(-n_assign // tm) + n_experts
    n_slots = n_blocks * tm
    slot_tok = jnp.zeros((n_slots,), jnp.int32).at[dest].set(jnp.arange(n_assign, dtype=jnp.int32) // TOP_K)
    block_e = jnp.minimum(jnp.searchsorted(pad_end, jnp.arange(n_blocks, dtype=jnp.int32) * tm, side="right"),
                          n_experts - 1).astype(jnp.int32)
    n_used = (pad_end[-1:] // tm).astype(jnp.int32)
    dest_kt = dest.reshape(T, TOP_K).T.reshape(-1)
    return slot_tok, block_e, n_used, dest_kt


def _rot_half_cols(w):
    half = w.shape[-1] // 2
    return jnp.concatenate([-w[..., half:], w[..., :half]], axis=-1)


def kernel(x_prompt, x_sample, cache_mla_latent, cache_mla_krope, cache_conv, w_in, b_gate, gm_ln_g, gm_ln_b, gm_w_s, gm_b_s, gm_w_p, cv_w_dw, cv_b_dw, cv_ln_g, cv_ln_b, cv_w_pw, mla_q_norm_g, mla_kv_norm_g, mla_w_uq, mla_w_uk, mla_w_uv, mla_w_o, w_out, ln1_g, ln1_b, ln2_g, ln2_b, ffn_w1, ffn_w3, ffn_w2, moe_router, moe_w1, moe_w3, moe_w2):
    B, S, D = x_prompt.shape
    NB, L, _ = x_sample.shape
    depth = w_in.shape[0]
    past = cache_mla_latent.shape[2]
    GW = gm_ln_g.shape[1]
    CW = cv_ln_g.shape[1]
    RQ = mla_q_norm_g.shape[1]
    RKV = mla_kv_norm_g.shape[1]
    H = N_HEADS
    n_experts = moe_router.shape[2]
    alpha = float((2 * depth) ** 0.25)
    TP, TS = B * S, NB * L
    T = TP + TS
    off_cv = 2 * GW
    off_q = off_cv + 2 * CW
    off_kv = off_q + RQ
    off_kr = off_kv + RKV
    off_g = off_kr + QK_ROPE
    assert L <= GM_CHUNK and GM_CHUNK % L == 0 and S % GM_CHUNK == 0 and TS % GM_CHUNK == 0
    assert L >= CV_KERNEL - 1 and L % 8 == 0 and CV_KERNEL - 1 <= CONV_HALO

    w_uv = w_in[:, :, :off_cv].astype(BF16)
    w_a = w_in[:, :, off_cv:off_cv + CW].astype(BF16)
    w_b = w_in[:, :, off_cv + CW:off_q].astype(BF16)
    w_q = w_in[:, :, off_q:off_kv].astype(BF16)
    w_kv = w_in[:, :, off_kv:off_kr].astype(BF16)
    w_kr = w_in[:, :, off_kr:off_g]
    w_kr2 = jnp.concatenate([w_kr, _rot_half_cols(w_kr)], axis=-1).astype(BF16)
    w_g = w_in[:, :, off_g:].astype(BF16)
    zero_uv = jnp.zeros((1, off_cv), F32)

    uq = mla_w_uq.reshape(depth, RQ, H, QK_NOPE + QK_ROPE)
    uq_rope = uq[..., QK_NOPE:]
    wq3 = jnp.concatenate([uq[..., :QK_NOPE], uq_rope, _rot_half_cols(uq_rope)], axis=-1)
    wq3 = jnp.transpose(wq3, (0, 2, 1, 3)).astype(BF16)
    wkv3 = jnp.concatenate([mla_w_uk.reshape(depth, RKV, H, QK_NOPE),
                            mla_w_uv.reshape(depth, RKV, H, V_DIM)], axis=-1)
    wkv3 = jnp.transpose(wkv3, (0, 2, 1, 3)).astype(BF16)
    wq3t = jnp.swapaxes(wq3, 2, 3)
    wvt3 = jnp.swapaxes(wkv3[..., QK_NOPE:], 2, 3)

    causal = jnp.tril(jnp.ones((GM_CHUNK, GM_CHUNK), bool))
    ws_p = jnp.where(causal, gm_w_s, 0.0)
    reps = GM_CHUNK // L
    ws_l = jnp.where(causal[:L, :L], gm_w_s[:, :, :L, :L], 0.0)
    ws_s = jnp.einsum("ab,lgij->lgaibj", jnp.eye(reps, dtype=F32), ws_l).reshape(depth, GM_GROUPS, GM_CHUNK, GM_CHUNK)
    ws2 = jnp.stack([ws_p, ws_s], axis=1).astype(BF16)
    gd = GW // GM_GROUPS
    bs_p = jnp.repeat(jnp.transpose(gm_b_s, (0, 2, 1)), gd, axis=2)
    bs_s = jnp.repeat(jnp.tile(jnp.transpose(gm_b_s[:, :, :L], (0, 2, 1)), (1, reps, 1)), gd, axis=2)
    bs2 = jnp.stack([bs_p, bs_s], axis=1)

    gm_w_p_b = gm_w_p.astype(BF16)
    cv_w_pw_b = cv_w_pw.astype(BF16)
    w_o_b = mla_w_o.astype(BF16)
    w_out_b = w_out.astype(BF16)
    ffn_w1_b, ffn_w3_b, ffn_w2_b = ffn_w1.astype(BF16), ffn_w3.astype(BF16), ffn_w2.astype(BF16)
    moe_w1_b, moe_w3_b, moe_w2_b = moe_w1.astype(BF16), moe_w3.astype(BF16), moe_w2.astype(BF16)
    router_pad = jnp.pad(moe_router, ((0, 0), (0, 0), (0, LANES - n_experts)))

    half = QK_ROPE // 2
    inv = ROPE_THETA ** (-jnp.arange(half, dtype=F32) / half)
    pos = jnp.concatenate([jnp.tile(jnp.arange(S), B), jnp.tile(past + jnp.arange(L), NB)]).astype(F32)
    ang = pos[:, None] * inv[None, :]
    cos = jnp.tile(jnp.cos(ang), (1, 2))
    sin = jnp.tile(jnp.sin(ang), (1, 2))
    cos_t, sin_t = cos[:TP].T, sin[:TP].T
    scale_log2e = ATTN_SCALE * math.log2(math.e)

    hist_s = jnp.pad(cache_conv, ((0, 0), (0, 0), (CONV_HALO - (CV_KERNEL - 1), 0), (0, 0)))

    x = jnp.concatenate([x_prompt.reshape(TP, D), x_sample.reshape(TS, D)], axis=0)
    xb = x.astype(BF16)
    row2 = lambda a: a.reshape(1, -1)

    tm_dense = _tile(T, 512)
    eid_dense = jnp.zeros((T // tm_dense,), jnp.int32)
    nu_dense = jnp.full((1,), T // tm_dense, jnp.int32)

    outs = {k: [] for k in ("lat_p", "kr_p", "conv_p", "lat_s", "kr_s", "conv_s", "v_s")}
    for l in range(depth):
        uvg = _mm_act(xb, w_uv[l], zero_uv, "gelu", F32)
        glu = _mm_glu(xb, w_a[l], w_b[l])
        cqn, ckv, kr = _latents(xb, w_q[l], w_kv[l], w_kr2[l], row2(mla_q_norm_g[l]),
                                row2(mla_kv_norm_g[l]), cos, sin)

        us, v_ln = _gmlp(uvg, ws2[l], bs2[l], row2(gm_ln_g[l]), row2(gm_ln_b[l]), TP)

        cv_args = (cv_w_dw[l], row2(cv_b_dw[l]), row2(cv_ln_g[l]), row2(cv_ln_b[l]))
        yb_p = _conv(glu, glu, 0, B, S, *cv_args, zero_first=True)
        yb_s = _conv(glu, hist_s[l], TP, NB, L, *cv_args, zero_first=False)

        qt_p = _qproj_t(cqn, wq3t[l], cos_t, sin_t, B, S, scale_log2e)
        q_s = _qproj(cqn, wq3[l], cos, sin, TP, NB, L)
        k_p, vt_p = _kvproj(ckv, kr, wkv3[l], wvt3[l], B, S)
        o_p = _attn_prompt(qt_p, k_p, vt_p)
        o_s = _attn_sample(q_s, cache_mla_latent, cache_mla_krope, l, ckv, kr, wkv3[l], TP)

        merged = _merge(xb, us, yb_p, yb_s, o_p, o_s, gm_w_p_b[l], cv_w_pw_b[l], w_o_b[l], w_g[l], row2(b_gate[l]))
        x, xb = _outproj_ln(merged, w_out_b[l], x, row2(ln1_g[l]), row2(ln1_b[l]), alpha)

        j = l // 2
        if l % 2 == 0:
            x, xb = _ffn(xb, eid_dense, nu_dense, ffn_w1_b[j][None], ffn_w3_b[j][None], ffn_w2_b[j][None],
                         tm_dense, dense=(x, row2(ln2_g[l]), row2(ln2_b[l]), alpha))
        else:
            route = _router(x, router_pad[j], n_experts)
            slot_tok, block_e, n_used, dest_kt = _moe_plan(route, n_experts, MOE_TM)
            xg = _gather_rows(x, slot_tok)
            y = _ffn(xg, block_e, n_used, moe_w1_b[j], moe_w3_b[j], moe_w2_b[j], MOE_TM)
            yg = _gather_rows(y, dest_kt)
            x, xb = _combine(x, yg, route, row2(ln2_g[l]), row2(ln2_b[l]), alpha)

        outs["lat_p"].append(ckv[:TP].reshape(B, S, RKV))
        outs["kr_p"].append(kr[:TP].reshape(B, S, QK_ROPE))
        outs["conv_p"].append(jnp.stack([glu[(s + 1) * S - (CV_KERNEL - 1):(s + 1) * S] for s in range(B)]))
        outs["lat_s"].append(ckv[TP:].reshape(NB, L, RKV))
        outs["kr_s"].append(kr[TP:].reshape(NB, L, QK_ROPE))
        outs["conv_s"].append(glu[TP:].reshape(NB, L, CW)[:, L - (CV_KERNEL - 1):])
        outs["v_s"].append(v_ln[TP:].reshape(NB, L, GW))

    return (x[:TP].reshape(B, S, D), x[TP:].reshape(NB, L, D),
            jnp.stack(outs["lat_p"]), jnp.stack(outs["kr_p"]), jnp.stack(outs["conv_p"]),
            jnp.stack(outs["lat_s"]), jnp.stack(outs["kr_s"]), jnp.stack(outs["conv_s"]),
            jnp.stack(outs["v_s"]))
```

```python
import functools
import math

import jax
import jax.numpy as jnp
from jax import lax
from jax.experimental import pallas as pl
from jax.experimental.pallas import tpu as pltpu

F32 = jnp.float32
BF16 = jnp.bfloat16

CHUNK = 64
GM_GROUPS = 8
GM_CHUNK = 128
CV_KERNEL = 31
N_HEADS = 16
QK_NOPE = 128
QK_ROPE = 64
V_DIM = 128
QK_PAD = 256
V_ONES = 16
ROPE_THETA = 10000.0
TOP_K = 2
LN_EPS = 1e-5
RMS_EPS = 1e-6
ATTN_SCALE = (QK_NOPE + QK_ROPE) ** -0.5

V7X_VMEM_LIMIT_BYTES = 56 * 1024 * 1024
LANES = 128
SUBLANES = 8
CONV_HALO = 32
MOE_TM = 512
GATHER_UNROLL = 8


def _cp(*sem):
    return pltpu.CompilerParams(dimension_semantics=sem,
                                vmem_limit_bytes=V7X_VMEM_LIMIT_BYTES)


def _tile(n, pref, mult=8):
    if n <= pref:
        return n
    for t in range(pref, 0, -1):
        if n % t == 0 and t % mult == 0:
            return t
    return n


def _ln(x, g, b):
    mu = jnp.mean(x, axis=-1, keepdims=True)
    xc = x - mu
    var = jnp.mean(xc * xc, axis=-1, keepdims=True)
    return xc * lax.rsqrt(var + LN_EPS) * g + b


def _rms(x, g):
    ms = jnp.mean(x * x, axis=-1, keepdims=True)
    return x * lax.rsqrt(ms + RMS_EPS) * g


def _dot(a, b):
    return jnp.dot(a, b, preferred_element_type=F32)


def _dot_nt(a, b):
    return lax.dot_general(a, b, (((1,), (1,)), ((), ())), preferred_element_type=F32)


def _mm_act_kernel(x_ref, w_ref, b_ref, o_ref, *, act):
    acc = _dot(x_ref[...], w_ref[...]) + b_ref[...]
    if act == "gelu":
        acc = jax.nn.gelu(acc)
    elif act == "sigmoid":
        acc = jax.nn.sigmoid(acc)
    o_ref[...] = acc.astype(o_ref.dtype)


def _mm_act(x, w, b, act, out_dtype, tm=1024, tn=1024):
    M, K = x.shape
    N = w.shape[1]
    tm, tn = _tile(M, tm), _tile(N, tn, LANES)
    return pl.pallas_call(
        functools.partial(_mm_act_kernel, act=act),
        grid=(M // tm, N // tn),
        in_specs=[pl.BlockSpec((tm, K), lambda i, j: (i, 0)),
                  pl.BlockSpec((K, tn), lambda i, j: (0, j)),
                  pl.BlockSpec((1, tn), lambda i, j: (0, j))],
        out_specs=pl.BlockSpec((tm, tn), lambda i, j: (i, j)),
        out_shape=jax.ShapeDtypeStruct((M, N), out_dtype),
        compiler_params=_cp("parallel", "arbitrary"),
        name="mm_" + act,
    )(x, w, b)


def _mm_glu_kernel(x_ref, wa_ref, wb_ref, o_ref):
    x = x_ref[...]
    o_ref[...] = _dot(x, wa_ref[...]) * jax.nn.sigmoid(_dot(x, wb_ref[...]))


def _mm_glu(x, wa, wb, tm=1024, tn=512):
    M, K = x.shape
    N = wa.shape[1]
    tm, tn = _tile(M, tm), _tile(N, tn, LANES)
    return pl.pallas_call(
        _mm_glu_kernel,
        grid=(M // tm, N // tn),
        in_specs=[pl.BlockSpec((tm, K), lambda i, j: (i, 0)),
                  pl.BlockSpec((K, tn), lambda i, j: (0, j)),
                  pl.BlockSpec((K, tn), lambda i, j: (0, j))],
        out_specs=pl.BlockSpec((tm, tn), lambda i, j: (i, j)),
        out_shape=jax.ShapeDtypeStruct((M, N), F32),
        compiler_params=_cp("parallel", "arbitrary"),
        name="mm_glu",
    )(x, wa, wb)


def _latent_kernel(x_ref, wq_ref, wkv_ref, wkr_ref, gq_ref, gkv_ref, cos_ref, sin_ref,
                   cq_ref, ckv_ref, kr_ref):
    x = x_ref[...]
    cq_ref[...] = _rms(_dot(x, wq_ref[...]), gq_ref[...]).astype(cq_ref.dtype)
    ckv_ref[...] = _rms(_dot(x, wkv_ref[...]), gkv_ref[...])
    r = _dot(x, wkr_ref[...])
    kr_ref[...] = r[:, :QK_ROPE] * cos_ref[...] + r[:, QK_ROPE:] * sin_ref[...]


def _latents(x, wq, wkv, wkr2, gq, gkv, cos, sin, tm=512):
    M, K = x.shape
    Rq, Rkv = wq.shape[1], wkv.shape[1]
    tm = _tile(M, tm)
    row = lambda i: (i, 0)
    full = lambda i: (0, 0)
    return pl.pallas_call(
        _latent_kernel,
        grid=(M // tm,),
        in_specs=[pl.BlockSpec((tm, K), row),
                  pl.BlockSpec((K, Rq), full), pl.BlockSpec((K, Rkv), full),
                  pl.BlockSpec((K, 2 * QK_ROPE), full),
                  pl.BlockSpec((1, Rq), full), pl.BlockSpec((1, Rkv), full),
                  pl.BlockSpec((tm, QK_ROPE), row), pl.BlockSpec((tm, QK_ROPE), row)],
        out_specs=[pl.BlockSpec((tm, Rq), row), pl.BlockSpec((tm, Rkv), row),
                   pl.BlockSpec((tm, QK_ROPE), row)],
        out_shape=[jax.ShapeDtypeStruct((M, Rq), BF16), jax.ShapeDtypeStruct((M, Rkv), F32),
                   jax.ShapeDtypeStruct((M, QK_ROPE), F32)],
        compiler_params=_cp("parallel"),
        name="latents",
    )(x, wq, wkv, wkr2, gq, gkv, cos, sin)


def _gmlp_kernel(uv_ref, ws_ref, bs_ref, g_ref, b_ref, us_ref, v_ref, *, n_chunks, groups):
    W = v_ref.shape[1]
    gd = W // groups
    vn = _ln(uv_ref[:, W:], g_ref[...], b_ref[...])
    v_ref[...] = vn
    vb = vn.astype(BF16)
    for c in range(n_chunks):
        r0 = c * GM_CHUNK
        for g in range(groups):
            c0 = g * gd
            s = _dot(ws_ref[0, g], vb[r0:r0 + GM_CHUNK, c0:c0 + gd]) + bs_ref[0, :, c0:c0 + gd]
            u = uv_ref[r0:r0 + GM_CHUNK, c0:c0 + gd]
            us_ref[r0:r0 + GM_CHUNK, c0:c0 + gd] = (u * s).astype(us_ref.dtype)


def _gmlp(uvg, ws2, bs2, g, b, n_prompt_rows, tm=512):
    M, W2 = uvg.shape
    W = W2 // 2
    tm = _tile(math.gcd(n_prompt_rows, M - n_prompt_rows), tm, GM_CHUNK)
    npt = n_prompt_rows // tm
    sel = lambda i: (jnp.minimum(i // npt, 1), 0, 0, 0)
    sel3 = lambda i: (jnp.minimum(i // npt, 1), 0, 0)
    row = lambda i: (i, 0)
    full = lambda i: (0, 0)
    return pl.pallas_call(
        functools.partial(_gmlp_kernel, n_chunks=tm // GM_CHUNK, groups=GM_GROUPS),
        grid=(M // tm,),
        in_specs=[pl.BlockSpec((tm, W2), row),
                  pl.BlockSpec((1, GM_GROUPS, GM_CHUNK, GM_CHUNK), sel),
                  pl.BlockSpec((1, GM_CHUNK, W), sel3),
                  pl.BlockSpec((1, W), full), pl.BlockSpec((1, W), full)],
        out_specs=[pl.BlockSpec((tm, W), row), pl.BlockSpec((tm, W), row)],
        out_shape=[jax.ShapeDtypeStruct((M, W), BF16), jax.ShapeDtypeStruct((M, W), F32)],
        compiler_params=_cp("parallel"),
        name="gmlp",
    )(uvg, ws2, bs2, g, b)


def _conv_kernel(cur_ref, halo_ref, w_ref, bdw_ref, g_ref, b_ref, o_ref, full_ref, sh_ref, acc_ref,
                 *, tm, zero_first):
    C = cur_ref.shape[1]
    halo = halo_ref[...].reshape(CONV_HALO, C)
    if zero_first:
        halo = jnp.where(pl.program_id(1) == 0, 0.0, halo)
    full_ref[0:CONV_HALO, :] = halo
    full_ref[CONV_HALO:CONV_HALO + tm, :] = cur_ref[...]
    off = CONV_HALO - (CV_KERNEL - 1)
    ns = sh_ref.shape[1]

    def lane_block(c, carry):
        c0 = pl.multiple_of(c * LANES, LANES)
        sh_ref[0] = full_ref[:, pl.ds(c0, LANES)]
        for r in range(1, SUBLANES):
            sh_ref[r, :ns - SUBLANES, :] = full_ref[pl.ds(r, ns - SUBLANES), pl.ds(c0, LANES)]
        wk = w_ref[:, pl.ds(c0, LANES)]
        wrows = [jnp.broadcast_to(wk[k:k + 1, :], (SUBLANES, LANES)) for k in range(CV_KERNEL)]
        for rb in range(tm // SUBLANES):
            acc = jnp.zeros((SUBLANES, LANES), F32)
            for k in range(CV_KERNEL):
                a, r = divmod(off + k, SUBLANES)
                acc = acc + sh_ref[r, pl.ds((rb + a) * SUBLANES, SUBLANES), :] * wrows[k]
            acc_ref[pl.ds(rb * SUBLANES, SUBLANES), pl.ds(c0, LANES)] = acc
        return carry

    lax.fori_loop(0, C // LANES, lane_block, 0)
    y = _ln(acc_ref[...] + bdw_ref[...], g_ref[...], b_ref[...])
    o_ref[...] = (y * jax.nn.sigmoid(y)).astype(o_ref.dtype)


def _conv(glu, halo_src, row0, n_seq, L, w_dw, b_dw, g, b, zero_first, tm=256):
    C = glu.shape[1]
    tm = _tile(L, tm, CONV_HALO)
    lt = L // tm
    hb = tm // CONV_HALO
    assert row0 % tm == 0
    r0 = row0 // tm
    if zero_first:
        halo_spec = pl.BlockSpec((CONV_HALO, C), lambda s, i: (jnp.maximum((r0 + s * lt + i) * hb - 1, 0), 0))
    else:
        halo_spec = pl.BlockSpec((1, CONV_HALO, C), lambda s, i: (s, 0, 0))
    full = lambda s, i: (0, 0)
    return pl.pallas_call(
        functools.partial(_conv_kernel, tm=tm, zero_first=zero_first),
        grid=(n_seq, lt),
        in_specs=[pl.BlockSpec((tm, C), lambda s, i: (r0 + s * lt + i, 0)),
                  halo_spec,
                  pl.BlockSpec((CV_KERNEL, C), full),
                  pl.BlockSpec((1, C), full), pl.BlockSpec((1, C), full), pl.BlockSpec((1, C), full)],
        out_specs=pl.BlockSpec((tm, C), lambda s, i: (s * lt + i, 0)),
        out_shape=jax.ShapeDtypeStruct((n_seq * L, C), BF16),
        scratch_shapes=[pltpu.VMEM((CONV_HALO + tm, C), F32),
                        pltpu.VMEM((SUBLANES, CONV_HALO + tm, LANES), F32),
                        pltpu.VMEM((tm, C), F32)],
        compiler_params=_cp("parallel", "arbitrary"),
        name="conv_prompt" if zero_first else "conv_sample",
    )(glu, halo_src, w_dw, b_dw, g, b)


def _qproj_kernel(cq_ref, w_ref, cos_ref, sin_ref, q_ref, *, hg, nb, L):
    cq = cq_ref[...]
    cos, sin = cos_ref[...], sin_ref[...]
    for h in range(hg):
        r = _dot(cq, w_ref[h])
        nope = r[:, :QK_NOPE] * ATTN_SCALE
        rp = (r[:, QK_NOPE:QK_NOPE + QK_ROPE] * cos + r[:, QK_NOPE + QK_ROPE:] * sin) * ATTN_SCALE
        if nb == 1:
            q_ref[0, h, :, :QK_NOPE] = nope.astype(q_ref.dtype)
            q_ref[0, h, :, QK_NOPE:] = rp.astype(q_ref.dtype)
        else:
            q_ref[:, h, :, :QK_NOPE] = nope.reshape(nb, L, QK_NOPE).astype(q_ref.dtype)
            q_ref[:, h, :, QK_NOPE:] = rp.reshape(nb, L, QK_ROPE).astype(q_ref.dtype)


def _qproj(cqn, wq3, cos, sin, row0, n_seq, L, hg=4, tm=512):
    R = cqn.shape[1]
    H = wq3.shape[0]
    hg = _tile(H, hg, 1)
    dq = QK_NOPE + QK_ROPE
    if L >= tm:
        tm = _tile(L, tm)
        nb, lt = 1, L // tm
        out_spec = pl.BlockSpec((1, hg, tm, dq), lambda i, h: (i // lt, h, i % lt, 0))
    else:
        nb = _tile(n_seq, max(tm // L, 1), 1)
        tm = nb * L
        out_spec = pl.BlockSpec((nb, hg, L, dq), lambda i, h: (i, h, 0, 0))
    assert row0 % tm == 0
    r0 = row0 // tm
    row = lambda i, h: (r0 + i, 0)
    return pl.pallas_call(
        functools.partial(_qproj_kernel, hg=hg, nb=nb, L=L),
        grid=(n_seq * L // tm, H // hg),
        in_specs=[pl.BlockSpec((tm, R), row),
                  pl.BlockSpec((hg, R, wq3.shape[2]), lambda i, h: (h, 0, 0)),
                  pl.BlockSpec((tm, QK_ROPE), row), pl.BlockSpec((tm, QK_ROPE), row)],
        out_specs=out_spec,
        out_shape=jax.ShapeDtypeStruct((n_seq, H, L, dq), BF16),
        compiler_params=_cp("parallel", "arbitrary"),
        name="qproj",
    )(cqn, wq3, cos, sin)


def _qproj_t_kernel(cq_ref, w_ref, cos_ref, sin_ref, q_ref, *, hg, scale):
    cq = cq_ref[...]
    cos, sin = cos_ref[...], sin_ref[...]
    for h in range(hg):
        r = _dot_nt(w_ref[h], cq)
        rp = r[QK_NOPE:QK_NOPE + QK_ROPE] * cos + r[QK_NOPE + QK_ROPE:] * sin
        q_ref[0, h, :QK_NOPE, :] = (r[:QK_NOPE] * scale).astype(q_ref.dtype)
        q_ref[0, h, QK_NOPE:QK_NOPE + QK_ROPE, :] = (rp * scale).astype(q_ref.dtype)
        q_ref[0, h, QK_NOPE + QK_ROPE:, :] = jnp.zeros((QK_PAD - QK_NOPE - QK_ROPE, cq.shape[0]), q_ref.dtype)


def _qproj_t(cqn, wq3t, cos_t, sin_t, n_seq, L, scale, hg=4, tm=512):
    R = cqn.shape[1]
    H = wq3t.shape[0]
    hg = _tile(H, hg, 1)
    tm = _tile(L, tm, LANES)
    lt = L // tm
    dq = QK_PAD
    return pl.pallas_call(
        functools.partial(_qproj_t_kernel, hg=hg, scale=scale),
        grid=(n_seq * lt, H // hg),
        in_specs=[pl.BlockSpec((tm, R), lambda i, h: (i, 0)),
                  pl.BlockSpec((hg, wq3t.shape[1], R), lambda i, h: (h, 0, 0)),
                  pl.BlockSpec((QK_ROPE, tm), lambda i, h: (0, i)),
                  pl.BlockSpec((QK_ROPE, tm), lambda i, h: (0, i))],
        out_specs=pl.BlockSpec((1, hg, dq, tm), lambda i, h: (i // lt, h, 0, i % lt)),
        out_shape=jax.ShapeDtypeStruct((n_seq, H, dq, L), BF16),
        compiler_params=_cp("parallel", "arbitrary"),
        name="qproj_t",
    )(cqn, wq3t, cos_t, sin_t)


def _kvproj_kernel(ckv_ref, kr_ref, w_ref, k_ref, vt_ref, *, hg):
    ckv = ckv_ref[...].astype(BF16)
    kr = kr_ref[...].astype(BF16)
    for h in range(hg):
        r = _dot(ckv, w_ref[h])
        k_ref[0, h, :, :QK_NOPE] = r[:, :QK_NOPE].astype(k_ref.dtype)
        k_ref[0, h, :, QK_NOPE:QK_NOPE + QK_ROPE] = kr
        k_ref[0, h, :, QK_NOPE + QK_ROPE:] = jnp.zeros((kr.shape[0], QK_PAD - QK_NOPE - QK_ROPE), k_ref.dtype)
        vt_ref[0, h, :V_DIM, :] = r[:, QK_NOPE:].T.astype(vt_ref.dtype)
        vt_ref[0, h, V_DIM:, :] = jnp.ones((V_ONES, ckv.shape[0]), vt_ref.dtype)


def _kvproj(ckv, kr, wkv3, n_seq, L, hg=4, tm=512):
    R = ckv.shape[1]
    H = wkv3.shape[0]
    hg = _tile(H, hg, 1)
    tm = _tile(L, tm, LANES)
    lt = L // tm
    dq = QK_PAD
    row = lambda i, h: (i, 0)
    wmap = lambda i, h: (h, 0, 0)
    return pl.pallas_call(
        functools.partial(_kvproj_kernel, hg=hg),
        grid=(n_seq * lt, H // hg),
        in_specs=[pl.BlockSpec((tm, R), row), pl.BlockSpec((tm, QK_ROPE), row),
                  pl.BlockSpec((hg, R, wkv3.shape[2]), wmap)],
        out_specs=[pl.BlockSpec((1, hg, tm, dq), lambda i, h: (i // lt, h, i % lt, 0)),
                   pl.BlockSpec((1, hg, V_DIM + V_ONES, tm), lambda i, h: (i // lt, h, 0, i % lt))],
        out_shape=[jax.ShapeDtypeStruct((n_seq, H, L, dq), BF16),
                   jax.ShapeDtypeStruct((n_seq, H, V_DIM + V_ONES, L), BF16)],
        compiler_params=_cp("parallel", "arbitrary"),
        name="kvproj",
    )(ckv, kr, wkv3)


def _attn_prompt_kernel(qt_ref, k_ref, vt_ref, o_ref, s_ref, p_ref, acc_ref, *, tq, tk, hg):
    qi = pl.program_id(2)
    d0 = pl.multiple_of(qi * tq, tq)

    def scores(g, j0, slot):
        s = _dot(k_ref[0, g, pl.ds(j0, tk), :], qt_ref[0, g])
        s_ref[slot, g] = s
        return jnp.max(s, axis=0, keepdims=True)

    def softmax(g, slot, mx, m, mask=None):
        s = s_ref[slot, g]
        if mask is not None:
            s = jnp.where(mask, s, -jnp.inf)
            mx = jnp.max(s, axis=0, keepdims=True)
        m_new = jnp.maximum(m, mx)
        p_ref[slot, g] = jnp.exp2(s - m_new).astype(BF16)
        return jnp.exp2(m - m_new), m_new

    def accumulate(g, j0, slot, alpha):
        acc_ref[g] = alpha * acc_ref[g] + _dot(vt_ref[0, g, :, pl.ds(j0, tk)], p_ref[slot, g])

    def step(j0, slot, state, mask=None, last=False):
        jp = pl.multiple_of(jnp.maximum(j0 - tk, 0), tk)
        new = []
        for g in range(hg):
            mx, a_prev, m = state[g]
            mx_next = mx if last else scores(g, pl.multiple_of(j0 + tk, tk), 1 - slot)
            accumulate(g, jp, 1 - slot, a_prev)
            alpha, m = softmax(g, slot, mx, m, mask)
            new.append((mx_next, alpha, m))
        return tuple(new)

    def body(i, state):
        j0 = pl.multiple_of(2 * i * tk, tk)
        state = step(j0, 0, state)
        return step(pl.multiple_of(j0 + tk, tk), 1, state)

    p_ref[1] = jnp.zeros_like(p_ref[1])
    acc_ref[...] = jnp.zeros_like(acc_ref)
    init = tuple((scores(g, 0, 0), jnp.ones((1, tq), F32), jnp.full((1, tq), -jnp.inf, F32))
                 for g in range(hg))
    state = lax.fori_loop(0, qi * (tq // (2 * tk)), body, init)

    key_chunk = lax.broadcasted_iota(jnp.int32, (tk, tq), 0) // CHUNK
    qry_chunk = lax.broadcasted_iota(jnp.int32, (tk, tq), 1) // CHUNK
    state = step(d0, 0, state, mask=key_chunk <= qry_chunk)
    state = step(pl.multiple_of(d0 + tk, tk), 1, state, mask=key_chunk + tk // CHUNK <= qry_chunk, last=True)
    for g in range(hg):
        accumulate(g, pl.multiple_of(d0 + tk, tk), 1, state[g][1])
        o = acc_ref[g, :V_DIM, :] / acc_ref[g, V_DIM:V_DIM + 1, :]
        o_ref[:, g * V_DIM:(g + 1) * V_DIM] = o.T.astype(o_ref.dtype)


def _attn_prompt(qt, k, vt, tq=512, hg=2):
    B, H, dq, S = qt.shape
    tq = _tile(S, tq, 2 * LANES)
    tk = tq // 2
    hg = _tile(H, hg, 1)
    return pl.pallas_call(
        functools.partial(_attn_prompt_kernel, tq=tq, tk=tk, hg=hg),
        grid=(B, H // hg, S // tq),
        in_specs=[pl.BlockSpec((1, hg, dq, tq), lambda b, h, i: (b, h, 0, i)),
                  pl.BlockSpec((1, hg, S, dq), lambda b, h, i: (b, h, 0, 0)),
                  pl.BlockSpec((1, hg, vt.shape[2], S), lambda b, h, i: (b, h, 0, 0))],
        out_specs=pl.BlockSpec((tq, hg * V_DIM), lambda b, h, i: (b * (S // tq) + i, h)),
        out_shape=jax.ShapeDtypeStruct((B * S, H * V_DIM), BF16),
        scratch_shapes=[pltpu.VMEM((2, hg, tk, tq), F32), pltpu.VMEM((2, hg, tk, tq), BF16),
                        pltpu.VMEM((hg, vt.shape[2], tq), F32)],
        compiler_params=_cp("parallel", "parallel", "arbitrary"),
        name="attn_prompt",
    )(qt, k, vt)


def _attn_sample_kernel(q_ref, plat_ref, pkrt_ref, nlat_ref, nkr_ref, w_ref, o_ref, ql_ref, qr_ref,
                        *, H, L, kc):
    P = plat_ref.shape[2]
    for h in range(H):
        qh = q_ref[0, h]
        ql_ref[h * L:(h + 1) * L, :] = _dot_nt(qh[:, :QK_NOPE], w_ref[h, :, :QK_NOPE]).astype(BF16)
        qr_ref[h * L:(h + 1) * L, :] = qh[:, QK_NOPE:]
    ql = ql_ref[...]
    qr = qr_ref[...]

    def step(lat, s_rope, carry):
        m, l, acc = carry
        s = _dot_nt(ql, lat) + s_rope
        m_new = jnp.maximum(m, jnp.max(s, axis=-1, keepdims=True))
        alpha = jnp.exp(m - m_new)
        p = jnp.exp(s - m_new)
        l = alpha * l + jnp.sum(p, axis=-1, keepdims=True)
        acc = alpha * acc + _dot(p.astype(BF16), lat)
        return m_new, l, acc

    R = ql.shape[1]
    carry = (jnp.full((H * L, 1), -jnp.inf, F32), jnp.zeros((H * L, 1), F32), jnp.zeros((H * L, R), F32))
    for c in range(P // kc):
        krt = pkrt_ref[0, 0, :, c * kc:(c + 1) * kc].astype(BF16)
        carry = step(plat_ref[0, 0, c * kc:(c + 1) * kc, :].astype(BF16), _dot(qr, krt), carry)
    m, l, acc = step(nlat_ref[...].astype(BF16), _dot_nt(qr, nkr_ref[...].astype(BF16)), carry)
    ol = (acc / l).astype(BF16)
    for h in range(H):
        o_ref[:, h * V_DIM:(h + 1) * V_DIM] = _dot(ol[h * L:(h + 1) * L, :], w_ref[h, :, QK_NOPE:]).astype(o_ref.dtype)


def _attn_sample(q, cache_lat, cache_kr_t, layer, ckv, kr, wkv3, row0):
    n_req, H, L, dq = q.shape
    P, R = cache_lat.shape[2], cache_lat.shape[3]
    kc = _tile(P, 1024)
    assert row0 % L == 0
    r0 = row0 // L
    return pl.pallas_call(
        functools.partial(_attn_sample_kernel, H=H, L=L, kc=kc),
        grid=(n_req,),
        in_specs=[pl.BlockSpec((1, H, L, dq), lambda b: (b, 0, 0, 0)),
                  pl.BlockSpec((1, 1, P, R), lambda b: (layer, b, 0, 0)),
                  pl.BlockSpec((1, 1, QK_ROPE, P), lambda b: (layer, b, 0, 0)),
                  pl.BlockSpec((L, R), lambda b: (r0 + b, 0)),
                  pl.BlockSpec((L, QK_ROPE), lambda b: (r0 + b, 0)),
                  pl.BlockSpec(wkv3.shape, lambda b: (0, 0, 0))],
        out_specs=pl.BlockSpec((L, H * V_DIM), lambda b: (b, 0)),
        out_shape=jax.ShapeDtypeStruct((n_req * L, H * V_DIM), BF16),
        scratch_shapes=[pltpu.VMEM((H * L, R), BF16), pltpu.VMEM((H * L, QK_ROPE), BF16)],
        compiler_params=_cp("parallel"),
        name="attn_sample",
    )(q, cache_lat, cache_kr_t, ckv, kr, wkv3)


def _merge_kernel(x_ref, us_ref, ybp_ref, ybs_ref, op_ref, os_ref, wp_ref, wpw_ref, wo_ref,
                  wga_ref, wgb_ref, wgc_ref, ba_ref, bb_ref, bc_ref, out_ref, *, npt):
    prompt = pl.program_id(0) < npt
    x = x_ref[...]
    yb_in = jnp.where(prompt, ybp_ref[...], ybs_ref[...])
    o_in = jnp.where(prompt, op_ref[...], os_ref[...])
    out = jax.nn.sigmoid(_dot(x, wga_ref[...]) + ba_ref[...]) * _dot(us_ref[...], wp_ref[...])
    out += jax.nn.sigmoid(_dot(x, wgb_ref[...]) + bb_ref[...]) * _dot(yb_in, wpw_ref[...])
    out += jax.nn.sigmoid(_dot(x, wgc_ref[...]) + bc_ref[...]) * _dot(o_in, wo_ref[...])
    out_ref[...] = out.astype(out_ref.dtype)


def _merge(xb, us, yb_p, yb_s, o_p, o_s, wp, wpw, wo, wg, bg, tm=512, tn=512):
    M = us.shape[0]
    D = wp.shape[1]
    TP, TS = yb_p.shape[0], yb_s.shape[0]
    tm, tn = _tile(math.gcd(TP, TS), tm), _tile(D, tn, LANES)
    npt = TP // tm
    nj = D // tn
    row = lambda i, j: (i, 0)
    prow = lambda i, j: (jnp.minimum(i, npt - 1), 0)
    srow = lambda i, j: (jnp.maximum(i - npt, 0), 0)
    col = lambda i, j: (0, j)
    gcol = lambda g: (lambda i, j: (0, g * nj + j))
    return pl.pallas_call(
        functools.partial(_merge_kernel, npt=npt),
        grid=(M // tm, nj),
        in_specs=[pl.BlockSpec((tm, xb.shape[1]), row), pl.BlockSpec((tm, us.shape[1]), row),
                  pl.BlockSpec((tm, yb_p.shape[1]), prow), pl.BlockSpec((tm, yb_s.shape[1]), srow),
                  pl.BlockSpec((tm, o_p.shape[1]), prow), pl.BlockSpec((tm, o_s.shape[1]), srow),
                  pl.BlockSpec((wp.shape[0], tn), col), pl.BlockSpec((wpw.shape[0], tn), col),
                  pl.BlockSpec((wo.shape[0], tn), col),
                  pl.BlockSpec((wg.shape[0], tn), gcol(0)), pl.BlockSpec((wg.shape[0], tn), gcol(1)),
                  pl.BlockSpec((wg.shape[0], tn), gcol(2)),
                  pl.BlockSpec((1, tn), gcol(0)), pl.BlockSpec((1, tn), gcol(1)), pl.BlockSpec((1, tn), gcol(2))],
        out_specs=pl.BlockSpec((tm, tn), lambda i, j: (i, j)),
        out_shape=jax.ShapeDtypeStruct((M, D), BF16),
        compiler_params=_cp("parallel", "arbitrary"),
        name="merge",
    )(xb, us, yb_p, yb_s, o_p, o_s, wp, wpw, wo, wg, wg, wg, bg, bg, bg)


def _outproj_kernel(m_ref, w_ref, x_ref, g_ref, b_ref, o_ref, ob_ref, *, alpha):
    y = _ln(alpha * x_ref[...] + _dot(m_ref[...], w_ref[...]), g_ref[...], b_ref[...])
    o_ref[...] = y
    ob_ref[...] = y.astype(ob_ref.dtype)


def _outproj_ln(merged, w, x, g, b, alpha, tm=512):
    M, D = x.shape
    tm = _tile(M, tm)
    row = lambda i: (i, 0)
    full = lambda i: (0, 0)
    return pl.pallas_call(
        functools.partial(_outproj_kernel, alpha=alpha),
        grid=(M // tm,),
        in_specs=[pl.BlockSpec((tm, merged.shape[1]), row), pl.BlockSpec(w.shape, full),
                  pl.BlockSpec((tm, D), row), pl.BlockSpec((1, D), full), pl.BlockSpec((1, D), full)],
        out_specs=[pl.BlockSpec((tm, D), row), pl.BlockSpec((tm, D), row)],
        out_shape=[jax.ShapeDtypeStruct((M, D), F32), jax.ShapeDtypeStruct((M, D), BF16)],
        compiler_params=_cp("parallel"),
        name="outproj_ln",
    )(merged, w, x, g, b)


def _ffn_dense_kernel(eid_ref, nu_ref, x_ref, w1_ref, w3_ref, w2_ref, r_ref, g_ref, b_ref,
                      o_ref, ob_ref, acc_ref, *, alpha):
    f = pl.program_id(1)

    @pl.when(f == 0)
    def _():
        acc_ref[...] = jnp.zeros_like(acc_ref)

    x = x_ref[...]
    h = jax.nn.silu(_dot(x, w1_ref[0])) * _dot(x, w3_ref[0])
    acc_ref[...] += _dot(h.astype(BF16), w2_ref[0])

    @pl.when(f == pl.num_programs(1) - 1)
    def _():
        y = _ln(alpha * r_ref[...] + acc_ref[...], g_ref[...], b_ref[...])
        o_ref[...] = y
        ob_ref[...] = y.astype(ob_ref.dtype)


def _ffn_moe_kernel(eid_ref, nu_ref, x_ref, w1_ref, w3_ref, w2_ref, o_ref, xb_ref):
    i = pl.program_id(0)
    f = pl.program_id(1)

    @pl.when(f == 0)
    def _():
        o_ref[...] = jnp.zeros_like(o_ref)
        xb_ref[...] = x_ref[...].astype(BF16)

    @pl.when(i < nu_ref[0])
    def _():
        x = xb_ref[...]
        h = jax.nn.silu(_dot(x, w1_ref[0])) * _dot(x, w3_ref[0])
        o_ref[...] += _dot(h.astype(BF16), w2_ref[0])


def _ffn(x, eid, n_used, w1, w3, w2, tm, tf=512, dense=None):
    M, D = x.shape
    Fdim = w1.shape[2]
    tf = _tile(Fdim, tf, LANES)
    last = lambda i, nu: jnp.minimum(i, nu[0] - 1)
    xmap = lambda i, f, e, nu: (last(i, nu), 0)
    w13 = lambda i, f, e, nu: (e[last(i, nu)], 0, jnp.where(i < nu[0], f, 0))
    w2m = lambda i, f, e, nu: (e[last(i, nu)], jnp.where(i < nu[0], f, 0), 0)
    row = lambda i, f, e, nu: (i, 0)
    full = lambda i, f, e, nu: (0, 0)
    in_specs = [pl.BlockSpec((tm, D), xmap),
                pl.BlockSpec((1, D, tf), w13), pl.BlockSpec((1, D, tf), w13),
                pl.BlockSpec((1, tf, D), w2m)]
    if dense is not None:
        resid, g, b, alpha = dense
        kern = functools.partial(_ffn_dense_kernel, alpha=alpha)
        in_specs += [pl.BlockSpec((tm, D), row), pl.BlockSpec((1, D), full), pl.BlockSpec((1, D), full)]
        out_specs = [pl.BlockSpec((tm, D), row), pl.BlockSpec((tm, D), row)]
        out_shape = [jax.ShapeDtypeStruct((M, D), F32), jax.ShapeDtypeStruct((M, D), BF16)]
        scratch = [pltpu.VMEM((tm, D), F32)]
        args = (x, w1, w3, w2, resid, g, b)
    else:
        kern = _ffn_moe_kernel
        out_specs = pl.BlockSpec((tm, D), row)
        out_shape = jax.ShapeDtypeStruct((M, D), F32)
        scratch = [pltpu.VMEM((tm, D), BF16)]
        args = (x, w1, w3, w2)
    return pl.pallas_call(
        kern,
        grid_spec=pltpu.PrefetchScalarGridSpec(
            num_scalar_prefetch=2, grid=(M // tm, Fdim // tf),
            in_specs=in_specs, out_specs=out_specs, scratch_shapes=scratch),
        out_shape=out_shape,
        compiler_params=_cp("arbitrary", "arbitrary"),
        name="ffn_dense" if dense is not None else "ffn_moe",
    )(eid, n_used, *args)


def _router_kernel(x_ref, r_ref, o_ref, *, n_experts):
    logits = jnp.dot(x_ref[...], r_ref[...], preferred_element_type=F32,
                     precision=lax.Precision.HIGHEST)
    col = lax.broadcasted_iota(jnp.int32, logits.shape, 1)
    lg = jnp.where(col < n_experts, logits, -jnp.inf)
    m1 = jnp.max(lg, axis=-1, keepdims=True)
    i1 = jnp.min(jnp.where(lg == m1, col, LANES), axis=-1, keepdims=True)
    lg2 = jnp.where(col == i1, -jnp.inf, lg)
    m2 = jnp.max(lg2, axis=-1, keepdims=True)
    i2 = jnp.min(jnp.where(lg2 == m2, col, LANES), axis=-1, keepdims=True)
    e = jnp.exp(m2 - m1)
    g1 = 1.0 / (1.0 + e)
    g2 = e * g1
    out = jnp.where(col == 0, i1.astype(F32),
                    jnp.where(col == 1, i2.astype(F32),
                              jnp.where(col == 2, g1, jnp.where(col == 3, g2, 0.0))))
    o_ref[...] = out


def _router(x, router_pad, n_experts, tm=512):
    M, D = x.shape
    tm = _tile(M, tm)
    return pl.pallas_call(
        functools.partial(_router_kernel, n_experts=n_experts),
        grid=(M // tm,),
        in_specs=[pl.BlockSpec((tm, D), lambda i: (i, 0)), pl.BlockSpec((D, LANES), lambda i: (0, 0))],
        out_specs=pl.BlockSpec((tm, LANES), lambda i: (i, 0)),
        out_shape=jax.ShapeDtypeStruct((M, LANES), F32),
        compiler_params=_cp("parallel"),
        name="router",
    )(x, router_pad)


def _gather_kernel(idx_ref, src_ref, o_ref, sem, *, tm):
    def row_copy(r, t):
        return pltpu.make_async_copy(src_ref.at[pl.ds(t, 1), :], o_ref.at[pl.ds(r, 1), :], sem)

    def start(c, carry):
        for u in range(GATHER_UNROLL):
            r = c * GATHER_UNROLL + u
            row_copy(r, idx_ref[0, 0, r]).start(priority=u % 2)
        return carry

    def wait(c, carry):
        for u in range(GATHER_UNROLL):
            row_copy(c * GATHER_UNROLL + u, 0).wait()
        return carry

    lax.fori_loop(0, tm // GATHER_UNROLL, start, 0)
    lax.fori_loop(0, tm // GATHER_UNROLL, wait, 0)


def _gather_rows(src, idx, tm=256):
    M = idx.shape[0]
    D = src.shape[1]
    tm = _tile(M, tm)
    idx3 = idx.reshape(M // tm, 1, tm)
    return pl.pallas_call(
        functools.partial(_gather_kernel, tm=tm),
        grid=(M // tm,),
        in_specs=[pl.BlockSpec((1, 1, tm), lambda i: (i, 0, 0), memory_space=pltpu.SMEM),
                  pl.BlockSpec(memory_space=pl.ANY)],
        out_specs=pl.BlockSpec((tm, D), lambda i: (i, 0)),
        out_shape=jax.ShapeDtypeStruct((M, D), src.dtype),
        scratch_shapes=[pltpu.SemaphoreType.DMA(())],
        compiler_params=_cp("arbitrary"),
        name="gather_rows",
    )(idx3, src)


def _combine_kernel(x_ref, y1_ref, y2_ref, r_ref, g_ref, b_ref, o_ref, ob_ref, *, alpha):
    r = r_ref[...]
    y = alpha * x_ref[...] + r[:, 2:3] * y1_ref[...] + r[:, 3:4] * y2_ref[...]
    y = _ln(y, g_ref[...], b_ref[...])
    o_ref[...] = y
    ob_ref[...] = y.astype(ob_ref.dtype)


def _combine(x, yg, route, g, b, alpha, tm=512):
    M, D = x.shape
    tm = _tile(M, tm)
    nt = M // tm
    row = lambda i: (i, 0)
    full = lambda i: (0, 0)
    return pl.pallas_call(
        functools.partial(_combine_kernel, alpha=alpha),
        grid=(nt,),
        in_specs=[pl.BlockSpec((tm, D), row), pl.BlockSpec((tm, D), row),
                  pl.BlockSpec((tm, D), lambda i: (nt + i, 0)),
                  pl.BlockSpec((tm, LANES), row), pl.BlockSpec((1, D), full), pl.BlockSpec((1, D), full)],
        out_specs=[pl.BlockSpec((tm, D), row), pl.BlockSpec((tm, D), row)],
        out_shape=[jax.ShapeDtypeStruct((M, D), F32), jax.ShapeDtypeStruct((M, D), BF16)],
        compiler_params=_cp("parallel"),
        name="combine",
    )(x, yg, yg, route, g, b)


def _moe_plan(route, n_experts, tm):
    T = route.shape[0]
    n_assign = T * TOP_K
    e_flat = route[:, :TOP_K].astype(jnp.int32).reshape(-1)
    onehot = (e_flat[:, None] == jnp.arange(n_experts, dtype=jnp.int32)[None, :]).astype(jnp.int32)
    csum = jnp.cumsum(onehot, axis=0)
    rank = jnp.take_along_axis(csum, e_flat[:, None], axis=1)[:, 0] - 1
    counts = csum[-1]
    padded = ((counts + tm - 1) // tm) * tm
    pad_end = jnp.cumsum(padded)
    pad_start = pad_end - padded
    dest = (pad_start[e_flat] + rank).astype(jnp.int32)
    n_blocks = -(-n_assign // tm) + n_experts
    n_slots = n_blocks * tm
    slot_tok = jnp.zeros((n_slots,), jnp.int32).at[dest].set(jnp.arange(n_assign, dtype=jnp.int32) // TOP_K)
    block_e = jnp.minimum(jnp.searchsorted(pad_end, jnp.arange(n_blocks, dtype=jnp.int32) * tm, side="right"),
                          n_experts - 1).astype(jnp.int32)
    n_used = (pad_end[-1:] // tm).astype(jnp.int32)
    dest_kt = dest.reshape(T, TOP_K).T.reshape(-1)
    return slot_tok, block_e, n_used, dest_kt


def _rot_half_cols(w):
    half = w.shape[-1] // 2
    return jnp.concatenate([-w[..., half:], w[..., :half]], axis=-1)


def kernel(x_prompt, x_sample, cache_mla_latent, cache_mla_krope, cache_conv, w_in, b_gate, gm_ln_g, gm_ln_b, gm_w_s, gm_b_s, gm_w_p, cv_w_dw, cv_b_dw, cv_ln_g, cv_ln_b, cv_w_pw, mla_q_norm_g, mla_kv_norm_g, mla_w_uq, mla_w_uk, mla_w_uv, mla_w_o, w_out, ln1_g, ln1_b, ln2_g, ln2_b, ffn_w1, ffn_w3, ffn_w2, moe_router, moe_w1, moe_w3, moe_w2):
    B, S, D = x_prompt.shape
    NB, L, _ = x_sample.shape
    depth = w_in.shape[0]
    past = cache_mla_latent.shape[2]
    GW = gm_ln_g.shape[1]
    CW = cv_ln_g.shape[1]
    RQ = mla_q_norm_g.shape[1]
    RKV = mla_kv_norm_g.shape[1]
    H = N_HEADS
    n_experts = moe_router.shape[2]
    alpha = float((2 * depth) ** 0.25)
    TP, TS = B * S, NB * L
    T = TP + TS
    off_cv = 2 * GW
    off_q = off_cv + 2 * CW
    off_kv = off_q + RQ
    off_kr = off_kv + RKV
    off_g = off_kr + QK_ROPE
    assert L <= GM_CHUNK and GM_CHUNK % L == 0 and S % GM_CHUNK == 0 and TS % GM_CHUNK == 0
    assert L >= CV_KERNEL - 1 and L % 8 == 0 and CV_KERNEL - 1 <= CONV_HALO

    def in_proj_weights(l):
        seg = lambda a, b: w_in[l, :, a:b]
        w_kr = seg(off_kr, off_g)
        return dict(uv=seg(0, off_cv).astype(BF16), a=seg(off_cv, off_cv + CW).astype(BF16),
                    b=seg(off_cv + CW, off_q).astype(BF16), q=seg(off_q, off_kv).astype(BF16),
                    kv=seg(off_kv, off_kr).astype(BF16),
                    kr2=jnp.concatenate([w_kr, _rot_half_cols(w_kr)], axis=-1).astype(BF16),
                    g=seg(off_g, w_in.shape[2]).astype(BF16))

    zero_uv = jnp.zeros((1, off_cv), F32)

    uq = mla_w_uq.reshape(depth, RQ, H, QK_NOPE + QK_ROPE)
    uq_rope = uq[..., QK_NOPE:]
    wq3 = jnp.concatenate([uq[..., :QK_NOPE], uq_rope, _rot_half_cols(uq_rope)], axis=-1)
    wq3 = jnp.transpose(wq3, (0, 2, 1, 3)).astype(BF16)
    wkv3 = jnp.concatenate([mla_w_uk.reshape(depth, RKV, H, QK_NOPE),
                            mla_w_uv.reshape(depth, RKV, H, V_DIM)], axis=-1)
    wkv3 = jnp.transpose(wkv3, (0, 2, 1, 3)).astype(BF16)
    wq3t = jnp.swapaxes(wq3, 2, 3)

    causal = jnp.tril(jnp.ones((GM_CHUNK, GM_CHUNK), bool))
    ws_p = jnp.where(causal, gm_w_s, 0.0)
    reps = GM_CHUNK // L
    ws_l = jnp.where(causal[:L, :L], gm_w_s[:, :, :L, :L], 0.0)
    ws_s = jnp.einsum("ab,lgij->lgaibj", jnp.eye(reps, dtype=F32), ws_l).reshape(depth, GM_GROUPS, GM_CHUNK, GM_CHUNK)
    ws2 = jnp.stack([ws_p, ws_s], axis=1).astype(BF16)
    gd = GW // GM_GROUPS
    bs_p = jnp.repeat(jnp.transpose(gm_b_s, (0, 2, 1)), gd, axis=2)
    bs_s = jnp.repeat(jnp.tile(jnp.transpose(gm_b_s[:, :, :L], (0, 2, 1)), (1, reps, 1)), gd, axis=2)
    bs2 = jnp.stack([bs_p, bs_s], axis=1)

    router_pad = jnp.pad(moe_router, ((0, 0), (0, 0), (0, LANES - n_experts)))

    half = QK_ROPE // 2
    inv = ROPE_THETA ** (-jnp.arange(half, dtype=F32) / half)
    pos = jnp.concatenate([jnp.tile(jnp.arange(S), B), jnp.tile(past + jnp.arange(L), NB)]).astype(F32)
    ang = pos[:, None] * inv[None, :]
    cos = jnp.tile(jnp.cos(ang), (1, 2))
    sin = jnp.tile(jnp.sin(ang), (1, 2))
    cos_t, sin_t = cos[:TP].T, sin[:TP].T
    scale_log2e = ATTN_SCALE * math.log2(math.e)

    cache_kr_t = jnp.swapaxes(cache_mla_krope, 2, 3)
    hist_s = jnp.pad(cache_conv, ((0, 0), (0, 0), (CONV_HALO - (CV_KERNEL - 1), 0), (0, 0)))

    x = jnp.concatenate([x_prompt.reshape(TP, D), x_sample.reshape(TS, D)], axis=0)
    xb = x.astype(BF16)
    row2 = lambda a: a.reshape(1, -1)

    tm_dense = _tile(T, 512)
    eid_dense = jnp.zeros((T // tm_dense,), jnp.int32)
    nu_dense = jnp.full((1,), T // tm_dense, jnp.int32)

    outs = {k: [] for k in ("lat_p", "kr_p", "conv_p", "lat_s", "kr_s", "conv_s", "v_s")}
    for l in range(depth):
        wl = in_proj_weights(l)
        uvg = _mm_act(xb, wl["uv"], zero_uv, "gelu", F32)
        glu = _mm_glu(xb, wl["a"], wl["b"])
        cqn, ckv, kr = _latents(xb, wl["q"], wl["kv"], wl["kr2"], row2(mla_q_norm_g[l]),
                                row2(mla_kv_norm_g[l]), cos, sin)

        us, v_ln = _gmlp(uvg, ws2[l], bs2[l], row2(gm_ln_g[l]), row2(gm_ln_b[l]), TP)

        cv_args = (cv_w_dw[l], row2(cv_b_dw[l]), row2(cv_ln_g[l]), row2(cv_ln_b[l]))
        yb_p = _conv(glu, glu, 0, B, S, *cv_args, zero_first=True)
        yb_s = _conv(glu, hist_s[l], TP, NB, L, *cv_args, zero_first=False)

        qt_p = _qproj_t(cqn, wq3t[l], cos_t, sin_t, B, S, scale_log2e)
        q_s = _qproj(cqn, wq3[l], cos, sin, TP, NB, L)
        k_p, vt_p = _kvproj(ckv, kr, wkv3[l], B, S)
        o_p = _attn_prompt(qt_p, k_p, vt_p)
        o_s = _attn_sample(q_s, cache_mla_latent, cache_kr_t, l, ckv, kr, wkv3[l], TP)

        merged = _merge(xb, us, yb_p, yb_s, o_p, o_s, gm_w_p[l].astype(BF16), cv_w_pw[l].astype(BF16),
                        mla_w_o[l].astype(BF16), wl["g"], row2(b_gate[l]))
        x, xb = _outproj_ln(merged, w_out[l].astype(BF16), x, row2(ln1_g[l]), row2(ln1_b[l]), alpha)

        j = l // 2
        if l % 2 == 0:
            x, xb = _ffn(xb, eid_dense, nu_dense, ffn_w1[j][None].astype(BF16), ffn_w3[j][None].astype(BF16),
                         ffn_w2[j][None].astype(BF16), tm_dense, dense=(x, row2(ln2_g[l]), row2(ln2_b[l]), alpha))
        else:
            route = _router(x, router_pad[j], n_experts)
            slot_tok, block_e, n_used, dest_kt = _moe_plan(route, n_experts, MOE_TM)
            xg = _gather_rows(x, slot_tok)
            y = _ffn(xg, block_e, n_used, moe_w1[j].astype(BF16), moe_w3[j].astype(BF16),
                     moe_w2[j].astype(BF16), MOE_TM)
            yg = _gather_rows(y, dest_kt)
            x, xb = _combine(x, yg, route, row2(ln2_g[l]), row2(ln2_b[l]), alpha)

        outs["lat_p"].append(ckv[:TP].reshape(B, S, RKV))
        outs["kr_p"].append(kr[:TP].reshape(B, S, QK_ROPE))
        outs["conv_p"].append(jnp.stack([glu[(s + 1) * S - (CV_KERNEL - 1):(s + 1) * S] for s in range(B)]))
        outs["lat_s"].append(ckv[TP:].reshape(NB, L, RKV))
        outs["kr_s"].append(kr[TP:].reshape(NB, L, QK_ROPE))
        outs["conv_s"].append(glu[TP:].reshape(NB, L, CW)[:, L - (CV_KERNEL - 1):])
        outs["v_s"].append(v_ln[TP:].reshape(NB, L, GW))

    return (x[:TP].reshape(B, S, D), x[TP:].reshape(NB, L, D),
            jnp.stack(outs["lat_p"]), jnp.stack(outs["kr_p"]), jnp.stack(outs["conv_p"]),
            jnp.stack(outs["lat_s"]), jnp.stack(outs["kr_s"]), jnp.stack(outs["conv_s"]),
            jnp.stack(outs["v_s"]))
```

```python
import functools
import math

import jax
import jax.numpy as jnp
from jax import lax
from jax.experimental import pallas as pl
from jax.experimental.pallas import tpu as pltpu

F32 = jnp.float32
BF16 = jnp.bfloat16

CHUNK = 64
GM_GROUPS = 8
GM_CHUNK = 128
CV_KERNEL = 31
N_HEADS = 16
QK_NOPE = 128
QK_ROPE = 64
V_DIM = 128
QK_PAD = 256
V_ONES = 16
ROPE_THETA = 10000.0
TOP_K = 2
LN_EPS = 1e-5
RMS_EPS = 1e-6
ATTN_SCALE = (QK_NOPE + QK_ROPE) ** -0.5

V7X_VMEM_LIMIT_BYTES = 56 * 1024 * 1024
LANES = 128
SUBLANES = 8
CONV_HALO = 32
MOE_TM = 512
FFN_CHUNKS = 2
GATHER_UNROLL = 8


def _cp(*sem):
    return pltpu.CompilerParams(dimension_semantics=sem,
                                vmem_limit_bytes=V7X_VMEM_LIMIT_BYTES)


def _tile(n, pref, mult=8):
    if n <= pref:
        return n
    for t in range(pref, 0, -1):
        if n % t == 0 and t % mult == 0:
            return t
    return n


def _ln(x, g, b):
    mu = jnp.mean(x, axis=-1, keepdims=True)
    xc = x - mu
    var = jnp.mean(xc * xc, axis=-1, keepdims=True)
    return xc * lax.rsqrt(var + LN_EPS) * g + b


def _rms(x, g):
    ms = jnp.mean(x * x, axis=-1, keepdims=True)
    return x * lax.rsqrt(ms + RMS_EPS) * g


def _dot(a, b):
    return jnp.dot(a, b, preferred_element_type=F32)


def _dot_nt(a, b):
    return lax.dot_general(a, b, (((1,), (1,)), ((), ())), preferred_element_type=F32)


def _wt_spec(layer, row0, tn, K, jmap):
    assert row0 % tn == 0
    return pl.BlockSpec((1, tn, K), lambda *ij: (layer, row0 // tn + jmap(*ij), 0))


def _mm_act_kernel(x_ref, wt_ref, b_ref, o_ref, *, act):
    acc = _dot_nt(x_ref[...], wt_ref[0]) + b_ref[...]
    if act == "gelu":
        acc = jax.nn.gelu(acc)
    elif act == "sigmoid":
        acc = jax.nn.sigmoid(acc)
    o_ref[...] = acc.astype(o_ref.dtype)


def _mm_act(x, wt, layer, row0, N, b, act, out_dtype, tm=1024, tn=1024):
    M, K = x.shape
    tm, tn = _tile(M, tm), _tile(N, tn, LANES)
    return pl.pallas_call(
        functools.partial(_mm_act_kernel, act=act),
        grid=(M // tm, N // tn),
        in_specs=[pl.BlockSpec((tm, K), lambda i, j: (i, 0)),
                  _wt_spec(layer, row0, tn, K, lambda i, j: j),
                  pl.BlockSpec((1, tn), lambda i, j: (0, j))],
        out_specs=pl.BlockSpec((tm, tn), lambda i, j: (i, j)),
        out_shape=jax.ShapeDtypeStruct((M, N), out_dtype),
        compiler_params=_cp("parallel", "arbitrary"),
        name="mm_" + act,
    )(x, wt, b)


def _mm_glu_kernel(x_ref, wa_ref, wb_ref, o_ref):
    x = x_ref[...]
    o_ref[...] = _dot_nt(x, wa_ref[0]) * jax.nn.sigmoid(_dot_nt(x, wb_ref[0]))


def _mm_glu(x, wt, layer, row_a, row_b, N, tm=1024, tn=512):
    M, K = x.shape
    tm, tn = _tile(M, tm), _tile(N, tn, LANES)
    return pl.pallas_call(
        _mm_glu_kernel,
        grid=(M // tm, N // tn),
        in_specs=[pl.BlockSpec((tm, K), lambda i, j: (i, 0)),
                  _wt_spec(layer, row_a, tn, K, lambda i, j: j),
                  _wt_spec(layer, row_b, tn, K, lambda i, j: j)],
        out_specs=pl.BlockSpec((tm, tn), lambda i, j: (i, j)),
        out_shape=jax.ShapeDtypeStruct((M, N), F32),
        compiler_params=_cp("parallel", "arbitrary"),
        name="mm_glu",
    )(x, wt, wt)


def _latent_kernel(x_ref, wq_ref, wkv_ref, wkr_ref, gq_ref, gkv_ref, cos_ref, sin_ref,
                   cq_ref, ckv_ref, kr_ref):
    x = x_ref[...]
    cq_ref[...] = _rms(_dot_nt(x, wq_ref[0]), gq_ref[...]).astype(cq_ref.dtype)
    ckv_ref[...] = _rms(_dot_nt(x, wkv_ref[0]), gkv_ref[...])
    r = _dot_nt(x, wkr_ref[0])
    kr_ref[...] = r[:, :QK_ROPE] * cos_ref[...] + r[:, QK_ROPE:] * sin_ref[...]


def _latents(x, wt, wkr2t, layer, row_q, row_kv, gq, gkv, cos, sin, tm=512):
    M, K = x.shape
    Rq, Rkv = gq.shape[1], gkv.shape[1]
    tm = _tile(M, tm)
    row = lambda i: (i, 0)
    full = lambda i: (0, 0)
    return pl.pallas_call(
        _latent_kernel,
        grid=(M // tm,),
        in_specs=[pl.BlockSpec((tm, K), row),
                  _wt_spec(layer, row_q, Rq, K, lambda i: 0), _wt_spec(layer, row_kv, Rkv, K, lambda i: 0),
                  pl.BlockSpec((1, 2 * QK_ROPE, K), lambda i: (layer, 0, 0)),
                  pl.BlockSpec((1, Rq), full), pl.BlockSpec((1, Rkv), full),
                  pl.BlockSpec((tm, QK_ROPE), row), pl.BlockSpec((tm, QK_ROPE), row)],
        out_specs=[pl.BlockSpec((tm, Rq), row), pl.BlockSpec((tm, Rkv), row),
                   pl.BlockSpec((tm, QK_ROPE), row)],
        out_shape=[jax.ShapeDtypeStruct((M, Rq), BF16), jax.ShapeDtypeStruct((M, Rkv), F32),
                   jax.ShapeDtypeStruct((M, QK_ROPE), F32)],
        compiler_params=_cp("parallel"),
        name="latents",
    )(x, wt, wt, wkr2t, gq, gkv, cos, sin)


def _gmlp_kernel(uv_ref, ws_ref, bs_ref, g_ref, b_ref, us_ref, v_ref, *, n_chunks, groups):
    W = v_ref.shape[1]
    gd = W // groups
    vn = _ln(uv_ref[:, W:], g_ref[...], b_ref[...])
    v_ref[...] = vn
    vb = vn.astype(BF16)
    for c in range(n_chunks):
        r0 = c * GM_CHUNK
        for g in range(groups):
            c0 = g * gd
            s = _dot(ws_ref[0, g], vb[r0:r0 + GM_CHUNK, c0:c0 + gd]) + bs_ref[0, :, c0:c0 + gd]
            u = uv_ref[r0:r0 + GM_CHUNK, c0:c0 + gd]
            us_ref[r0:r0 + GM_CHUNK, c0:c0 + gd] = (u * s).astype(us_ref.dtype)


def _gmlp(uvg, ws2, bs2, g, b, n_prompt_rows, tm=512):
    M, W2 = uvg.shape
    W = W2 // 2
    tm = _tile(math.gcd(n_prompt_rows, M - n_prompt_rows), tm, GM_CHUNK)
    npt = n_prompt_rows // tm
    sel = lambda i: (jnp.minimum(i // npt, 1), 0, 0, 0)
    sel3 = lambda i: (jnp.minimum(i // npt, 1), 0, 0)
    row = lambda i: (i, 0)
    full = lambda i: (0, 0)
    return pl.pallas_call(
        functools.partial(_gmlp_kernel, n_chunks=tm // GM_CHUNK, groups=GM_GROUPS),
        grid=(M // tm,),
        in_specs=[pl.BlockSpec((tm, W2), row),
                  pl.BlockSpec((1, GM_GROUPS, GM_CHUNK, GM_CHUNK), sel),
                  pl.BlockSpec((1, GM_CHUNK, W), sel3),
                  pl.BlockSpec((1, W), full), pl.BlockSpec((1, W), full)],
        out_specs=[pl.BlockSpec((tm, W), row), pl.BlockSpec((tm, W), row)],
        out_shape=[jax.ShapeDtypeStruct((M, W), BF16), jax.ShapeDtypeStruct((M, W), F32)],
        compiler_params=_cp("parallel"),
        name="gmlp",
    )(uvg, ws2, bs2, g, b)


def _conv_kernel(cur_ref, halo_ref, w_ref, bdw_ref, g_ref, b_ref, o_ref, full_ref, sh_ref, acc_ref,
                 *, tm, zero_first):
    C = cur_ref.shape[1]
    halo = halo_ref[...].reshape(CONV_HALO, C)
    if zero_first:
        halo = jnp.where(pl.program_id(1) == 0, 0.0, halo)
    full_ref[0:CONV_HALO, :] = halo
    full_ref[CONV_HALO:CONV_HALO + tm, :] = cur_ref[...]
    off = CONV_HALO - (CV_KERNEL - 1)
    ns = sh_ref.shape[1]

    def lane_block(c, carry):
        c0 = pl.multiple_of(c * LANES, LANES)
        sh_ref[0] = full_ref[:, pl.ds(c0, LANES)]
        for r in range(1, SUBLANES):
            sh_ref[r, :ns - SUBLANES, :] = full_ref[pl.ds(r, ns - SUBLANES), pl.ds(c0, LANES)]
        wk = w_ref[:, pl.ds(c0, LANES)]
        wrows = [jnp.broadcast_to(wk[k:k + 1, :], (SUBLANES, LANES)) for k in range(CV_KERNEL)]
        for rb in range(tm // SUBLANES):
            acc = jnp.zeros((SUBLANES, LANES), F32)
            for k in range(CV_KERNEL):
                a, r = divmod(off + k, SUBLANES)
                acc = acc + sh_ref[r, pl.ds((rb + a) * SUBLANES, SUBLANES), :] * wrows[k]
            acc_ref[pl.ds(rb * SUBLANES, SUBLANES), pl.ds(c0, LANES)] = acc
        return carry

    lax.fori_loop(0, C // LANES, lane_block, 0)
    y = _ln(acc_ref[...] + bdw_ref[...], g_ref[...], b_ref[...])
    o_ref[...] = (y * jax.nn.sigmoid(y)).astype(o_ref.dtype)


def _conv(glu, halo_src, row0, n_seq, L, w_dw, b_dw, g, b, zero_first, tm=256):
    C = glu.shape[1]
    tm = _tile(L, tm, CONV_HALO)
    lt = L // tm
    hb = tm // CONV_HALO
    assert row0 % tm == 0
    r0 = row0 // tm
    if zero_first:
        halo_spec = pl.BlockSpec((CONV_HALO, C), lambda s, i: (jnp.maximum((r0 + s * lt + i) * hb - 1, 0), 0))
    else:
        halo_spec = pl.BlockSpec((1, CONV_HALO, C), lambda s, i: (s, 0, 0))
    full = lambda s, i: (0, 0)
    return pl.pallas_call(
        functools.partial(_conv_kernel, tm=tm, zero_first=zero_first),
        grid=(n_seq, lt),
        in_specs=[pl.BlockSpec((tm, C), lambda s, i: (r0 + s * lt + i, 0)),
                  halo_spec,
                  pl.BlockSpec((CV_KERNEL, C), full),
                  pl.BlockSpec((1, C), full), pl.BlockSpec((1, C), full), pl.BlockSpec((1, C), full)],
        out_specs=pl.BlockSpec((tm, C), lambda s, i: (s * lt + i, 0)),
        out_shape=jax.ShapeDtypeStruct((n_seq * L, C), BF16),
        scratch_shapes=[pltpu.VMEM((CONV_HALO + tm, C), F32),
                        pltpu.VMEM((SUBLANES, CONV_HALO + tm, LANES), F32),
                        pltpu.VMEM((tm, C), F32)],
        compiler_params=_cp("parallel", "arbitrary"),
        name="conv_prompt" if zero_first else "conv_sample",
    )(glu, halo_src, w_dw, b_dw, g, b)


def _qproj_kernel(cq_ref, w_ref, cos_ref, sin_ref, q_ref, *, hg, nb, L):
    cq = cq_ref[...]
    cos, sin = cos_ref[...], sin_ref[...]
    for h in range(hg):
        r = _dot(cq, w_ref[h])
        nope = r[:, :QK_NOPE] * ATTN_SCALE
        rp = (r[:, QK_NOPE:QK_NOPE + QK_ROPE] * cos + r[:, QK_NOPE + QK_ROPE:] * sin) * ATTN_SCALE
        if nb == 1:
            q_ref[0, h, :, :QK_NOPE] = nope.astype(q_ref.dtype)
            q_ref[0, h, :, QK_NOPE:] = rp.astype(q_ref.dtype)
        else:
            q_ref[:, h, :, :QK_NOPE] = nope.reshape(nb, L, QK_NOPE).astype(q_ref.dtype)
            q_ref[:, h, :, QK_NOPE:] = rp.reshape(nb, L, QK_ROPE).astype(q_ref.dtype)


def _qproj(cqn, wq3, cos, sin, row0, n_seq, L, hg=4, tm=512):
    R = cqn.shape[1]
    H = wq3.shape[0]
    hg = _tile(H, hg, 1)
    dq = QK_NOPE + QK_ROPE
    if L >= tm:
        tm = _tile(L, tm)
        nb, lt = 1, L // tm
        out_spec = pl.BlockSpec((1, hg, tm, dq), lambda i, h: (i // lt, h, i % lt, 0))
    else:
        nb = _tile(n_seq, max(tm // L, 1), 1)
        tm = nb * L
        out_spec = pl.BlockSpec((nb, hg, L, dq), lambda i, h: (i, h, 0, 0))
    assert row0 % tm == 0
    r0 = row0 // tm
    row = lambda i, h: (r0 + i, 0)
    return pl.pallas_call(
        functools.partial(_qproj_kernel, hg=hg, nb=nb, L=L),
        grid=(n_seq * L // tm, H // hg),
        in_specs=[pl.BlockSpec((tm, R), row),
                  pl.BlockSpec((hg, R, wq3.shape[2]), lambda i, h: (h, 0, 0)),
                  pl.BlockSpec((tm, QK_ROPE), row), pl.BlockSpec((tm, QK_ROPE), row)],
        out_specs=out_spec,
        out_shape=jax.ShapeDtypeStruct((n_seq, H, L, dq), BF16),
        compiler_params=_cp("parallel", "arbitrary"),
        name="qproj",
    )(cqn, wq3, cos, sin)


def _qproj_t_kernel(cq_ref, w_ref, cos_ref, sin_ref, q_ref, *, hg, scale):
    cq = cq_ref[...]
    cos, sin = cos_ref[...], sin_ref[...]
    for h in range(hg):
        r = _dot_nt(w_ref[h], cq)
        rp = r[QK_NOPE:QK_NOPE + QK_ROPE] * cos + r[QK_NOPE + QK_ROPE:] * sin
        q_ref[0, h, :QK_NOPE, :] = (r[:QK_NOPE] * scale).astype(q_ref.dtype)
        q_ref[0, h, QK_NOPE:QK_NOPE + QK_ROPE, :] = (rp * scale).astype(q_ref.dtype)
        q_ref[0, h, QK_NOPE + QK_ROPE:, :] = jnp.zeros((QK_PAD - QK_NOPE - QK_ROPE, cq.shape[0]), q_ref.dtype)


def _qproj_t(cqn, wq3t, cos_t, sin_t, n_seq, L, scale, hg=4, tm=512):
    R = cqn.shape[1]
    H = wq3t.shape[0]
    hg = _tile(H, hg, 1)
    tm = _tile(L, tm, LANES)
    lt = L // tm
    dq = QK_PAD
    return pl.pallas_call(
        functools.partial(_qproj_t_kernel, hg=hg, scale=scale),
        grid=(n_seq * lt, H // hg),
        in_specs=[pl.BlockSpec((tm, R), lambda i, h: (i, 0)),
                  pl.BlockSpec((hg, wq3t.shape[1], R), lambda i, h: (h, 0, 0)),
                  pl.BlockSpec((QK_ROPE, tm), lambda i, h: (0, i)),
                  pl.BlockSpec((QK_ROPE, tm), lambda i, h: (0, i))],
        out_specs=pl.BlockSpec((1, hg, dq, tm), lambda i, h: (i // lt, h, 0, i % lt)),
        out_shape=jax.ShapeDtypeStruct((n_seq, H, dq, L), BF16),
        compiler_params=_cp("parallel", "arbitrary"),
        name="qproj_t",
    )(cqn, wq3t, cos_t, sin_t)


def _kvproj_kernel(ckv_ref, kr_ref, w_ref, k_ref, vt_ref, *, hg):
    ckv = ckv_ref[...].astype(BF16)
    kr = kr_ref[...].astype(BF16)
    for h in range(hg):
        r = _dot(ckv, w_ref[h])
        k_ref[0, h, :, :QK_NOPE] = r[:, :QK_NOPE].astype(k_ref.dtype)
        k_ref[0, h, :, QK_NOPE:QK_NOPE + QK_ROPE] = kr
        k_ref[0, h, :, QK_NOPE + QK_ROPE:] = jnp.zeros((kr.shape[0], QK_PAD - QK_NOPE - QK_ROPE), k_ref.dtype)
        vt_ref[0, h, :V_DIM, :] = r[:, QK_NOPE:].T.astype(vt_ref.dtype)
        vt_ref[0, h, V_DIM:, :] = jnp.ones((V_ONES, ckv.shape[0]), vt_ref.dtype)


def _kvproj(ckv, kr, wkv3, n_seq, L, hg=4, tm=512):
    R = ckv.shape[1]
    H = wkv3.shape[0]
    hg = _tile(H, hg, 1)
    tm = _tile(L, tm, LANES)
    lt = L // tm
    dq = QK_PAD
    row = lambda i, h: (i, 0)
    wmap = lambda i, h: (h, 0, 0)
    return pl.pallas_call(
        functools.partial(_kvproj_kernel, hg=hg),
        grid=(n_seq * lt, H // hg),
        in_specs=[pl.BlockSpec((tm, R), row), pl.BlockSpec((tm, QK_ROPE), row),
                  pl.BlockSpec((hg, R, wkv3.shape[2]), wmap)],
        out_specs=[pl.BlockSpec((1, hg, tm, dq), lambda i, h: (i // lt, h, i % lt, 0)),
                   pl.BlockSpec((1, hg, V_DIM + V_ONES, tm), lambda i, h: (i // lt, h, 0, i % lt))],
        out_shape=[jax.ShapeDtypeStruct((n_seq, H, L, dq), BF16),
                   jax.ShapeDtypeStruct((n_seq, H, V_DIM + V_ONES, L), BF16)],
        compiler_params=_cp("parallel", "arbitrary"),
        name="kvproj",
    )(ckv, kr, wkv3)


def _attn_prompt_kernel(qt_ref, k_ref, vt_ref, o_ref, s_ref, p_ref, acc_ref, *, tq, tk, hg):
    qi = pl.program_id(2)
    d0 = pl.multiple_of(qi * tq, tq)

    def scores(g, j0, slot):
        s = _dot(k_ref[0, g, pl.ds(j0, tk), :], qt_ref[0, g])
        s_ref[slot, g] = s
        return jnp.max(s, axis=0, keepdims=True)

    def softmax(g, slot, mx, m, mask=None):
        s = s_ref[slot, g]
        if mask is not None:
            s = jnp.where(mask, s, -jnp.inf)
            mx = jnp.max(s, axis=0, keepdims=True)
        m_new = jnp.maximum(m, mx)
        p_ref[slot, g] = jnp.exp2(s - m_new).astype(BF16)
        return jnp.exp2(m - m_new), m_new

    def accumulate(g, j0, slot, alpha):
        acc_ref[g] = alpha * acc_ref[g] + _dot(vt_ref[0, g, :, pl.ds(j0, tk)], p_ref[slot, g])

    def step(j0, slot, state, mask=None, last=False):
        jp = pl.multiple_of(jnp.maximum(j0 - tk, 0), tk)
        new = []
        for g in range(hg):
            mx, a_prev, m = state[g]
            mx_next = mx if last else scores(g, pl.multiple_of(j0 + tk, tk), 1 - slot)
            accumulate(g, jp, 1 - slot, a_prev)
            alpha, m = softmax(g, slot, mx, m, mask)
            new.append((mx_next, alpha, m))
        return tuple(new)

    def body(i, state):
        j0 = pl.multiple_of(2 * i * tk, tk)
        state = step(j0, 0, state)
        return step(pl.multiple_of(j0 + tk, tk), 1, state)

    p_ref[1] = jnp.zeros_like(p_ref[1])
    acc_ref[...] = jnp.zeros_like(acc_ref)
    init = tuple((scores(g, 0, 0), jnp.ones((1, tq), F32), jnp.full((1, tq), -jnp.inf, F32))
                 for g in range(hg))
    state = lax.fori_loop(0, qi * (tq // (2 * tk)), body, init)

    key_chunk = lax.broadcasted_iota(jnp.int32, (tk, tq), 0) // CHUNK
    qry_chunk = lax.broadcasted_iota(jnp.int32, (tk, tq), 1) // CHUNK
    state = step(d0, 0, state, mask=key_chunk <= qry_chunk)
    state = step(pl.multiple_of(d0 + tk, tk), 1, state, mask=key_chunk + tk // CHUNK <= qry_chunk, last=True)
    for g in range(hg):
        accumulate(g, pl.multiple_of(d0 + tk, tk), 1, state[g][1])
        o = acc_ref[g, :V_DIM, :] / acc_ref[g, V_DIM:V_DIM + 1, :]
        o_ref[:, g * V_DIM:(g + 1) * V_DIM] = o.T.astype(o_ref.dtype)


def _attn_prompt(qt, k, vt, tq=512, hg=2):
    B, H, dq, S = qt.shape
    tq = _tile(S, tq, 2 * LANES)
    tk = tq // 2
    hg = _tile(H, hg, 1)
    return pl.pallas_call(
        functools.partial(_attn_prompt_kernel, tq=tq, tk=tk, hg=hg),
        grid=(B, H // hg, S // tq),
        in_specs=[pl.BlockSpec((1, hg, dq, tq), lambda b, h, i: (b, h, 0, i)),
                  pl.BlockSpec((1, hg, S, dq), lambda b, h, i: (b, h, 0, 0)),
                  pl.BlockSpec((1, hg, vt.shape[2], S), lambda b, h, i: (b, h, 0, 0))],
        out_specs=pl.BlockSpec((tq, hg * V_DIM), lambda b, h, i: (b * (S // tq) + i, h)),
        out_shape=jax.ShapeDtypeStruct((B * S, H * V_DIM), BF16),
        scratch_shapes=[pltpu.VMEM((2, hg, tk, tq), F32), pltpu.VMEM((2, hg, tk, tq), BF16),
                        pltpu.VMEM((hg, vt.shape[2], tq), F32)],
        compiler_params=_cp("parallel", "parallel", "arbitrary"),
        name="attn_prompt",
    )(qt, k, vt)


def _attn_sample_kernel(q_ref, plat_ref, pkrt_ref, nlat_ref, nkr_ref, w_ref, o_ref, ql_ref, qr_ref,
                        *, H, L, kc):
    P = plat_ref.shape[2]
    for h in range(H):
        qh = q_ref[0, h]
        ql_ref[h * L:(h + 1) * L, :] = _dot_nt(qh[:, :QK_NOPE], w_ref[h, :, :QK_NOPE]).astype(BF16)
        qr_ref[h * L:(h + 1) * L, :] = qh[:, QK_NOPE:]
    ql = ql_ref[...]
    qr = qr_ref[...]

    def step(lat, s_rope, carry):
        m, l, acc = carry
        s = _dot_nt(ql, lat) + s_rope
        m_new = jnp.maximum(m, jnp.max(s, axis=-1, keepdims=True))
        alpha = jnp.exp(m - m_new)
        p = jnp.exp(s - m_new)
        l = alpha * l + jnp.sum(p, axis=-1, keepdims=True)
        acc = alpha * acc + _dot(p.astype(BF16), lat)
        return m_new, l, acc

    R = ql.shape[1]
    carry = (jnp.full((H * L, 1), -jnp.inf, F32), jnp.zeros((H * L, 1), F32), jnp.zeros((H * L, R), F32))
    for c in range(P // kc):
        krt = pkrt_ref[0, 0, :, c * kc:(c + 1) * kc].astype(BF16)
        carry = step(plat_ref[0, 0, c * kc:(c + 1) * kc, :].astype(BF16), _dot(qr, krt), carry)
    m, l, acc = step(nlat_ref[...].astype(BF16), _dot_nt(qr, nkr_ref[...].astype(BF16)), carry)
    ol = (acc / l).astype(BF16)
    for h in range(H):
        o_ref[:, h * V_DIM:(h + 1) * V_DIM] = _dot(ol[h * L:(h + 1) * L, :], w_ref[h, :, QK_NOPE:]).astype(o_ref.dtype)


def _attn_sample(q, cache_lat, cache_kr_t, layer, ckv, kr, wkv3, row0):
    n_req, H, L, dq = q.shape
    P, R = cache_lat.shape[2], cache_lat.shape[3]
    kc = _tile(P, 1024)
    assert row0 % L == 0
    r0 = row0 // L
    return pl.pallas_call(
        functools.partial(_attn_sample_kernel, H=H, L=L, kc=kc),
        grid=(n_req,),
        in_specs=[pl.BlockSpec((1, H, L, dq), lambda b: (b, 0, 0, 0)),
                  pl.BlockSpec((1, 1, P, R), lambda b: (layer, b, 0, 0)),
                  pl.BlockSpec((1, 1, QK_ROPE, P), lambda b: (layer, b, 0, 0)),
                  pl.BlockSpec((L, R), lambda b: (r0 + b, 0)),
                  pl.BlockSpec((L, QK_ROPE), lambda b: (r0 + b, 0)),
                  pl.BlockSpec(wkv3.shape, lambda b: (0, 0, 0))],
        out_specs=pl.BlockSpec((L, H * V_DIM), lambda b: (b, 0)),
        out_shape=jax.ShapeDtypeStruct((n_req * L, H * V_DIM), BF16),
        scratch_shapes=[pltpu.VMEM((H * L, R), BF16), pltpu.VMEM((H * L, QK_ROPE), BF16)],
        compiler_params=_cp("parallel"),
        name="attn_sample",
    )(q, cache_lat, cache_kr_t, ckv, kr, wkv3)


def _merge_kernel(x_ref, us_ref, ybp_ref, ybs_ref, op_ref, os_ref, wp_ref, wpw_ref, wo_ref,
                  wga_ref, wgb_ref, wgc_ref, ba_ref, bb_ref, bc_ref, out_ref, *, npt):
    prompt = pl.program_id(0) < npt
    x = x_ref[...]
    yb_in = jnp.where(prompt, ybp_ref[...], ybs_ref[...])
    o_in = jnp.where(prompt, op_ref[...], os_ref[...])
    out = jax.nn.sigmoid(_dot_nt(x, wga_ref[0]) + ba_ref[...]) * _dot(us_ref[...], wp_ref[0])
    out += jax.nn.sigmoid(_dot_nt(x, wgb_ref[0]) + bb_ref[...]) * _dot(yb_in, wpw_ref[0])
    out += jax.nn.sigmoid(_dot_nt(x, wgc_ref[0]) + bc_ref[...]) * _dot(o_in, wo_ref[0])
    out_ref[...] = out.astype(out_ref.dtype)


def _merge(xb, us, yb_p, yb_s, o_p, o_s, wp, wpw, wo, wgt, bg, layer, tm=512, tn=512):
    M = us.shape[0]
    D = wp.shape[2]
    TP, TS = yb_p.shape[0], yb_s.shape[0]
    tm, tn = _tile(math.gcd(TP, TS), tm), _tile(D, tn, LANES)
    npt = TP // tm
    nj = D // tn
    row = lambda i, j: (i, 0)
    prow = lambda i, j: (jnp.minimum(i, npt - 1), 0)
    srow = lambda i, j: (jnp.maximum(i - npt, 0), 0)
    col = lambda i, j: (layer, 0, j)
    gcol = lambda g: (lambda i, j: (0, g * nj + j))
    grow = lambda g: _wt_spec(layer, g * D, tn, wgt.shape[2], lambda i, j: j)
    return pl.pallas_call(
        functools.partial(_merge_kernel, npt=npt),
        grid=(M // tm, nj),
        in_specs=[pl.BlockSpec((tm, xb.shape[1]), row), pl.BlockSpec((tm, us.shape[1]), row),
                  pl.BlockSpec((tm, yb_p.shape[1]), prow), pl.BlockSpec((tm, yb_s.shape[1]), srow),
                  pl.BlockSpec((tm, o_p.shape[1]), prow), pl.BlockSpec((tm, o_s.shape[1]), srow),
                  pl.BlockSpec((1, wp.shape[1], tn), col), pl.BlockSpec((1, wpw.shape[1], tn), col),
                  pl.BlockSpec((1, wo.shape[1], tn), col),
                  grow(0), grow(1), grow(2),
                  pl.BlockSpec((1, tn), gcol(0)), pl.BlockSpec((1, tn), gcol(1)), pl.BlockSpec((1, tn), gcol(2))],
        out_specs=pl.BlockSpec((tm, tn), lambda i, j: (i, j)),
        out_shape=jax.ShapeDtypeStruct((M, D), BF16),
        compiler_params=_cp("parallel", "arbitrary"),
        name="merge",
    )(xb, us, yb_p, yb_s, o_p, o_s, wp, wpw, wo, wgt, wgt, wgt, bg, bg, bg)


def _outproj_kernel(m_ref, w_ref, x_ref, g_ref, b_ref, o_ref, ob_ref, *, alpha):
    y = _ln(alpha * x_ref[...] + _dot(m_ref[...], w_ref[0]), g_ref[...], b_ref[...])
    o_ref[...] = y
    ob_ref[...] = y.astype(ob_ref.dtype)


def _outproj_ln(merged, w, layer, x, g, b, alpha, tm=512):
    M, D = x.shape
    tm = _tile(M, tm)
    row = lambda i: (i, 0)
    full = lambda i: (0, 0)
    return pl.pallas_call(
        functools.partial(_outproj_kernel, alpha=alpha),
        grid=(M // tm,),
        in_specs=[pl.BlockSpec((tm, merged.shape[1]), row),
                  pl.BlockSpec((1,) + w.shape[1:], lambda i: (layer, 0, 0)),
                  pl.BlockSpec((tm, D), row), pl.BlockSpec((1, D), full), pl.BlockSpec((1, D), full)],
        out_specs=[pl.BlockSpec((tm, D), row), pl.BlockSpec((tm, D), row)],
        out_shape=[jax.ShapeDtypeStruct((M, D), F32), jax.ShapeDtypeStruct((M, D), BF16)],
        compiler_params=_cp("parallel"),
        name="outproj_ln",
    )(merged, w, x, g, b)


def _swiglu_partial(x, w1_ref, w3_ref, w2_ref):
    c = w1_ref.shape[3] // FFN_CHUNKS
    hs = []
    for k in range(FFN_CHUNKS):
        cols = slice(k * c, (k + 1) * c)
        h = jax.nn.silu(_dot(x, w1_ref[0, 0, :, cols])) * _dot(x, w3_ref[0, 0, :, cols])
        hs.append(h.astype(BF16))
    out = _dot(hs[0], w2_ref[0, 0, 0:c, :])
    for k in range(1, FFN_CHUNKS):
        out += _dot(hs[k], w2_ref[0, 0, k * c:(k + 1) * c, :])
    return out


def _ffn_dense_kernel(eid_ref, nu_ref, x_ref, w1_ref, w3_ref, w2_ref, r_ref, g_ref, b_ref,
                      o_ref, ob_ref, acc_ref, *, alpha):
    f = pl.program_id(1)

    @pl.when(f == 0)
    def _():
        acc_ref[...] = jnp.zeros_like(acc_ref)

    acc_ref[...] += _swiglu_partial(x_ref[...], w1_ref, w3_ref, w2_ref)

    @pl.when(f == pl.num_programs(1) - 1)
    def _():
        y = _ln(alpha * r_ref[...] + acc_ref[...], g_ref[...], b_ref[...])
        o_ref[...] = y
        ob_ref[...] = y.astype(ob_ref.dtype)


def _ffn_moe_kernel(eid_ref, nu_ref, x_ref, w1_ref, w3_ref, w2_ref, o_ref, xb_ref):
    i = pl.program_id(0)
    f = pl.program_id(1)

    @pl.when(f == 0)
    def _():
        o_ref[...] = jnp.zeros_like(o_ref)
        xb_ref[...] = x_ref[...].astype(BF16)

    @pl.when(i < nu_ref[0])
    def _():
        o_ref[...] += _swiglu_partial(xb_ref[...], w1_ref, w3_ref, w2_ref)


def _ffn(x, eid, n_used, w1, w3, w2, layer, tm, tf=512, dense=None):
    M, D = x.shape
    Fdim = w1.shape[3]
    tf = _tile(Fdim, tf, LANES)
    last = lambda i, nu: jnp.minimum(i, nu[0] - 1)
    xmap = lambda i, f, e, nu: (last(i, nu), 0)
    w13 = lambda i, f, e, nu: (layer, e[last(i, nu)], 0, jnp.where(i < nu[0], f, 0))
    w2m = lambda i, f, e, nu: (layer, e[last(i, nu)], jnp.where(i < nu[0], f, 0), 0)
    row = lambda i, f, e, nu: (i, 0)
    full = lambda i, f, e, nu: (0, 0)
    in_specs = [pl.BlockSpec((tm, D), xmap),
                pl.BlockSpec((1, 1, D, tf), w13), pl.BlockSpec((1, 1, D, tf), w13),
                pl.BlockSpec((1, 1, tf, D), w2m)]
    if dense is not None:
        resid, g, b, alpha = dense
        kern = functools.partial(_ffn_dense_kernel, alpha=alpha)
        in_specs += [pl.BlockSpec((tm, D), row), pl.BlockSpec((1, D), full), pl.BlockSpec((1, D), full)]
        out_specs = [pl.BlockSpec((tm, D), row), pl.BlockSpec((tm, D), row)]
        out_shape = [jax.ShapeDtypeStruct((M, D), F32), jax.ShapeDtypeStruct((M, D), BF16)]
        scratch = [pltpu.VMEM((tm, D), F32)]
        args = (x, w1, w3, w2, resid, g, b)
    else:
        kern = _ffn_moe_kernel
        out_specs = pl.BlockSpec((tm, D), row)
        out_shape = jax.ShapeDtypeStruct((M, D), F32)
        scratch = [pltpu.VMEM((tm, D), BF16)]
        args = (x, w1, w3, w2)
    return pl.pallas_call(
        kern,
        grid_spec=pltpu.PrefetchScalarGridSpec(
            num_scalar_prefetch=2, grid=(M // tm, Fdim // tf),
            in_specs=in_specs, out_specs=out_specs, scratch_shapes=scratch),
        out_shape=out_shape,
        compiler_params=_cp("arbitrary", "arbitrary"),
        name="ffn_dense" if dense is not None else "ffn_moe",
    )(eid, n_used, *args)


def _router_kernel(x_ref, r_ref, o_ref, *, n_experts):
    logits = jnp.dot(x_ref[...], r_ref[...], preferred_element_type=F32,
                     precision=lax.Precision.HIGHEST)
    col = lax.broadcasted_iota(jnp.int32, logits.shape, 1)
    lg = jnp.where(col < n_experts, logits, -jnp.inf)
    m1 = jnp.max(lg, axis=-1, keepdims=True)
    i1 = jnp.min(jnp.where(lg == m1, col, LANES), axis=-1, keepdims=True)
    lg2 = jnp.where(col == i1, -jnp.inf, lg)
    m2 = jnp.max(lg2, axis=-1, keepdims=True)
    i2 = jnp.min(jnp.where(lg2 == m2, col, LANES), axis=-1, keepdims=True)
    e = jnp.exp(m2 - m1)
    g1 = 1.0 / (1.0 + e)
    g2 = e * g1
    out = jnp.where(col == 0, i1.astype(F32),
                    jnp.where(col == 1, i2.astype(F32),
                              jnp.where(col == 2, g1, jnp.where(col == 3, g2, 0.0))))
    o_ref[...] = out


def _router(x, router_pad, n_experts, tm=512):
    M, D = x.shape
    tm = _tile(M, tm)
    return pl.pallas_call(
        functools.partial(_router_kernel, n_experts=n_experts),
        grid=(M // tm,),
        in_specs=[pl.BlockSpec((tm, D), lambda i: (i, 0)), pl.BlockSpec((D, LANES), lambda i: (0, 0))],
        out_specs=pl.BlockSpec((tm, LANES), lambda i: (i, 0)),
        out_shape=jax.ShapeDtypeStruct((M, LANES), F32),
        compiler_params=_cp("parallel"),
        name="router",
    )(x, router_pad)


def _gather_kernel(idx_ref, src_ref, o_ref, sem, *, tm):
    def row_copy(r, t):
        return pltpu.make_async_copy(src_ref.at[pl.ds(t, 1), :], o_ref.at[pl.ds(r, 1), :], sem)

    def start(c, carry):
        for u in range(GATHER_UNROLL):
            r = c * GATHER_UNROLL + u
            row_copy(r, idx_ref[0, 0, r]).start(priority=u % 2)
        return carry

    def wait(c, carry):
        for u in range(GATHER_UNROLL):
            row_copy(c * GATHER_UNROLL + u, 0).wait()
        return carry

    lax.fori_loop(0, tm // GATHER_UNROLL, start, 0)
    lax.fori_loop(0, tm // GATHER_UNROLL, wait, 0)


def _gather_rows(src, idx, tm=256):
    M = idx.shape[0]
    D = src.shape[1]
    tm = _tile(M, tm)
    idx3 = idx.reshape(M // tm, 1, tm)
    return pl.pallas_call(
        functools.partial(_gather_kernel, tm=tm),
        grid=(M // tm,),
        in_specs=[pl.BlockSpec((1, 1, tm), lambda i: (i, 0, 0), memory_space=pltpu.SMEM),
                  pl.BlockSpec(memory_space=pl.ANY)],
        out_specs=pl.BlockSpec((tm, D), lambda i: (i, 0)),
        out_shape=jax.ShapeDtypeStruct((M, D), src.dtype),
        scratch_shapes=[pltpu.SemaphoreType.DMA(())],
        compiler_params=_cp("arbitrary"),
        name="gather_rows",
    )(idx3, src)


def _combine_kernel(x_ref, y1_ref, y2_ref, r_ref, g_ref, b_ref, o_ref, ob_ref, *, alpha):
    r = r_ref[...]
    y = alpha * x_ref[...] + r[:, 2:3] * y1_ref[...] + r[:, 3:4] * y2_ref[...]
    y = _ln(y, g_ref[...], b_ref[...])
    o_ref[...] = y
    ob_ref[...] = y.astype(ob_ref.dtype)


def _combine(x, yg, route, g, b, alpha, tm=512):
    M, D = x.shape
    tm = _tile(M, tm)
    nt = M // tm
    row = lambda i: (i, 0)
    full = lambda i: (0, 0)
    return pl.pallas_call(
        functools.partial(_combine_kernel, alpha=alpha),
        grid=(nt,),
        in_specs=[pl.BlockSpec((tm, D), row), pl.BlockSpec((tm, D), row),
                  pl.BlockSpec((tm, D), lambda i: (nt + i, 0)),
                  pl.BlockSpec((tm, LANES), row), pl.BlockSpec((1, D), full), pl.BlockSpec((1, D), full)],
        out_specs=[pl.BlockSpec((tm, D), row), pl.BlockSpec((tm, D), row)],
        out_shape=[jax.ShapeDtypeStruct((M, D), F32), jax.ShapeDtypeStruct((M, D), BF16)],
        compiler_params=_cp("parallel"),
        name="combine",
    )(x, yg, yg, route, g, b)


def _moe_plan(route, n_experts, tm):
    T = route.shape[0]
    n_assign = T * TOP_K
    e_flat = route[:, :TOP_K].astype(jnp.int32).reshape(-1)
    onehot = (e_flat[:, None] == jnp.arange(n_experts, dtype=jnp.int32)[None, :]).astype(jnp.int32)
    csum = jnp.cumsum(onehot, axis=0)
    rank = jnp.take_along_axis(csum, e_flat[:, None], axis=1)[:, 0] - 1
    counts = csum[-1]
    padded = ((counts + tm - 1) // tm) * tm
    pad_end = jnp.cumsum(padded)
    pad_start = pad_end - padded
    dest = (pad_start[e_flat] + rank).astype(jnp.int32)
    n_blocks = -(-n_assign // tm) + n_experts
    n_slots = n_blocks * tm
    slot_tok = jnp.zeros((n_slots,), jnp.int32).at[dest].set(jnp.arange(n_assign, dtype=jnp.int32) // TOP_K)
    block_e = jnp.minimum(jnp.searchsorted(pad_end, jnp.arange(n_blocks, dtype=jnp.int32) * tm, side="right"),
                          n_experts - 1).astype(jnp.int32)
    n_used = (pad_end[-1:] // tm).astype(jnp.int32)
    dest_kt = dest.reshape(T, TOP_K).T.reshape(-1)
    return slot_tok, block_e, n_used, dest_kt


def _rot_half_cols(w):
    half = w.shape[-1] // 2
    return jnp.concatenate([-w[..., half:], w[..., :half]], axis=-1)


def kernel(x_prompt, x_sample, cache_mla_latent, cache_mla_krope, cache_conv, w_in, b_gate, gm_ln_g, gm_ln_b, gm_w_s, gm_b_s, gm_w_p, cv_w_dw, cv_b_dw, cv_ln_g, cv_ln_b, cv_w_pw, mla_q_norm_g, mla_kv_norm_g, mla_w_uq, mla_w_uk, mla_w_uv, mla_w_o, w_out, ln1_g, ln1_b, ln2_g, ln2_b, ffn_w1, ffn_w3, ffn_w2, moe_router, moe_w1, moe_w3, moe_w2):
    B, S, D = x_prompt.shape
    NB, L, _ = x_sample.shape
    depth = w_in.shape[0]
    past = cache_mla_latent.shape[2]
    GW = gm_ln_g.shape[1]
    CW = cv_ln_g.shape[1]
    RQ = mla_q_norm_g.shape[1]
    RKV = mla_kv_norm_g.shape[1]
    H = N_HEADS
    n_experts = moe_router.shape[2]
    alpha = float((2 * depth) ** 0.25)
    TP, TS = B * S, NB * L
    T = TP + TS
    off_cv = 2 * GW
    off_q = off_cv + 2 * CW
    off_kv = off_q + RQ
    off_kr = off_kv + RKV
    off_g = off_kr + QK_ROPE
    assert L <= GM_CHUNK and GM_CHUNK % L == 0 and S % GM_CHUNK == 0 and TS % GM_CHUNK == 0
    assert L >= CV_KERNEL - 1 and L % 8 == 0 and CV_KERNEL - 1 <= CONV_HALO

    w_in_t = jnp.swapaxes(w_in, 1, 2).astype(BF16)
    wkr_t = w_in_t[:, off_kr:off_g, :]
    half_r = QK_ROPE // 2
    wkr2_t = jnp.concatenate([wkr_t, -wkr_t[:, half_r:], wkr_t[:, :half_r]], axis=1)
    wg_t = w_in_t[:, off_g:, :]
    gm_w_p_b, cv_w_pw_b = gm_w_p.astype(BF16), cv_w_pw.astype(BF16)
    w_o_b, w_out_b = mla_w_o.astype(BF16), w_out.astype(BF16)
    ffn_w1_b, ffn_w3_b, ffn_w2_b = (w.astype(BF16)[:, None] for w in (ffn_w1, ffn_w3, ffn_w2))
    moe_w1_b, moe_w3_b, moe_w2_b = moe_w1.astype(BF16), moe_w3.astype(BF16), moe_w2.astype(BF16)
    zero_uv = jnp.zeros((1, off_cv), F32)

    uq = mla_w_uq.reshape(depth, RQ, H, QK_NOPE + QK_ROPE)
    uq_rope = uq[..., QK_NOPE:]
    wq3 = jnp.concatenate([uq[..., :QK_NOPE], uq_rope, _rot_half_cols(uq_rope)], axis=-1)
    wq3 = jnp.transpose(wq3, (0, 2, 1, 3)).astype(BF16)
    wkv3 = jnp.concatenate([mla_w_uk.reshape(depth, RKV, H, QK_NOPE),
                            mla_w_uv.reshape(depth, RKV, H, V_DIM)], axis=-1)
    wkv3 = jnp.transpose(wkv3, (0, 2, 1, 3)).astype(BF16)
    wq3t = jnp.swapaxes(wq3, 2, 3)

    causal = jnp.tril(jnp.ones((GM_CHUNK, GM_CHUNK), bool))
    ws_p = jnp.where(causal, gm_w_s, 0.0)
    reps = GM_CHUNK // L
    ws_l = jnp.where(causal[:L, :L], gm_w_s[:, :, :L, :L], 0.0)
    ws_s = jnp.einsum("ab,lgij->lgaibj", jnp.eye(reps, dtype=F32), ws_l).reshape(depth, GM_GROUPS, GM_CHUNK, GM_CHUNK)
    ws2 = jnp.stack([ws_p, ws_s], axis=1).astype(BF16)
    gd = GW // GM_GROUPS
    bs_p = jnp.repeat(jnp.transpose(gm_b_s, (0, 2, 1)), gd, axis=2)
    bs_s = jnp.repeat(jnp.tile(jnp.transpose(gm_b_s[:, :, :L], (0, 2, 1)), (1, reps, 1)), gd, axis=2)
    bs2 = jnp.stack([bs_p, bs_s], axis=1)

    router_pad = jnp.pad(moe_router, ((0, 0), (0, 0), (0, LANES - n_experts)))

    half = QK_ROPE // 2
    inv = ROPE_THETA ** (-jnp.arange(half, dtype=F32) / half)
    pos = jnp.concatenate([jnp.tile(jnp.arange(S), B), jnp.tile(past + jnp.arange(L), NB)]).astype(F32)
    ang = pos[:, None] * inv[None, :]
    cos = jnp.tile(jnp.cos(ang), (1, 2))
    sin = jnp.tile(jnp.sin(ang), (1, 2))
    cos_t, sin_t = cos[:TP].T, sin[:TP].T
    scale_log2e = ATTN_SCALE * math.log2(math.e)

    cache_kr_t = jnp.swapaxes(cache_mla_krope, 2, 3)
    hist_s = jnp.pad(cache_conv, ((0, 0), (0, 0), (CONV_HALO - (CV_KERNEL - 1), 0), (0, 0)))

    x = jnp.concatenate([x_prompt.reshape(TP, D), x_sample.reshape(TS, D)], axis=0)
    xb = x.astype(BF16)
    row2 = lambda a: a.reshape(1, -1)

    tm_dense = _tile(T, 512)
    eid_dense = jnp.zeros((T // tm_dense,), jnp.int32)
    nu_dense = jnp.full((1,), T // tm_dense, jnp.int32)

    outs = {k: [] for k in ("lat_p", "kr_p", "conv_p", "lat_s", "kr_s", "conv_s", "v_s")}
    for l in range(depth):
        uvg = _mm_act(xb, w_in_t, l, 0, off_cv, zero_uv, "gelu", F32)
        glu = _mm_glu(xb, w_in_t, l, off_cv, off_cv + CW, CW)
        cqn, ckv, kr = _latents(xb, w_in_t, wkr2_t, l, off_q, off_kv, row2(mla_q_norm_g[l]),
                                row2(mla_kv_norm_g[l]), cos, sin)

        us, v_ln = _gmlp(uvg, ws2[l], bs2[l], row2(gm_ln_g[l]), row2(gm_ln_b[l]), TP)

        cv_args = (cv_w_dw[l], row2(cv_b_dw[l]), row2(cv_ln_g[l]), row2(cv_ln_b[l]))
        yb_p = _conv(glu, glu, 0, B, S, *cv_args, zero_first=True)
        yb_s = _conv(glu, hist_s[l], TP, NB, L, *cv_args, zero_first=False)

        qt_p = _qproj_t(cqn, wq3t[l], cos_t, sin_t, B, S, scale_log2e)
        q_s = _qproj(cqn, wq3[l], cos, sin, TP, NB, L)
        k_p, vt_p = _kvproj(ckv, kr, wkv3[l], B, S)
        o_p = _attn_prompt(qt_p, k_p, vt_p)
        o_s = _attn_sample(q_s, cache_mla_latent, cache_kr_t, l, ckv, kr, wkv3[l], TP)

        merged = _merge(xb, us, yb_p, yb_s, o_p, o_s, gm_w_p_b, cv_w_pw_b, w_o_b, wg_t, row2(b_gate[l]), l)
        x, xb = _outproj_ln(merged, w_out_b, l, x, row2(ln1_g[l]), row2(ln1_b[l]), alpha)

        j = l // 2
        if l % 2 == 0:
            x, xb = _ffn(xb, eid_dense, nu_dense, ffn_w1_b, ffn_w3_b, ffn_w2_b, j, tm_dense, dense=(x, row2(ln2_g[l]), row2(ln2_b[l]), alpha))
        else:
            route = _router(x, router_pad[j], n_experts)
            slot_tok, block_e, n_used, dest_kt = _moe_plan(route, n_experts, MOE_TM)
            xg = _gather_rows(x, slot_tok)
            y = _ffn(xg, block_e, n_used, moe_w1_b, moe_w3_b, moe_w2_b, j, MOE_TM)
            yg = _gather_rows(y, dest_kt)
            x, xb = _combine(x, yg, route, row2(ln2_g[l]), row2(ln2_b[l]), alpha)

        outs["lat_p"].append(ckv[:TP].reshape(B, S, RKV))
        outs["kr_p"].append(kr[:TP].reshape(B, S, QK_ROPE))
        outs["conv_p"].append(jnp.stack([glu[(s + 1) * S - (CV_KERNEL - 1):(s + 1) * S] for s in range(B)]))
        outs["lat_s"].append(ckv[TP:].reshape(NB, L, RKV))
        outs["kr_s"].append(kr[TP:].reshape(NB, L, QK_ROPE))
        outs["conv_s"].append(glu[TP:].reshape(NB, L, CW)[:, L - (CV_KERNEL - 1):])
        outs["v_s"].append(v_ln[TP:].reshape(NB, L, GW))

    return (x[:TP].reshape(B, S, D), x[TP:].reshape(NB, L, D),
            jnp.stack(outs["lat_p"]), jnp.stack(outs["kr_p"]), jnp.stack(outs["conv_p"]),
            jnp.stack(outs["lat_s"]), jnp.stack(outs["kr_s"]), jnp.stack(outs["conv_s"]),
            jnp.stack(outs["v_s"]))
```

```python
import functools
import math

import jax
import jax.numpy as jnp
from jax import lax
from jax.experimental import pallas as pl
from jax.experimental.pallas import tpu as pltpu

F32 = jnp.float32
BF16 = jnp.bfloat16

CHUNK = 64
GM_GROUPS = 8
GM_CHUNK = 128
CV_KERNEL = 31
N_HEADS = 16
QK_NOPE = 128
QK_ROPE = 64
V_DIM = 128
QK_PAD = 256
V_ONES = 16
ROPE_THETA = 10000.0
TOP_K = 2
LN_EPS = 1e-5
RMS_EPS = 1e-6
ATTN_SCALE = (QK_NOPE + QK_ROPE) ** -0.5

V7X_VMEM_LIMIT_BYTES = 56 * 1024 * 1024
LANES = 128
SUBLANES = 8
CONV_HALO = 32
MOE_TM = 512
FFN_CHUNKS = 2
GATHER_UNROLL = 8


def _cp(*sem):
    return pltpu.CompilerParams(dimension_semantics=sem,
                                vmem_limit_bytes=V7X_VMEM_LIMIT_BYTES)


def _tile(n, pref, mult=8):
    if n <= pref:
        return n
    for t in range(pref, 0, -1):
        if n % t == 0 and t % mult == 0:
            return t
    return n


def _ln(x, g, b):
    mu = jnp.mean(x, axis=-1, keepdims=True)
    xc = x - mu
    var = jnp.mean(xc * xc, axis=-1, keepdims=True)
    return xc * lax.rsqrt(var + LN_EPS) * g + b


def _rms(x, g):
    ms = jnp.mean(x * x, axis=-1, keepdims=True)
    return x * lax.rsqrt(ms + RMS_EPS) * g


def _dot(a, b):
    return jnp.dot(a, b, preferred_element_type=F32)


def _dot_nt(a, b):
    return lax.dot_general(a, b, (((1,), (1,)), ((), ())), preferred_element_type=F32)


def _wt_spec(layer, row0, tn, K, jmap):
    assert row0 % tn == 0
    return pl.BlockSpec((1, tn, K), lambda *ij: (layer, row0 // tn + jmap(*ij), 0))


def _mm_act_kernel(x_ref, wt_ref, b_ref, o_ref, *, act):
    acc = _dot_nt(x_ref[...], wt_ref[0]) + b_ref[...]
    if act == "gelu":
        acc = jax.nn.gelu(acc)
    elif act == "sigmoid":
        acc = jax.nn.sigmoid(acc)
    o_ref[...] = acc.astype(o_ref.dtype)


def _mm_act(x, wt, layer, row0, N, b, act, out_dtype, tm=1024, tn=1024):
    M, K = x.shape
    tm, tn = _tile(M, tm), _tile(N, tn, LANES)
    return pl.pallas_call(
        functools.partial(_mm_act_kernel, act=act),
        grid=(M // tm, N // tn),
        in_specs=[pl.BlockSpec((tm, K), lambda i, j: (i, 0)),
                  _wt_spec(layer, row0, tn, K, lambda i, j: j),
                  pl.BlockSpec((1, tn), lambda i, j: (0, j))],
        out_specs=pl.BlockSpec((tm, tn), lambda i, j: (i, j)),
        out_shape=jax.ShapeDtypeStruct((M, N), out_dtype),
        compiler_params=_cp("parallel", "arbitrary"),
        name="mm_" + act,
    )(x, wt, b)


def _mm_glu_kernel(x_ref, wa_ref, wb_ref, o_ref):
    x = x_ref[...]
    o_ref[...] = _dot_nt(x, wa_ref[0]) * jax.nn.sigmoid(_dot_nt(x, wb_ref[0]))


def _mm_glu(x, wt, layer, row_a, row_b, N, tm=1024, tn=512):
    M, K = x.shape
    tm, tn = _tile(M, tm), _tile(N, tn, LANES)
    return pl.pallas_call(
        _mm_glu_kernel,
        grid=(M // tm, N // tn),
        in_specs=[pl.BlockSpec((tm, K), lambda i, j: (i, 0)),
                  _wt_spec(layer, row_a, tn, K, lambda i, j: j),
                  _wt_spec(layer, row_b, tn, K, lambda i, j: j)],
        out_specs=pl.BlockSpec((tm, tn), lambda i, j: (i, j)),
        out_shape=jax.ShapeDtypeStruct((M, N), F32),
        compiler_params=_cp("parallel", "arbitrary"),
        name="mm_glu",
    )(x, wt, wt)


def _latent_kernel(x_ref, wq_ref, wkv_ref, wkr_ref, gq_ref, gkv_ref, cos_ref, sin_ref,
                   cq_ref, ckv_ref, kr_ref):
    x = x_ref[...]
    cq_ref[...] = _rms(_dot_nt(x, wq_ref[0]), gq_ref[...]).astype(cq_ref.dtype)
    ckv_ref[...] = _rms(_dot_nt(x, wkv_ref[0]), gkv_ref[...])
    r = _dot_nt(x, wkr_ref[0])
    kr_ref[...] = r[:, :QK_ROPE] * cos_ref[...] + r[:, QK_ROPE:] * sin_ref[...]


def _latents(x, wt, wkr2t, layer, row_q, row_kv, gq, gkv, cos, sin, tm=512):
    M, K = x.shape
    Rq, Rkv = gq.shape[1], gkv.shape[1]
    tm = _tile(M, tm)
    row = lambda i: (i, 0)
    full = lambda i: (0, 0)
    return pl.pallas_call(
        _latent_kernel,
        grid=(M // tm,),
        in_specs=[pl.BlockSpec((tm, K), row),
                  _wt_spec(layer, row_q, Rq, K, lambda i: 0), _wt_spec(layer, row_kv, Rkv, K, lambda i: 0),
                  pl.BlockSpec((1, 2 * QK_ROPE, K), lambda i: (layer, 0, 0)),
                  pl.BlockSpec((1, Rq), full), pl.BlockSpec((1, Rkv), full),
                  pl.BlockSpec((tm, QK_ROPE), row), pl.BlockSpec((tm, QK_ROPE), row)],
        out_specs=[pl.BlockSpec((tm, Rq), row), pl.BlockSpec((tm, Rkv), row),
                   pl.BlockSpec((tm, QK_ROPE), row)],
        out_shape=[jax.ShapeDtypeStruct((M, Rq), BF16), jax.ShapeDtypeStruct((M, Rkv), F32),
                   jax.ShapeDtypeStruct((M, QK_ROPE), F32)],
        compiler_params=_cp("parallel"),
        name="latents",
    )(x, wt, wt, wkr2t, gq, gkv, cos, sin)


def _gmlp_kernel(uv_ref, ws_ref, bs_ref, g_ref, b_ref, us_ref, v_ref, *, n_chunks, groups):
    W = v_ref.shape[1]
    gd = W // groups
    vn = _ln(uv_ref[:, W:], g_ref[...], b_ref[...])
    v_ref[...] = vn
    vb = vn.astype(BF16)
    for c in range(n_chunks):
        r0 = c * GM_CHUNK
        for g in range(groups):
            c0 = g * gd
            s = _dot(ws_ref[0, g], vb[r0:r0 + GM_CHUNK, c0:c0 + gd]) + bs_ref[0, :, c0:c0 + gd]
            u = uv_ref[r0:r0 + GM_CHUNK, c0:c0 + gd]
            us_ref[r0:r0 + GM_CHUNK, c0:c0 + gd] = (u * s).astype(us_ref.dtype)


def _gmlp(uvg, ws2, bs2, g, b, n_prompt_rows, tm=512):
    M, W2 = uvg.shape
    W = W2 // 2
    tm = _tile(math.gcd(n_prompt_rows, M - n_prompt_rows), tm, GM_CHUNK)
    npt = n_prompt_rows // tm
    sel = lambda i: (jnp.minimum(i // npt, 1), 0, 0, 0)
    sel3 = lambda i: (jnp.minimum(i // npt, 1), 0, 0)
    row = lambda i: (i, 0)
    full = lambda i: (0, 0)
    return pl.pallas_call(
        functools.partial(_gmlp_kernel, n_chunks=tm // GM_CHUNK, groups=GM_GROUPS),
        grid=(M // tm,),
        in_specs=[pl.BlockSpec((tm, W2), row),
                  pl.BlockSpec((1, GM_GROUPS, GM_CHUNK, GM_CHUNK), sel),
                  pl.BlockSpec((1, GM_CHUNK, W), sel3),
                  pl.BlockSpec((1, W), full), pl.BlockSpec((1, W), full)],
        out_specs=[pl.BlockSpec((tm, W), row), pl.BlockSpec((tm, W), row)],
        out_shape=[jax.ShapeDtypeStruct((M, W), BF16), jax.ShapeDtypeStruct((M, W), F32)],
        compiler_params=_cp("parallel"),
        name="gmlp",
    )(uvg, ws2, bs2, g, b)


def _conv_kernel(cur_ref, halo_ref, w_ref, bdw_ref, g_ref, b_ref, o_ref, full_ref, sh_ref, acc_ref,
                 *, tm, zero_first):
    C = cur_ref.shape[1]
    halo = halo_ref[...].reshape(CONV_HALO, C)
    if zero_first:
        halo = jnp.where(pl.program_id(1) == 0, 0.0, halo)
    full_ref[0:CONV_HALO, :] = halo
    full_ref[CONV_HALO:CONV_HALO + tm, :] = cur_ref[...]
    off = CONV_HALO - (CV_KERNEL - 1)
    ns = sh_ref.shape[1]

    def lane_block(c, carry):
        c0 = pl.multiple_of(c * LANES, LANES)
        sh_ref[0] = full_ref[:, pl.ds(c0, LANES)]
        for r in range(1, SUBLANES):
            sh_ref[r, :ns - SUBLANES, :] = full_ref[pl.ds(r, ns - SUBLANES), pl.ds(c0, LANES)]
        wk = w_ref[:, pl.ds(c0, LANES)]
        wrows = [jnp.broadcast_to(wk[k:k + 1, :], (SUBLANES, LANES)) for k in range(CV_KERNEL)]
        for rb in range(tm // SUBLANES):
            acc = jnp.zeros((SUBLANES, LANES), F32)
            for k in range(CV_KERNEL):
                a, r = divmod(off + k, SUBLANES)
                acc = acc + sh_ref[r, pl.ds((rb + a) * SUBLANES, SUBLANES), :] * wrows[k]
            acc_ref[pl.ds(rb * SUBLANES, SUBLANES), pl.ds(c0, LANES)] = acc
        return carry

    lax.fori_loop(0, C // LANES, lane_block, 0)
    y = _ln(acc_ref[...] + bdw_ref[...], g_ref[...], b_ref[...])
    o_ref[...] = (y * jax.nn.sigmoid(y)).astype(o_ref.dtype)


def _conv(glu, halo_src, row0, n_seq, L, w_dw, b_dw, g, b, zero_first, tm=256):
    C = glu.shape[1]
    tm = _tile(L, tm, CONV_HALO)
    lt = L // tm
    hb = tm // CONV_HALO
    assert row0 % tm == 0
    r0 = row0 // tm
    if zero_first:
        halo_spec = pl.BlockSpec((CONV_HALO, C), lambda s, i: (jnp.maximum((r0 + s * lt + i) * hb - 1, 0), 0))
    else:
        halo_spec = pl.BlockSpec((1, CONV_HALO, C), lambda s, i: (s, 0, 0))
    full = lambda s, i: (0, 0)
    return pl.pallas_call(
        functools.partial(_conv_kernel, tm=tm, zero_first=zero_first),
        grid=(n_seq, lt),
        in_specs=[pl.BlockSpec((tm, C), lambda s, i: (r0 + s * lt + i, 0)),
                  halo_spec,
                  pl.BlockSpec((CV_KERNEL, C), full),
                  pl.BlockSpec((1, C), full), pl.BlockSpec((1, C), full), pl.BlockSpec((1, C), full)],
        out_specs=pl.BlockSpec((tm, C), lambda s, i: (s * lt + i, 0)),
        out_shape=jax.ShapeDtypeStruct((n_seq * L, C), BF16),
        scratch_shapes=[pltpu.VMEM((CONV_HALO + tm, C), F32),
                        pltpu.VMEM((SUBLANES, CONV_HALO + tm, LANES), F32),
                        pltpu.VMEM((tm, C), F32)],
        compiler_params=_cp("parallel", "arbitrary"),
        name="conv_prompt" if zero_first else "conv_sample",
    )(glu, halo_src, w_dw, b_dw, g, b)


def _qproj_kernel(cq_ref, w_ref, cos_ref, sin_ref, q_ref, *, hg, nb, L):
    cq = cq_ref[...]
    cos, sin = cos_ref[...], sin_ref[...]
    for h in range(hg):
        r = _dot(cq, w_ref[h])
        nope = r[:, :QK_NOPE] * ATTN_SCALE
        rp = (r[:, QK_NOPE:QK_NOPE + QK_ROPE] * cos + r[:, QK_NOPE + QK_ROPE:] * sin) * ATTN_SCALE
        if nb == 1:
            q_ref[0, h, :, :QK_NOPE] = nope.astype(q_ref.dtype)
            q_ref[0, h, :, QK_NOPE:] = rp.astype(q_ref.dtype)
        else:
            q_ref[:, h, :, :QK_NOPE] = nope.reshape(nb, L, QK_NOPE).astype(q_ref.dtype)
            q_ref[:, h, :, QK_NOPE:] = rp.reshape(nb, L, QK_ROPE).astype(q_ref.dtype)


def _qproj(cqn, wq3, cos, sin, row0, n_seq, L, hg=4, tm=512):
    R = cqn.shape[1]
    H = wq3.shape[0]
    hg = _tile(H, hg, 1)
    dq = QK_NOPE + QK_ROPE
    if L >= tm:
        tm = _tile(L, tm)
        nb, lt = 1, L // tm
        out_spec = pl.BlockSpec((1, hg, tm, dq), lambda i, h: (i // lt, h, i % lt, 0))
    else:
        nb = _tile(n_seq, max(tm // L, 1), 1)
        tm = nb * L
        out_spec = pl.BlockSpec((nb, hg, L, dq), lambda i, h: (i, h, 0, 0))
    assert row0 % tm == 0
    r0 = row0 // tm
    row = lambda i, h: (r0 + i, 0)
    return pl.pallas_call(
        functools.partial(_qproj_kernel, hg=hg, nb=nb, L=L),
        grid=(n_seq * L // tm, H // hg),
        in_specs=[pl.BlockSpec((tm, R), row),
                  pl.BlockSpec((hg, R, wq3.shape[2]), lambda i, h: (h, 0, 0)),
                  pl.BlockSpec((tm, QK_ROPE), row), pl.BlockSpec((tm, QK_ROPE), row)],
        out_specs=out_spec,
        out_shape=jax.ShapeDtypeStruct((n_seq, H, L, dq), BF16),
        compiler_params=_cp("parallel", "arbitrary"),
        name="qproj",
    )(cqn, wq3, cos, sin)


def _qproj_t_kernel(cq_ref, w_ref, cos_ref, sin_ref, q_ref, *, hg, scale):
    cq = cq_ref[...]
    cos, sin = cos_ref[...], sin_ref[...]
    for h in range(hg):
        r = _dot_nt(w_ref[h], cq)
        rp = r[QK_NOPE:QK_NOPE + QK_ROPE] * cos + r[QK_NOPE + QK_ROPE:] * sin
        q_ref[0, h, :QK_NOPE, :] = (r[:QK_NOPE] * scale).astype(q_ref.dtype)
        q_ref[0, h, QK_NOPE:QK_NOPE + QK_ROPE, :] = (rp * scale).astype(q_ref.dtype)
        q_ref[0, h, QK_NOPE + QK_ROPE:, :] = jnp.zeros((QK_PAD - QK_NOPE - QK_ROPE, cq.shape[0]), q_ref.dtype)


def _qproj_t(cqn, wq3t, cos_t, sin_t, n_seq, L, scale, hg=4, tm=512):
    R = cqn.shape[1]
    H = wq3t.shape[0]
    hg = _tile(H, hg, 1)
    tm = _tile(L, tm, LANES)
    lt = L // tm
    dq = QK_PAD
    return pl.pallas_call(
        functools.partial(_qproj_t_kernel, hg=hg, scale=scale),
        grid=(n_seq * lt, H // hg),
        in_specs=[pl.BlockSpec((tm, R), lambda i, h: (i, 0)),
                  pl.BlockSpec((hg, wq3t.shape[1], R), lambda i, h: (h, 0, 0)),
                  pl.BlockSpec((QK_ROPE, tm), lambda i, h: (0, i)),
                  pl.BlockSpec((QK_ROPE, tm), lambda i, h: (0, i))],
        out_specs=pl.BlockSpec((1, hg, dq, tm), lambda i, h: (i // lt, h, 0, i % lt)),
        out_shape=jax.ShapeDtypeStruct((n_seq, H, dq, L), BF16),
        compiler_params=_cp("parallel", "arbitrary"),
        name="qproj_t",
    )(cqn, wq3t, cos_t, sin_t)


def _kvproj_kernel(ckv_ref, kr_ref, w_ref, k_ref, vt_ref, *, hg):
    ckv = ckv_ref[...].astype(BF16)
    kr = kr_ref[...].astype(BF16)
    for h in range(hg):
        r = _dot(ckv, w_ref[h])
        k_ref[0, h, :, :QK_NOPE] = r[:, :QK_NOPE].astype(k_ref.dtype)
        k_ref[0, h, :, QK_NOPE:QK_NOPE + QK_ROPE] = kr
        k_ref[0, h, :, QK_NOPE + QK_ROPE:] = jnp.zeros((kr.shape[0], QK_PAD - QK_NOPE - QK_ROPE), k_ref.dtype)
        vt_ref[0, h, :V_DIM, :] = r[:, QK_NOPE:].T.astype(vt_ref.dtype)
        vt_ref[0, h, V_DIM:, :] = jnp.ones((V_ONES, ckv.shape[0]), vt_ref.dtype)


def _kvproj(ckv, kr, wkv3, n_seq, L, hg=4, tm=512):
    R = ckv.shape[1]
    H = wkv3.shape[0]
    hg = _tile(H, hg, 1)
    tm = _tile(L, tm, LANES)
    lt = L // tm
    dq = QK_PAD
    row = lambda i, h: (i, 0)
    wmap = lambda i, h: (h, 0, 0)
    return pl.pallas_call(
        functools.partial(_kvproj_kernel, hg=hg),
        grid=(n_seq * lt, H // hg),
        in_specs=[pl.BlockSpec((tm, R), row), pl.BlockSpec((tm, QK_ROPE), row),
                  pl.BlockSpec((hg, R, wkv3.shape[2]), wmap)],
        out_specs=[pl.BlockSpec((1, hg, tm, dq), lambda i, h: (i // lt, h, i % lt, 0)),
                   pl.BlockSpec((1, hg, V_DIM + V_ONES, tm), lambda i, h: (i // lt, h, 0, i % lt))],
        out_shape=[jax.ShapeDtypeStruct((n_seq, H, L, dq), BF16),
                   jax.ShapeDtypeStruct((n_seq, H, V_DIM + V_ONES, L), BF16)],
        compiler_params=_cp("parallel", "arbitrary"),
        name="kvproj",
    )(ckv, kr, wkv3)


def _attn_prompt_kernel(qt_ref, k_ref, vt_ref, o_ref, s_ref, p_ref, acc_ref, *, tq, tk, hg):
    qi = pl.program_id(2)
    d0 = pl.multiple_of(qi * tq, tq)

    def scores(g, j0, slot):
        s = _dot(k_ref[0, g, pl.ds(j0, tk), :], qt_ref[0, g])
        s_ref[slot, g] = s
        return jnp.max(s, axis=0, keepdims=True)

    def softmax(g, slot, mx, m, mask=None):
        s = s_ref[slot, g]
        if mask is not None:
            s = jnp.where(mask, s, -jnp.inf)
            mx = jnp.max(s, axis=0, keepdims=True)
        m_new = jnp.maximum(m, mx)
        p_ref[slot, g] = jnp.exp2(s - m_new).astype(BF16)
        return jnp.exp2(m - m_new), m_new

    def accumulate(g, j0, slot, alpha):
        acc_ref[g] = alpha * acc_ref[g] + _dot(vt_ref[0, g, :, pl.ds(j0, tk)], p_ref[slot, g])

    def step(j0, slot, state, mask=None, last=False):
        jp = pl.multiple_of(jnp.maximum(j0 - tk, 0), tk)
        new = []
        for g in range(hg):
            mx, a_prev, m = state[g]
            mx_next = mx if last else scores(g, pl.multiple_of(j0 + tk, tk), 1 - slot)
            accumulate(g, jp, 1 - slot, a_prev)
            alpha, m = softmax(g, slot, mx, m, mask)
            new.append((mx_next, alpha, m))
        return tuple(new)

    def body(i, state):
        j0 = pl.multiple_of(2 * i * tk, tk)
        state = step(j0, 0, state)
        return step(pl.multiple_of(j0 + tk, tk), 1, state)

    p_ref[1] = jnp.zeros_like(p_ref[1])
    acc_ref[...] = jnp.zeros_like(acc_ref)
    init = tuple((scores(g, 0, 0), jnp.ones((1, tq), F32), jnp.full((1, tq), -jnp.inf, F32))
                 for g in range(hg))
    state = lax.fori_loop(0, qi * (tq // (2 * tk)), body, init)

    key_chunk = lax.broadcasted_iota(jnp.int32, (tk, tq), 0) // CHUNK
    qry_chunk = lax.broadcasted_iota(jnp.int32, (tk, tq), 1) // CHUNK
    state = step(d0, 0, state, mask=key_chunk <= qry_chunk)
    state = step(pl.multiple_of(d0 + tk, tk), 1, state, mask=key_chunk + tk // CHUNK <= qry_chunk, last=True)
    for g in range(hg):
        accumulate(g, pl.multiple_of(d0 + tk, tk), 1, state[g][1])
        o = acc_ref[g, :V_DIM, :] / acc_ref[g, V_DIM:V_DIM + 1, :]
        o_ref[:, g * V_DIM:(g + 1) * V_DIM] = o.T.astype(o_ref.dtype)


def _attn_prompt(qt, k, vt, tq=512, hg=2):
    B, H, dq, S = qt.shape
    tq = _tile(S, tq, 2 * LANES)
    tk = tq // 2
    hg = _tile(H, hg, 1)
    return pl.pallas_call(
        functools.partial(_attn_prompt_kernel, tq=tq, tk=tk, hg=hg),
        grid=(B, H // hg, S // tq),
        in_specs=[pl.BlockSpec((1, hg, dq, tq), lambda b, h, i: (b, h, 0, i)),
                  pl.BlockSpec((1, hg, S, dq), lambda b, h, i: (b, h, 0, 0)),
                  pl.BlockSpec((1, hg, vt.shape[2], S), lambda b, h, i: (b, h, 0, 0))],
        out_specs=pl.BlockSpec((tq, hg * V_DIM), lambda b, h, i: (b * (S // tq) + i, h)),
        out_shape=jax.ShapeDtypeStruct((B * S, H * V_DIM), BF16),
        scratch_shapes=[pltpu.VMEM((2, hg, tk, tq), F32), pltpu.VMEM((2, hg, tk, tq), BF16),
                        pltpu.VMEM((hg, vt.shape[2], tq), F32)],
        compiler_params=_cp("parallel", "parallel", "arbitrary"),
        name="attn_prompt",
    )(qt, k, vt)


def _attn_sample_kernel(q_ref, plat_ref, pkrt_ref, nlat_ref, nkr_ref, w_ref, o_ref, ql_ref, qr_ref,
                        *, H, L, kc):
    P = plat_ref.shape[2]
    for h in range(H):
        qh = q_ref[0, h]
        ql_ref[h * L:(h + 1) * L, :] = _dot_nt(qh[:, :QK_NOPE], w_ref[h, :, :QK_NOPE]).astype(BF16)
        qr_ref[h * L:(h + 1) * L, :] = qh[:, QK_NOPE:]
    ql = ql_ref[...]
    qr = qr_ref[...]

    def step(lat, s_rope, carry):
        m, l, acc = carry
        s = _dot_nt(ql, lat) + s_rope
        m_new = jnp.maximum(m, jnp.max(s, axis=-1, keepdims=True))
        alpha = jnp.exp(m - m_new)
        p = jnp.exp(s - m_new)
        l = alpha * l + jnp.sum(p, axis=-1, keepdims=True)
        acc = alpha * acc + _dot(p.astype(BF16), lat)
        return m_new, l, acc

    R = ql.shape[1]
    carry = (jnp.full((H * L, 1), -jnp.inf, F32), jnp.zeros((H * L, 1), F32), jnp.zeros((H * L, R), F32))
    for c in range(P // kc):
        krt = pkrt_ref[0, 0, :, c * kc:(c + 1) * kc].astype(BF16)
        carry = step(plat_ref[0, 0, c * kc:(c + 1) * kc, :].astype(BF16), _dot(qr, krt), carry)
    m, l, acc = step(nlat_ref[...].astype(BF16), _dot_nt(qr, nkr_ref[...].astype(BF16)), carry)
    ol = (acc / l).astype(BF16)
    for h in range(H):
        o_ref[:, h * V_DIM:(h + 1) * V_DIM] = _dot(ol[h * L:(h + 1) * L, :], w_ref[h, :, QK_NOPE:]).astype(o_ref.dtype)


def _attn_sample(q, cache_lat, cache_kr_t, layer, ckv, kr, wkv3, row0):
    n_req, H, L, dq = q.shape
    P, R = cache_lat.shape[2], cache_lat.shape[3]
    kc = _tile(P, 1024)
    assert row0 % L == 0
    r0 = row0 // L
    return pl.pallas_call(
        functools.partial(_attn_sample_kernel, H=H, L=L, kc=kc),
        grid=(n_req,),
        in_specs=[pl.BlockSpec((1, H, L, dq), lambda b: (b, 0, 0, 0)),
                  pl.BlockSpec((1, 1, P, R), lambda b: (layer, b, 0, 0)),
                  pl.BlockSpec((1, 1, QK_ROPE, P), lambda b: (layer, b, 0, 0)),
                  pl.BlockSpec((L, R), lambda b: (r0 + b, 0)),
                  pl.BlockSpec((L, QK_ROPE), lambda b: (r0 + b, 0)),
                  pl.BlockSpec(wkv3.shape, lambda b: (0, 0, 0))],
        out_specs=pl.BlockSpec((L, H * V_DIM), lambda b: (b, 0)),
        out_shape=jax.ShapeDtypeStruct((n_req * L, H * V_DIM), BF16),
        scratch_shapes=[pltpu.VMEM((H * L, R), BF16), pltpu.VMEM((H * L, QK_ROPE), BF16)],
        compiler_params=_cp("parallel"),
        name="attn_sample",
    )(q, cache_lat, cache_kr_t, ckv, kr, wkv3)


def _merge_kernel(x_ref, us_ref, ybp_ref, ybs_ref, op_ref, os_ref, wp_ref, wpw_ref, wo_ref,
                  wga_ref, wgb_ref, wgc_ref, ba_ref, bb_ref, bc_ref, out_ref, *, npt):
    prompt = pl.program_id(0) < npt
    x = x_ref[...]
    yb_in = jnp.where(prompt, ybp_ref[...], ybs_ref[...])
    o_in = jnp.where(prompt, op_ref[...], os_ref[...])
    out = jax.nn.sigmoid(_dot_nt(x, wga_ref[0]) + ba_ref[...]) * _dot(us_ref[...], wp_ref[0])
    out += jax.nn.sigmoid(_dot_nt(x, wgb_ref[0]) + bb_ref[...]) * _dot(yb_in, wpw_ref[0])
    out += jax.nn.sigmoid(_dot_nt(x, wgc_ref[0]) + bc_ref[...]) * _dot(o_in, wo_ref[0])
    out_ref[...] = out.astype(out_ref.dtype)


def _merge(xb, us, yb_p, yb_s, o_p, o_s, wp, wpw, wo, wgt, bg, layer, tm=512, tn=512):
    M = us.shape[0]
    D = wp.shape[2]
    TP, TS = yb_p.shape[0], yb_s.shape[0]
    tm, tn = _tile(math.gcd(TP, TS), tm), _tile(D, tn, LANES)
    npt = TP // tm
    nj = D // tn
    row = lambda i, j: (i, 0)
    prow = lambda i, j: (jnp.minimum(i, npt - 1), 0)
    srow = lambda i, j: (jnp.maximum(i - npt, 0), 0)
    col = lambda i, j: (layer, 0, j)
    gcol = lambda g: (lambda i, j: (0, g * nj + j))
    grow = lambda g: _wt_spec(layer, g * D, tn, wgt.shape[2], lambda i, j: j)
    return pl.pallas_call(
        functools.partial(_merge_kernel, npt=npt),
        grid=(M // tm, nj),
        in_specs=[pl.BlockSpec((tm, xb.shape[1]), row), pl.BlockSpec((tm, us.shape[1]), row),
                  pl.BlockSpec((tm, yb_p.shape[1]), prow), pl.BlockSpec((tm, yb_s.shape[1]), srow),
                  pl.BlockSpec((tm, o_p.shape[1]), prow), pl.BlockSpec((tm, o_s.shape[1]), srow),
                  pl.BlockSpec((1, wp.shape[1], tn), col), pl.BlockSpec((1, wpw.shape[1], tn), col),
                  pl.BlockSpec((1, wo.shape[1], tn), col),
                  grow(0), grow(1), grow(2),
                  pl.BlockSpec((1, tn), gcol(0)), pl.BlockSpec((1, tn), gcol(1)), pl.BlockSpec((1, tn), gcol(2))],
        out_specs=pl.BlockSpec((tm, tn), lambda i, j: (i, j)),
        out_shape=jax.ShapeDtypeStruct((M, D), BF16),
        compiler_params=_cp("parallel", "arbitrary"),
        name="merge",
    )(xb, us, yb_p, yb_s, o_p, o_s, wp, wpw, wo, wgt, wgt, wgt, bg, bg, bg)


def _outproj_kernel(m_ref, w_ref, x_ref, g_ref, b_ref, o_ref, ob_ref, *, alpha):
    y = _ln(alpha * x_ref[...] + _dot(m_ref[...], w_ref[0]), g_ref[...], b_ref[...])
    o_ref[...] = y
    ob_ref[...] = y.astype(ob_ref.dtype)


def _outproj_ln(merged, w, layer, x, g, b, alpha, tm=512):
    M, D = x.shape
    tm = _tile(M, tm)
    row = lambda i: (i, 0)
    full = lambda i: (0, 0)
    return pl.pallas_call(
        functools.partial(_outproj_kernel, alpha=alpha),
        grid=(M // tm,),
        in_specs=[pl.BlockSpec((tm, merged.shape[1]), row),
                  pl.BlockSpec((1,) + w.shape[1:], lambda i: (layer, 0, 0)),
                  pl.BlockSpec((tm, D), row), pl.BlockSpec((1, D), full), pl.BlockSpec((1, D), full)],
        out_specs=[pl.BlockSpec((tm, D), row), pl.BlockSpec((tm, D), row)],
        out_shape=[jax.ShapeDtypeStruct((M, D), F32), jax.ShapeDtypeStruct((M, D), BF16)],
        compiler_params=_cp("parallel"),
        name="outproj_ln",
    )(merged, w, x, g, b)


def _swiglu_partial(x, w1_ref, w3_ref, w2_ref):
    c = w1_ref.shape[3] // FFN_CHUNKS
    hs = []
    for k in range(FFN_CHUNKS):
        cols = slice(k * c, (k + 1) * c)
        h = jax.nn.silu(_dot(x, w1_ref[0, 0, :, cols])) * _dot(x, w3_ref[0, 0, :, cols])
        hs.append(h.astype(BF16))
    out = _dot(hs[0], w2_ref[0, 0, 0:c, :])
    for k in range(1, FFN_CHUNKS):
        out += _dot(hs[k], w2_ref[0, 0, k * c:(k + 1) * c, :])
    return out


def _ffn_dense_kernel(eid_ref, nu_ref, x_ref, w1_ref, w3_ref, w2_ref, r_ref, g_ref, b_ref,
                      o_ref, ob_ref, acc_ref, *, alpha):
    f = pl.program_id(1)

    @pl.when(f == 0)
    def _():
        acc_ref[...] = jnp.zeros_like(acc_ref)

    acc_ref[...] += _swiglu_partial(x_ref[...], w1_ref, w3_ref, w2_ref)

    @pl.when(f == pl.num_programs(1) - 1)
    def _():
        y = _ln(alpha * r_ref[...] + acc_ref[...], g_ref[...], b_ref[...])
        o_ref[...] = y
        ob_ref[...] = y.astype(ob_ref.dtype)


def _ffn_moe_kernel(eid_ref, nu_ref, idx0_ref, idxn_ref, x_hbm, w1_ref, w3_ref, w2_ref, o_ref,
                    xg_ref, xb_ref, sem, *, tm):
    i = pl.program_id(0)
    f = pl.program_id(1)
    n_used = nu_ref[0]
    slot = lax.rem(i, 2)

    def row_copy(t, r, s):
        return pltpu.make_async_copy(x_hbm.at[pl.ds(t, 1), :], xg_ref.at[s, pl.ds(r, 1), :], sem.at[s])

    def start_rows(idx_ref, s):
        def body(c, carry):
            for u in range(GATHER_UNROLL):
                r = c * GATHER_UNROLL + u
                row_copy(idx_ref[0, 0, r], r, s).start(priority=u % 2)
            return carry
        lax.fori_loop(0, tm // GATHER_UNROLL, body, 0)

    def wait_rows(s):
        def body(c, carry):
            for u in range(GATHER_UNROLL):
                row_copy(0, c * GATHER_UNROLL + u, s).wait()
            return carry
        lax.fori_loop(0, tm // GATHER_UNROLL, body, 0)

    @pl.when(f == 0)
    def _():
        o_ref[...] = jnp.zeros_like(o_ref)

    @pl.when((f == 0) & (i < n_used))
    def _():
        @pl.when(i == 0)
        def _():
            start_rows(idx0_ref, 0)

        wait_rows(slot)
        xb_ref[...] = xg_ref[slot].astype(BF16)

        @pl.when(i + 1 < n_used)
        def _():
            start_rows(idxn_ref, 1 - slot)

    @pl.when(i < n_used)
    def _():
        o_ref[...] += _swiglu_partial(xb_ref[...], w1_ref, w3_ref, w2_ref)


def _ffn_weight_specs(layer, D, tf):
    last = lambda i, nu: jnp.minimum(i, nu[0] - 1)
    w13 = lambda i, f, e, nu: (layer, e[last(i, nu)], 0, jnp.where(i < nu[0], f, 0))
    w2m = lambda i, f, e, nu: (layer, e[last(i, nu)], jnp.where(i < nu[0], f, 0), 0)
    return [pl.BlockSpec((1, 1, D, tf), w13), pl.BlockSpec((1, 1, D, tf), w13), pl.BlockSpec((1, 1, tf, D), w2m)]


def _ffn_dense(x, eid, n_used, w1, w3, w2, layer, tm, resid, g, b, alpha, tf=512):
    M, D = x.shape
    Fdim = w1.shape[3]
    tf = _tile(Fdim, tf, LANES)
    row = lambda i, f, e, nu: (i, 0)
    full = lambda i, f, e, nu: (0, 0)
    return pl.pallas_call(
        functools.partial(_ffn_dense_kernel, alpha=alpha),
        grid_spec=pltpu.PrefetchScalarGridSpec(
            num_scalar_prefetch=2, grid=(M // tm, Fdim // tf),
            in_specs=[pl.BlockSpec((tm, D), row)] + _ffn_weight_specs(layer, D, tf)
            + [pl.BlockSpec((tm, D), row), pl.BlockSpec((1, D), full), pl.BlockSpec((1, D), full)],
            out_specs=[pl.BlockSpec((tm, D), row), pl.BlockSpec((tm, D), row)],
            scratch_shapes=[pltpu.VMEM((tm, D), F32)]),
        out_shape=[jax.ShapeDtypeStruct((M, D), F32), jax.ShapeDtypeStruct((M, D), BF16)],
        compiler_params=_cp("arbitrary", "arbitrary"),
        name="ffn_dense",
    )(eid, n_used, x, w1, w3, w2, resid, g, b)


def _ffn_moe(x, slot_tok, eid, n_used, w1, w3, w2, layer, tm, tf=512):
    D = x.shape[1]
    n_slots = slot_tok.shape[0]
    nb = n_slots // tm
    Fdim = w1.shape[3]
    tf = _tile(Fdim, tf, LANES)
    idx3 = slot_tok.reshape(nb, 1, tm)
    smem = functools.partial(pl.BlockSpec, (1, 1, tm), memory_space=pltpu.SMEM)
    return pl.pallas_call(
        functools.partial(_ffn_moe_kernel, tm=tm),
        grid_spec=pltpu.PrefetchScalarGridSpec(
            num_scalar_prefetch=2, grid=(nb, Fdim // tf),
            in_specs=[smem(lambda i, f, e, nu: (0, 0, 0)),
                      smem(lambda i, f, e, nu: (jnp.minimum(i + 1, nb - 1), 0, 0)),
                      pl.BlockSpec(memory_space=pl.ANY)] + _ffn_weight_specs(layer, D, tf),
            out_specs=pl.BlockSpec((tm, D), lambda i, f, e, nu: (i, 0)),
            scratch_shapes=[pltpu.VMEM((2, tm, D), F32), pltpu.VMEM((tm, D), BF16),
                            pltpu.SemaphoreType.DMA((2,))]),
        out_shape=jax.ShapeDtypeStruct((n_slots, D), F32),
        compiler_params=_cp("arbitrary", "arbitrary"),
        name="ffn_moe",
    )(eid, n_used, idx3, idx3, x, w1, w3, w2)


def _router_kernel(x_ref, r_ref, o_ref, *, n_experts):
    logits = jnp.dot(x_ref[...], r_ref[...], preferred_element_type=F32,
                     precision=lax.Precision.HIGHEST)
    col = lax.broadcasted_iota(jnp.int32, logits.shape, 1)
    lg = jnp.where(col < n_experts, logits, -jnp.inf)
    m1 = jnp.max(lg, axis=-1, keepdims=True)
    i1 = jnp.min(jnp.where(lg == m1, col, LANES), axis=-1, keepdims=True)
    lg2 = jnp.where(col == i1, -jnp.inf, lg)
    m2 = jnp.max(lg2, axis=-1, keepdims=True)
    i2 = jnp.min(jnp.where(lg2 == m2, col, LANES), axis=-1, keepdims=True)
    e = jnp.exp(m2 - m1)
    g1 = 1.0 / (1.0 + e)
    g2 = e * g1
    out = jnp.where(col == 0, i1.astype(F32),
                    jnp.where(col == 1, i2.astype(F32),
                              jnp.where(col == 2, g1, jnp.where(col == 3, g2, 0.0))))
    o_ref[...] = out


def _router(x, router_pad, n_experts, tm=512):
    M, D = x.shape
    tm = _tile(M, tm)
    return pl.pallas_call(
        functools.partial(_router_kernel, n_experts=n_experts),
        grid=(M // tm,),
        in_specs=[pl.BlockSpec((tm, D), lambda i: (i, 0)), pl.BlockSpec((D, LANES), lambda i: (0, 0))],
        out_specs=pl.BlockSpec((tm, LANES), lambda i: (i, 0)),
        out_shape=jax.ShapeDtypeStruct((M, LANES), F32),
        compiler_params=_cp("parallel"),
        name="router",
    )(x, router_pad)


def _combine_kernel(idx0_ref, idxn_ref, x_ref, r_ref, g_ref, b_ref, y_hbm, o_ref, ob_ref, yg_ref, sem,
                    *, tm, alpha):
    i = pl.program_id(0)
    slot = lax.rem(i, 2)

    def row_copy(t, k, r, s):
        return pltpu.make_async_copy(y_hbm.at[pl.ds(t, 1), :], yg_ref.at[s, k, pl.ds(r, 1), :], sem.at[s])

    def start_rows(idx_ref, s):
        def body(c, carry):
            for u in range(GATHER_UNROLL):
                r = c * GATHER_UNROLL + u
                for k in range(TOP_K):
                    row_copy(idx_ref[0, k, r], k, r, s).start(priority=(u + k) % 2)
            return carry
        lax.fori_loop(0, tm // GATHER_UNROLL, body, 0)

    def wait_rows(s):
        def body(c, carry):
            for u in range(GATHER_UNROLL):
                for k in range(TOP_K):
                    row_copy(0, k, c * GATHER_UNROLL + u, s).wait()
            return carry
        lax.fori_loop(0, tm // GATHER_UNROLL, body, 0)

    @pl.when(i == 0)
    def _():
        start_rows(idx0_ref, 0)

    wait_rows(slot)

    @pl.when(i + 1 < pl.num_programs(0))
    def _():
        start_rows(idxn_ref, 1 - slot)

    r = r_ref[...]
    y = alpha * x_ref[...] + r[:, 2:3] * yg_ref[slot, 0] + r[:, 3:4] * yg_ref[slot, 1]
    y = _ln(y, g_ref[...], b_ref[...])
    o_ref[...] = y
    ob_ref[...] = y.astype(ob_ref.dtype)


def _combine(x, y, dest, route, g, b, alpha, tm=256):
    M, D = x.shape
    tm = _tile(M, tm)
    nt = M // tm
    idx3 = jnp.transpose(dest.reshape(nt, tm, TOP_K), (0, 2, 1))
    smem = functools.partial(pl.BlockSpec, (1, TOP_K, tm), memory_space=pltpu.SMEM)
    row = lambda i: (i, 0)
    full = lambda i: (0, 0)
    return pl.pallas_call(
        functools.partial(_combine_kernel, tm=tm, alpha=alpha),
        grid=(nt,),
        in_specs=[smem(lambda i: (0, 0, 0)), smem(lambda i: (jnp.minimum(i + 1, nt - 1), 0, 0)),
                  pl.BlockSpec((tm, D), row), pl.BlockSpec((tm, LANES), row),
                  pl.BlockSpec((1, D), full), pl.BlockSpec((1, D), full),
                  pl.BlockSpec(memory_space=pl.ANY)],
        out_specs=[pl.BlockSpec((tm, D), row), pl.BlockSpec((tm, D), row)],
        out_shape=[jax.ShapeDtypeStruct((M, D), F32), jax.ShapeDtypeStruct((M, D), BF16)],
        scratch_shapes=[pltpu.VMEM((2, TOP_K, tm, D), F32), pltpu.SemaphoreType.DMA((2,))],
        compiler_params=_cp("arbitrary"),
        name="combine",
    )(idx3, idx3, x, route, g, b, y)


def _moe_plan(route, n_experts, tm):
    T = route.shape[0]
    n_assign = T * TOP_K
    e_flat = route[:, :TOP_K].astype(jnp.int32).reshape(-1)
    onehot = (e_flat[:, None] == jnp.arange(n_experts, dtype=jnp.int32)[None, :]).astype(jnp.int32)
    csum = jnp.cumsum(onehot, axis=0)
    rank = jnp.take_along_axis(csum, e_flat[:, None], axis=1)[:, 0] - 1
    counts = csum[-1]
    padded = ((counts + tm - 1) // tm) * tm
    pad_end = jnp.cumsum(padded)
    pad_start = pad_end - padded
    dest = (pad_start[e_flat] + rank).astype(jnp.int32)
    n_blocks = -(-n_assign // tm) + n_experts
    n_slots = n_blocks * tm
    slot_tok = jnp.zeros((n_slots,), jnp.int32).at[dest].set(jnp.arange(n_assign, dtype=jnp.int32) // TOP_K)
    block_e = jnp.minimum(jnp.searchsorted(pad_end, jnp.arange(n_blocks, dtype=jnp.int32) * tm, side="right"),
                          n_experts - 1).astype(jnp.int32)
    n_used = (pad_end[-1:] // tm).astype(jnp.int32)
    return slot_tok, block_e, n_used, dest.reshape(T, TOP_K)


def _rot_half_cols(w):
    half = w.shape[-1] // 2
    return jnp.concatenate([-w[..., half:], w[..., :half]], axis=-1)


def kernel(x_prompt, x_sample, cache_mla_latent, cache_mla_krope, cache_conv, w_in, b_gate, gm_ln_g, gm_ln_b, gm_w_s, gm_b_s, gm_w_p, cv_w_dw, cv_b_dw, cv_ln_g, cv_ln_b, cv_w_pw, mla_q_norm_g, mla_kv_norm_g, mla_w_uq, mla_w_uk, mla_w_uv, mla_w_o, w_out, ln1_g, ln1_b, ln2_g, ln2_b, ffn_w1, ffn_w3, ffn_w2, moe_router, moe_w1, moe_w3, moe_w2):
    B, S, D = x_prompt.shape
    NB, L, _ = x_sample.shape
    depth = w_in.shape[0]
    past = cache_mla_latent.shape[2]
    GW = gm_ln_g.shape[1]
    CW = cv_ln_g.shape[1]
    RQ = mla_q_norm_g.shape[1]
    RKV = mla_kv_norm_g.shape[1]
    H = N_HEADS
    n_experts = moe_router.shape[2]
    alpha = float((2 * depth) ** 0.25)
    TP, TS = B * S, NB * L
    T = TP + TS
    off_cv = 2 * GW
    off_q = off_cv + 2 * CW
    off_kv = off_q + RQ
    off_kr = off_kv + RKV
    off_g = off_kr + QK_ROPE
    assert L <= GM_CHUNK and GM_CHUNK % L == 0 and S % GM_CHUNK == 0 and TS % GM_CHUNK == 0
    assert L >= CV_KERNEL - 1 and L % 8 == 0 and CV_KERNEL - 1 <= CONV_HALO

    w_in_t = jnp.swapaxes(w_in, 1, 2).astype(BF16)
    wkr_t = w_in_t[:, off_kr:off_g, :]
    half_r = QK_ROPE // 2
    wkr2_t = jnp.concatenate([wkr_t, -wkr_t[:, half_r:], wkr_t[:, :half_r]], axis=1)
    wg_t = w_in_t[:, off_g:, :]
    gm_w_p_b, cv_w_pw_b = gm_w_p.astype(BF16), cv_w_pw.astype(BF16)
    w_o_b, w_out_b = mla_w_o.astype(BF16), w_out.astype(BF16)
    ffn_w1_b, ffn_w3_b, ffn_w2_b = (w.astype(BF16)[:, None] for w in (ffn_w1, ffn_w3, ffn_w2))
    moe_w1_b, moe_w3_b, moe_w2_b = moe_w1.astype(BF16), moe_w3.astype(BF16), moe_w2.astype(BF16)
    zero_uv = jnp.zeros((1, off_cv), F32)

    uq = mla_w_uq.reshape(depth, RQ, H, QK_NOPE + QK_ROPE)
    uq_rope = uq[..., QK_NOPE:]
    wq3 = jnp.concatenate([uq[..., :QK_NOPE], uq_rope, _rot_half_cols(uq_rope)], axis=-1)
    wq3 = jnp.transpose(wq3, (0, 2, 1, 3)).astype(BF16)
    wkv3 = jnp.concatenate([mla_w_uk.reshape(depth, RKV, H, QK_NOPE),
                            mla_w_uv.reshape(depth, RKV, H, V_DIM)], axis=-1)
    wkv3 = jnp.transpose(wkv3, (0, 2, 1, 3)).astype(BF16)
    wq3t = jnp.swapaxes(wq3, 2, 3)

    causal = jnp.tril(jnp.ones((GM_CHUNK, GM_CHUNK), bool))
    ws_p = jnp.where(causal, gm_w_s, 0.0)
    reps = GM_CHUNK // L
    ws_l = jnp.where(causal[:L, :L], gm_w_s[:, :, :L, :L], 0.0)
    ws_s = jnp.einsum("ab,lgij->lgaibj", jnp.eye(reps, dtype=F32), ws_l).reshape(depth, GM_GROUPS, GM_CHUNK, GM_CHUNK)
    ws2 = jnp.stack([ws_p, ws_s], axis=1).astype(BF16)
    gd = GW // GM_GROUPS
    bs_p = jnp.repeat(jnp.transpose(gm_b_s, (0, 2, 1)), gd, axis=2)
    bs_s = jnp.repeat(jnp.tile(jnp.transpose(gm_b_s[:, :, :L], (0, 2, 1)), (1, reps, 1)), gd, axis=2)
    bs2 = jnp.stack([bs_p, bs_s], axis=1)

    router_pad = jnp.pad(moe_router, ((0, 0), (0, 0), (0, LANES - n_experts)))

    half = QK_ROPE // 2
    inv = ROPE_THETA ** (-jnp.arange(half, dtype=F32) / half)
    pos = jnp.concatenate([jnp.tile(jnp.arange(S), B), jnp.tile(past + jnp.arange(L), NB)]).astype(F32)
    ang = pos[:, None] * inv[None, :]
    cos = jnp.tile(jnp.cos(ang), (1, 2))
    sin = jnp.tile(jnp.sin(ang), (1, 2))
    cos_t, sin_t = cos[:TP].T, sin[:TP].T
    scale_log2e = ATTN_SCALE * math.log2(math.e)

    cache_kr_t = jnp.swapaxes(cache_mla_krope, 2, 3)
    hist_s = jnp.pad(cache_conv, ((0, 0), (0, 0), (CONV_HALO - (CV_KERNEL - 1), 0), (0, 0)))

    x = jnp.concatenate([x_prompt.reshape(TP, D), x_sample.reshape(TS, D)], axis=0)
    xb = x.astype(BF16)
    row2 = lambda a: a.reshape(1, -1)

    tm_dense = _tile(T, 512)
    eid_dense = jnp.zeros((T // tm_dense,), jnp.int32)
    nu_dense = jnp.full((1,), T // tm_dense, jnp.int32)

    outs = {k: [] for k in ("lat_p", "kr_p", "conv_p", "lat_s", "kr_s", "conv_s", "v_s")}
    for l in range(depth):
        uvg = _mm_act(xb, w_in_t, l, 0, off_cv, zero_uv, "gelu", F32)
        glu = _mm_glu(xb, w_in_t, l, off_cv, off_cv + CW, CW)
        cqn, ckv, kr = _latents(xb, w_in_t, wkr2_t, l, off_q, off_kv, row2(mla_q_norm_g[l]),
                                row2(mla_kv_norm_g[l]), cos, sin)

        us, v_ln = _gmlp(uvg, ws2[l], bs2[l], row2(gm_ln_g[l]), row2(gm_ln_b[l]), TP)

        cv_args = (cv_w_dw[l], row2(cv_b_dw[l]), row2(cv_ln_g[l]), row2(cv_ln_b[l]))
        yb_p = _conv(glu, glu, 0, B, S, *cv_args, zero_first=True)
        yb_s = _conv(glu, hist_s[l], TP, NB, L, *cv_args, zero_first=False)

        qt_p = _qproj_t(cqn, wq3t[l], cos_t, sin_t, B, S, scale_log2e)
        q_s = _qproj(cqn, wq3[l], cos, sin, TP, NB, L)
        k_p, vt_p = _kvproj(ckv, kr, wkv3[l], B, S)
        o_p = _attn_prompt(qt_p, k_p, vt_p)
        o_s = _attn_sample(q_s, cache_mla_latent, cache_kr_t, l, ckv, kr, wkv3[l], TP)

        merged = _merge(xb, us, yb_p, yb_s, o_p, o_s, gm_w_p_b, cv_w_pw_b, w_o_b, wg_t, row2(b_gate[l]), l)
        x, xb = _outproj_ln(merged, w_out_b, l, x, row2(ln1_g[l]), row2(ln1_b[l]), alpha)

        j = l // 2
        if l % 2 == 0:
            x, xb = _ffn_dense(xb, eid_dense, nu_dense, ffn_w1_b, ffn_w3_b, ffn_w2_b, j, tm_dense,
                               x, row2(ln2_g[l]), row2(ln2_b[l]), alpha)
        else:
            route = _router(x, router_pad[j], n_experts)
            slot_tok, block_e, n_used, dest = _moe_plan(route, n_experts, MOE_TM)
            y = _ffn_moe(x, slot_tok, block_e, n_used, moe_w1_b, moe_w3_b, moe_w2_b, j, MOE_TM)
            x, xb = _combine(x, y, dest, route, row2(ln2_g[l]), row2(ln2_b[l]), alpha)

        outs["lat_p"].append(ckv[:TP].reshape(B, S, RKV))
        outs["kr_p"].append(kr[:TP].reshape(B, S, QK_ROPE))
        outs["conv_p"].append(jnp.stack([glu[(s + 1) * S - (CV_KERNEL - 1):(s + 1) * S] for s in range(B)]))
        outs["lat_s"].append(ckv[TP:].reshape(NB, L, RKV))
        outs["kr_s"].append(kr[TP:].reshape(NB, L, QK_ROPE))
        outs["conv_s"].append(glu[TP:].reshape(NB, L, CW)[:, L - (CV_KERNEL - 1):])
        outs["v_s"].append(v_ln[TP:].reshape(NB, L, GW))

    return (x[:TP].reshape(B, S, D), x[TP:].reshape(NB, L, D),
            jnp.stack(outs["lat_p"]), jnp.stack(outs["kr_p"]), jnp.stack(outs["conv_p"]),
            jnp.stack(outs["lat_s"]), jnp.stack(outs["kr_s"]), jnp.stack(outs["conv_s"]),
            jnp.stack(outs["v_s"]))
```

```python
import functools
import math

import jax
import jax.numpy as jnp
from jax import lax
from jax.experimental import pallas as pl
from jax.experimental.pallas import tpu as pltpu

F32 = jnp.float32
BF16 = jnp.bfloat16

CHUNK = 64
GM_GROUPS = 8
GM_CHUNK = 128
CV_KERNEL = 31
N_HEADS = 16
QK_NOPE = 128
QK_ROPE = 64
V_DIM = 128
QK_PAD = 256
V_ONES = 16
ROPE_THETA = 10000.0
TOP_K = 2
LN_EPS = 1e-5
RMS_EPS = 1e-6
ATTN_SCALE = (QK_NOPE + QK_ROPE) ** -0.5

V7X_VMEM_LIMIT_BYTES = 56 * 1024 * 1024
LANES = 128
SUBLANES = 8
CONV_HALO = 32
MOE_TM = 512
FFN_CHUNKS = 2
GATHER_UNROLL = 8


def _cp(*sem):
    return pltpu.CompilerParams(dimension_semantics=sem,
                                vmem_limit_bytes=V7X_VMEM_LIMIT_BYTES)


def _tile(n, pref, mult=8):
    if n <= pref:
        return n
    for t in range(pref, 0, -1):
        if n % t == 0 and t % mult == 0:
            return t
    return n


def _ln(x, g, b):
    mu = jnp.mean(x, axis=-1, keepdims=True)
    xc = x - mu
    var = jnp.mean(xc * xc, axis=-1, keepdims=True)
    return xc * lax.rsqrt(var + LN_EPS) * g + b


def _rms(x, g):
    ms = jnp.mean(x * x, axis=-1, keepdims=True)
    return x * lax.rsqrt(ms + RMS_EPS) * g


def _dot(a, b):
    return jnp.dot(a, b, preferred_element_type=F32)


def _dot_nt(a, b):
    return lax.dot_general(a, b, (((1,), (1,)), ((), ())), preferred_element_type=F32)


def _wt_spec(layer, row0, tn, K, jmap):
    assert row0 % tn == 0
    return pl.BlockSpec((1, tn, K), lambda *ij: (layer, row0 // tn + jmap(*ij), 0))


def _mm_act_kernel(x_ref, wt_ref, b_ref, o_ref, *, act):
    acc = _dot_nt(x_ref[...], wt_ref[0]) + b_ref[...]
    if act == "gelu":
        acc = jax.nn.gelu(acc)
    elif act == "sigmoid":
        acc = jax.nn.sigmoid(acc)
    o_ref[...] = acc.astype(o_ref.dtype)


def _mm_act(x, wt, layer, row0, N, b, act, out_dtype, tm=1024, tn=1024):
    M, K = x.shape
    tm, tn = _tile(M, tm), _tile(N, tn, LANES)
    return pl.pallas_call(
        functools.partial(_mm_act_kernel, act=act),
        grid=(M // tm, N // tn),
        in_specs=[pl.BlockSpec((tm, K), lambda i, j: (i, 0)),
                  _wt_spec(layer, row0, tn, K, lambda i, j: j),
                  pl.BlockSpec((1, tn), lambda i, j: (0, j))],
        out_specs=pl.BlockSpec((tm, tn), lambda i, j: (i, j)),
        out_shape=jax.ShapeDtypeStruct((M, N), out_dtype),
        compiler_params=_cp("parallel", "arbitrary"),
        name="mm_" + act,
    )(x, wt, b)


def _mm_glu_kernel(x_ref, wa_ref, wb_ref, o_ref):
    x = x_ref[...]
    o_ref[...] = _dot_nt(x, wa_ref[0]) * jax.nn.sigmoid(_dot_nt(x, wb_ref[0]))


def _mm_glu(x, wt, layer, row_a, row_b, N, tm=1024, tn=512):
    M, K = x.shape
    tm, tn = _tile(M, tm), _tile(N, tn, LANES)
    return pl.pallas_call(
        _mm_glu_kernel,
        grid=(M // tm, N // tn),
        in_specs=[pl.BlockSpec((tm, K), lambda i, j: (i, 0)),
                  _wt_spec(layer, row_a, tn, K, lambda i, j: j),
                  _wt_spec(layer, row_b, tn, K, lambda i, j: j)],
        out_specs=pl.BlockSpec((tm, tn), lambda i, j: (i, j)),
        out_shape=jax.ShapeDtypeStruct((M, N), F32),
        compiler_params=_cp("parallel", "arbitrary"),
        name="mm_glu",
    )(x, wt, wt)


def _latent_kernel(x_ref, wq_ref, wkv_ref, wkr_ref, gq_ref, gkv_ref, cos_ref, sin_ref,
                   cq_ref, ckv_ref, kr_ref):
    x = x_ref[...]
    cq_ref[...] = _rms(_dot_nt(x, wq_ref[0]), gq_ref[...]).astype(cq_ref.dtype)
    ckv_ref[...] = _rms(_dot_nt(x, wkv_ref[0]), gkv_ref[...])
    r = _dot_nt(x, wkr_ref[0])
    kr_ref[...] = r[:, :QK_ROPE] * cos_ref[...] + r[:, QK_ROPE:] * sin_ref[...]


def _latents(x, wt, wkr2t, layer, row_q, row_kv, gq, gkv, cos, sin, tm=512):
    M, K = x.shape
    Rq, Rkv = gq.shape[1], gkv.shape[1]
    tm = _tile(M, tm)
    row = lambda i: (i, 0)
    full = lambda i: (0, 0)
    return pl.pallas_call(
        _latent_kernel,
        grid=(M // tm,),
        in_specs=[pl.BlockSpec((tm, K), row),
                  _wt_spec(layer, row_q, Rq, K, lambda i: 0), _wt_spec(layer, row_kv, Rkv, K, lambda i: 0),
                  pl.BlockSpec((1, 2 * QK_ROPE, K), lambda i: (layer, 0, 0)),
                  pl.BlockSpec((1, Rq), full), pl.BlockSpec((1, Rkv), full),
                  pl.BlockSpec((tm, QK_ROPE), row), pl.BlockSpec((tm, QK_ROPE), row)],
        out_specs=[pl.BlockSpec((tm, Rq), row), pl.BlockSpec((tm, Rkv), row),
                   pl.BlockSpec((tm, QK_ROPE), row)],
        out_shape=[jax.ShapeDtypeStruct((M, Rq), BF16), jax.ShapeDtypeStruct((M, Rkv), F32),
                   jax.ShapeDtypeStruct((M, QK_ROPE), F32)],
        compiler_params=_cp("parallel"),
        name="latents",
    )(x, wt, wt, wkr2t, gq, gkv, cos, sin)


def _gmlp_kernel(uv_ref, ws_ref, bs_ref, g_ref, b_ref, us_ref, v_ref, *, n_chunks, groups):
    W = v_ref.shape[1]
    gd = W // groups
    vn = _ln(uv_ref[:, W:], g_ref[...], b_ref[...])
    v_ref[...] = vn
    vb = vn.astype(BF16)
    for c in range(n_chunks):
        r0 = c * GM_CHUNK
        for g in range(groups):
            c0 = g * gd
            s = _dot(ws_ref[0, g], vb[r0:r0 + GM_CHUNK, c0:c0 + gd]) + bs_ref[0, :, c0:c0 + gd]
            u = uv_ref[r0:r0 + GM_CHUNK, c0:c0 + gd]
            us_ref[r0:r0 + GM_CHUNK, c0:c0 + gd] = (u * s).astype(us_ref.dtype)


def _gmlp(uvg, ws2, bs2, g, b, n_prompt_rows, tm=512):
    M, W2 = uvg.shape
    W = W2 // 2
    tm = _tile(math.gcd(n_prompt_rows, M - n_prompt_rows), tm, GM_CHUNK)
    npt = n_prompt_rows // tm
    sel = lambda i: (jnp.minimum(i // npt, 1), 0, 0, 0)
    sel3 = lambda i: (jnp.minimum(i // npt, 1), 0, 0)
    row = lambda i: (i, 0)
    full = lambda i: (0, 0)
    return pl.pallas_call(
        functools.partial(_gmlp_kernel, n_chunks=tm // GM_CHUNK, groups=GM_GROUPS),
        grid=(M // tm,),
        in_specs=[pl.BlockSpec((tm, W2), row),
                  pl.BlockSpec((1, GM_GROUPS, GM_CHUNK, GM_CHUNK), sel),
                  pl.BlockSpec((1, GM_CHUNK, W), sel3),
                  pl.BlockSpec((1, W), full), pl.BlockSpec((1, W), full)],
        out_specs=[pl.BlockSpec((tm, W), row), pl.BlockSpec((tm, W), row)],
        out_shape=[jax.ShapeDtypeStruct((M, W), BF16), jax.ShapeDtypeStruct((M, W), F32)],
        compiler_params=_cp("parallel"),
        name="gmlp",
    )(uvg, ws2, bs2, g, b)


def _conv_kernel(cur_ref, halo_ref, w_ref, bdw_ref, g_ref, b_ref, o_ref, full_ref, sh_ref, acc_ref,
                 *, tm, zero_first):
    C = cur_ref.shape[1]
    halo = halo_ref[...].reshape(CONV_HALO, C)
    if zero_first:
        halo = jnp.where(pl.program_id(1) == 0, 0.0, halo)
    full_ref[0:CONV_HALO, :] = halo
    full_ref[CONV_HALO:CONV_HALO + tm, :] = cur_ref[...]
    off = CONV_HALO - (CV_KERNEL - 1)
    ns = sh_ref.shape[1]

    def lane_block(c, carry):
        c0 = pl.multiple_of(c * LANES, LANES)
        sh_ref[0] = full_ref[:, pl.ds(c0, LANES)]
        for r in range(1, SUBLANES):
            sh_ref[r, :ns - SUBLANES, :] = full_ref[pl.ds(r, ns - SUBLANES), pl.ds(c0, LANES)]
        wk = w_ref[:, pl.ds(c0, LANES)]
        wrows = [jnp.broadcast_to(wk[k:k + 1, :], (SUBLANES, LANES)) for k in range(CV_KERNEL)]
        for rb in range(tm // SUBLANES):
            acc = jnp.zeros((SUBLANES, LANES), F32)
            for k in range(CV_KERNEL):
                a, r = divmod(off + k, SUBLANES)
                acc = acc + sh_ref[r, pl.ds((rb + a) * SUBLANES, SUBLANES), :] * wrows[k]
            acc_ref[pl.ds(rb * SUBLANES, SUBLANES), pl.ds(c0, LANES)] = acc
        return carry

    lax.fori_loop(0, C // LANES, lane_block, 0)
    y = _ln(acc_ref[...] + bdw_ref[...], g_ref[...], b_ref[...])
    o_ref[...] = (y * jax.nn.sigmoid(y)).astype(o_ref.dtype)


def _conv(glu, halo_src, row0, n_seq, L, w_dw, b_dw, g, b, zero_first, tm=256):
    C = glu.shape[1]
    tm = _tile(L, tm, CONV_HALO)
    lt = L // tm
    hb = tm // CONV_HALO
    assert row0 % tm == 0
    r0 = row0 // tm
    if zero_first:
        halo_spec = pl.BlockSpec((CONV_HALO, C), lambda s, i: (jnp.maximum((r0 + s * lt + i) * hb - 1, 0), 0))
    else:
        halo_spec = pl.BlockSpec((1, CONV_HALO, C), lambda s, i: (s, 0, 0))
    full = lambda s, i: (0, 0)
    return pl.pallas_call(
        functools.partial(_conv_kernel, tm=tm, zero_first=zero_first),
        grid=(n_seq, lt),
        in_specs=[pl.BlockSpec((tm, C), lambda s, i: (r0 + s * lt + i, 0)),
                  halo_spec,
                  pl.BlockSpec((CV_KERNEL, C), full),
                  pl.BlockSpec((1, C), full), pl.BlockSpec((1, C), full), pl.BlockSpec((1, C), full)],
        out_specs=pl.BlockSpec((tm, C), lambda s, i: (s * lt + i, 0)),
        out_shape=jax.ShapeDtypeStruct((n_seq * L, C), BF16),
        scratch_shapes=[pltpu.VMEM((CONV_HALO + tm, C), F32),
                        pltpu.VMEM((SUBLANES, CONV_HALO + tm, LANES), F32),
                        pltpu.VMEM((tm, C), F32)],
        compiler_params=_cp("parallel", "arbitrary"),
        name="conv_prompt" if zero_first else "conv_sample",
    )(glu, halo_src, w_dw, b_dw, g, b)


def _qproj_kernel(cq_ref, w_ref, cos_ref, sin_ref, q_ref, *, hg, nb, L):
    cq = cq_ref[...]
    cos, sin = cos_ref[...], sin_ref[...]
    for h in range(hg):
        r = _dot(cq, w_ref[h])
        nope = r[:, :QK_NOPE] * ATTN_SCALE
        rp = (r[:, QK_NOPE:QK_NOPE + QK_ROPE] * cos + r[:, QK_NOPE + QK_ROPE:] * sin) * ATTN_SCALE
        if nb == 1:
            q_ref[0, h, :, :QK_NOPE] = nope.astype(q_ref.dtype)
            q_ref[0, h, :, QK_NOPE:] = rp.astype(q_ref.dtype)
        else:
            q_ref[:, h, :, :QK_NOPE] = nope.reshape(nb, L, QK_NOPE).astype(q_ref.dtype)
            q_ref[:, h, :, QK_NOPE:] = rp.reshape(nb, L, QK_ROPE).astype(q_ref.dtype)


def _qproj(cqn, wq3, cos, sin, row0, n_seq, L, hg=4, tm=512):
    R = cqn.shape[1]
    H = wq3.shape[0]
    hg = _tile(H, hg, 1)
    dq = QK_NOPE + QK_ROPE
    if L >= tm:
        tm = _tile(L, tm)
        nb, lt = 1, L // tm
        out_spec = pl.BlockSpec((1, hg, tm, dq), lambda i, h: (i // lt, h, i % lt, 0))
    else:
        nb = _tile(n_seq, max(tm // L, 1), 1)
        tm = nb * L
        out_spec = pl.BlockSpec((nb, hg, L, dq), lambda i, h: (i, h, 0, 0))
    assert row0 % tm == 0
    r0 = row0 // tm
    row = lambda i, h: (r0 + i, 0)
    return pl.pallas_call(
        functools.partial(_qproj_kernel, hg=hg, nb=nb, L=L),
        grid=(n_seq * L // tm, H // hg),
        in_specs=[pl.BlockSpec((tm, R), row),
                  pl.BlockSpec((hg, R, wq3.shape[2]), lambda i, h: (h, 0, 0)),
                  pl.BlockSpec((tm, QK_ROPE), row), pl.BlockSpec((tm, QK_ROPE), row)],
        out_specs=out_spec,
        out_shape=jax.ShapeDtypeStruct((n_seq, H, L, dq), BF16),
        compiler_params=_cp("parallel", "arbitrary"),
        name="qproj",
    )(cqn, wq3, cos, sin)


def _qproj_t_kernel(cq_ref, w_ref, cos_ref, sin_ref, q_ref, *, hg, scale):
    cq = cq_ref[...]
    cos, sin = cos_ref[...], sin_ref[...]
    for h in range(hg):
        r = _dot_nt(w_ref[h], cq)
        rp = r[QK_NOPE:QK_NOPE + QK_ROPE] * cos + r[QK_NOPE + QK_ROPE:] * sin
        q_ref[0, h, :QK_NOPE, :] = (r[:QK_NOPE] * scale).astype(q_ref.dtype)
        q_ref[0, h, QK_NOPE:QK_NOPE + QK_ROPE, :] = (rp * scale).astype(q_ref.dtype)
        q_ref[0, h, QK_NOPE + QK_ROPE:, :] = jnp.zeros((QK_PAD - QK_NOPE - QK_ROPE, cq.shape[0]), q_ref.dtype)


def _qproj_t(cqn, wq3t, cos_t, sin_t, n_seq, L, scale, hg=8, tm=512):
    R = cqn.shape[1]
    H = wq3t.shape[0]
    hg = _tile(H, hg, 1)
    tm = _tile(L, tm, LANES)
    lt = L // tm
    dq = QK_PAD
    return pl.pallas_call(
        functools.partial(_qproj_t_kernel, hg=hg, scale=scale),
        grid=(n_seq * lt, H // hg),
        in_specs=[pl.BlockSpec((tm, R), lambda i, h: (i, 0)),
                  pl.BlockSpec((hg, wq3t.shape[1], R), lambda i, h: (h, 0, 0)),
                  pl.BlockSpec((QK_ROPE, tm), lambda i, h: (0, i)),
                  pl.BlockSpec((QK_ROPE, tm), lambda i, h: (0, i))],
        out_specs=pl.BlockSpec((1, hg, dq, tm), lambda i, h: (i // lt, h, 0, i % lt)),
        out_shape=jax.ShapeDtypeStruct((n_seq, H, dq, L), BF16),
        compiler_params=_cp("parallel", "arbitrary"),
        name="qproj_t",
    )(cqn, wq3t, cos_t, sin_t)


def _kvproj_kernel(ckv_ref, kr_ref, w_ref, k_ref, vt_ref, *, hg):
    ckv = ckv_ref[...].astype(BF16)
    kr = kr_ref[...].astype(BF16)
    for h in range(hg):
        r = _dot(ckv, w_ref[h])
        k_ref[0, h, :, :QK_NOPE] = r[:, :QK_NOPE].astype(k_ref.dtype)
        k_ref[0, h, :, QK_NOPE:QK_NOPE + QK_ROPE] = kr
        k_ref[0, h, :, QK_NOPE + QK_ROPE:] = jnp.zeros((kr.shape[0], QK_PAD - QK_NOPE - QK_ROPE), k_ref.dtype)
        vt_ref[0, h, :V_DIM, :] = r[:, QK_NOPE:].T.astype(vt_ref.dtype)
        vt_ref[0, h, V_DIM:, :] = jnp.ones((V_ONES, ckv.shape[0]), vt_ref.dtype)


def _kvproj(ckv, kr, wkv3, n_seq, L, hg=8, tm=512):
    R = ckv.shape[1]
    H = wkv3.shape[0]
    hg = _tile(H, hg, 1)
    tm = _tile(L, tm, LANES)
    lt = L // tm
    dq = QK_PAD
    row = lambda i, h: (i, 0)
    wmap = lambda i, h: (h, 0, 0)
    return pl.pallas_call(
        functools.partial(_kvproj_kernel, hg=hg),
        grid=(n_seq * lt, H // hg),
        in_specs=[pl.BlockSpec((tm, R), row), pl.BlockSpec((tm, QK_ROPE), row),
                  pl.BlockSpec((hg, R, wkv3.shape[2]), wmap)],
        out_specs=[pl.BlockSpec((1, hg, tm, dq), lambda i, h: (i // lt, h, i % lt, 0)),
                   pl.BlockSpec((1, hg, V_DIM + V_ONES, tm), lambda i, h: (i // lt, h, 0, i % lt))],
        out_shape=[jax.ShapeDtypeStruct((n_seq, H, L, dq), BF16),
                   jax.ShapeDtypeStruct((n_seq, H, V_DIM + V_ONES, L), BF16)],
        compiler_params=_cp("parallel", "arbitrary"),
        name="kvproj",
    )(ckv, kr, wkv3)


def _attn_prompt_kernel(qt_ref, k_ref, vt_ref, o_ref, s_ref, p_ref, acc_ref, *, tq, tk, hg):
    qi = pl.program_id(2)
    d0 = pl.multiple_of(qi * tq, tq)

    def scores(g, j0, slot):
        s = _dot(k_ref[0, g, pl.ds(j0, tk), :], qt_ref[0, g])
        s_ref[slot, g] = s
        return jnp.max(s, axis=0, keepdims=True)

    def softmax(g, slot, mx, m, mask=None):
        s = s_ref[slot, g]
        if mask is not None:
            s = jnp.where(mask, s, -jnp.inf)
            mx = jnp.max(s, axis=0, keepdims=True)
        m_new = jnp.maximum(m, mx)
        p_ref[slot, g] = jnp.exp2(s - m_new).astype(BF16)
        return jnp.exp2(m - m_new), m_new

    def accumulate(g, j0, slot, alpha):
        acc_ref[g] = alpha * acc_ref[g] + _dot(vt_ref[0, g, :, pl.ds(j0, tk)], p_ref[slot, g])

    def step(j0, slot, state, mask=None, last=False):
        jp = pl.multiple_of(jnp.maximum(j0 - tk, 0), tk)
        new = []
        for g in range(hg):
            mx, a_prev, m = state[g]
            mx_next = mx if last else scores(g, pl.multiple_of(j0 + tk, tk), 1 - slot)
            accumulate(g, jp, 1 - slot, a_prev)
            alpha, m = softmax(g, slot, mx, m, mask)
            new.append((mx_next, alpha, m))
        return tuple(new)

    def pair(i, state):
        j0 = pl.multiple_of(2 * i * tk, tk)
        state = step(j0, 0, state)
        return step(pl.multiple_of(j0 + tk, tk), 1, state)

    p_ref[1] = jnp.zeros_like(p_ref[1])
    acc_ref[...] = jnp.zeros_like(acc_ref)
    init = tuple((scores(g, 0, 0), jnp.ones((1, tq), F32), jnp.full((1, tq), -jnp.inf, F32))
                 for g in range(hg))
    n_pairs = qi * (tq // (2 * tk))
    odd = lax.rem(n_pairs, 2)
    state = lax.cond(odd == 1, lambda st: pair(0, st), lambda st: st, init)
    state = lax.fori_loop(0, n_pairs // 2, lambda i, st: pair(2 * i + odd + 1, pair(2 * i + odd, st)), state)

    key_chunk = lax.broadcasted_iota(jnp.int32, (tk, tq), 0) // CHUNK
    qry_chunk = lax.broadcasted_iota(jnp.int32, (tk, tq), 1) // CHUNK
    state = step(d0, 0, state, mask=key_chunk <= qry_chunk)
    state = step(pl.multiple_of(d0 + tk, tk), 1, state, mask=key_chunk + tk // CHUNK <= qry_chunk, last=True)
    for g in range(hg):
        accumulate(g, pl.multiple_of(d0 + tk, tk), 1, state[g][1])
        o = acc_ref[g, :V_DIM, :] / acc_ref[g, V_DIM:V_DIM + 1, :]
        o_ref[:, g * V_DIM:(g + 1) * V_DIM] = o.T.astype(o_ref.dtype)


def _attn_prompt(qt, k, vt, tq=512, hg=2):
    B, H, dq, S = qt.shape
    tq = _tile(S, tq, 2 * LANES)
    tk = tq // 2
    hg = _tile(H, hg, 1)
    return pl.pallas_call(
        functools.partial(_attn_prompt_kernel, tq=tq, tk=tk, hg=hg),
        grid=(B, H // hg, S // tq),
        in_specs=[pl.BlockSpec((1, hg, dq, tq), lambda b, h, i: (b, h, 0, i)),
                  pl.BlockSpec((1, hg, S, dq), lambda b, h, i: (b, h, 0, 0)),
                  pl.BlockSpec((1, hg, vt.shape[2], S), lambda b, h, i: (b, h, 0, 0))],
        out_specs=pl.BlockSpec((tq, hg * V_DIM), lambda b, h, i: (b * (S // tq) + i, h)),
        out_shape=jax.ShapeDtypeStruct((B * S, H * V_DIM), BF16),
        scratch_shapes=[pltpu.VMEM((2, hg, tk, tq), F32), pltpu.VMEM((2, hg, tk, tq), BF16),
                        pltpu.VMEM((hg, vt.shape[2], tq), F32)],
        compiler_params=_cp("parallel", "parallel", "arbitrary"),
        name="attn_prompt",
    )(qt, k, vt)


def _attn_sample_kernel(q_ref, plat_ref, pkrt_ref, nlat_ref, nkr_ref, w_ref, o_ref, ql_ref, qr_ref,
                        *, H, L, kc):
    P = plat_ref.shape[2]
    for h in range(H):
        qh = q_ref[0, h]
        ql_ref[h * L:(h + 1) * L, :] = _dot_nt(qh[:, :QK_NOPE], w_ref[h, :, :QK_NOPE]).astype(BF16)
        qr_ref[h * L:(h + 1) * L, :] = qh[:, QK_NOPE:]
    ql = ql_ref[...]
    qr = qr_ref[...]

    def step(lat, s_rope, carry):
        m, l, acc = carry
        s = _dot_nt(ql, lat) + s_rope
        m_new = jnp.maximum(m, jnp.max(s, axis=-1, keepdims=True))
        alpha = jnp.exp(m - m_new)
        p = jnp.exp(s - m_new)
        l = alpha * l + jnp.sum(p, axis=-1, keepdims=True)
        acc = alpha * acc + _dot(p.astype(BF16), lat)
        return m_new, l, acc

    R = ql.shape[1]
    carry = (jnp.full((H * L, 1), -jnp.inf, F32), jnp.zeros((H * L, 1), F32), jnp.zeros((H * L, R), F32))
    for c in range(P // kc):
        krt = pkrt_ref[0, 0, :, c * kc:(c + 1) * kc].astype(BF16)
        carry = step(plat_ref[0, 0, c * kc:(c + 1) * kc, :].astype(BF16), _dot(qr, krt), carry)
    m, l, acc = step(nlat_ref[...].astype(BF16), _dot_nt(qr, nkr_ref[...].astype(BF16)), carry)
    ol = (acc / l).astype(BF16)
    for h in range(H):
        o_ref[:, h * V_DIM:(h + 1) * V_DIM] = _dot(ol[h * L:(h + 1) * L, :], w_ref[h, :, QK_NOPE:]).astype(o_ref.dtype)


def _attn_sample(q, cache_lat, cache_kr_t, layer, ckv, kr, wkv3, row0):
    n_req, H, L, dq = q.shape
    P, R = cache_lat.shape[2], cache_lat.shape[3]
    kc = _tile(P, 1024)
    assert row0 % L == 0
    r0 = row0 // L
    return pl.pallas_call(
        functools.partial(_attn_sample_kernel, H=H, L=L, kc=kc),
        grid=(n_req,),
        in_specs=[pl.BlockSpec((1, H, L, dq), lambda b: (b, 0, 0, 0)),
                  pl.BlockSpec((1, 1, P, R), lambda b: (layer, b, 0, 0)),
                  pl.BlockSpec((1, 1, QK_ROPE, P), lambda b: (layer, b, 0, 0)),
                  pl.BlockSpec((L, R), lambda b: (r0 + b, 0)),
                  pl.BlockSpec((L, QK_ROPE), lambda b: (r0 + b, 0)),
                  pl.BlockSpec(wkv3.shape, lambda b: (0, 0, 0))],
        out_specs=pl.BlockSpec((L, H * V_DIM), lambda b: (b, 0)),
        out_shape=jax.ShapeDtypeStruct((n_req * L, H * V_DIM), BF16),
        scratch_shapes=[pltpu.VMEM((H * L, R), BF16), pltpu.VMEM((H * L, QK_ROPE), BF16)],
        compiler_params=_cp("parallel"),
        name="attn_sample",
    )(q, cache_lat, cache_kr_t, ckv, kr, wkv3)


def _merge_kernel(x_ref, us_ref, ybp_ref, ybs_ref, op_ref, os_ref, wp_ref, wpw_ref, wo_ref,
                  wga_ref, wgb_ref, wgc_ref, ba_ref, bb_ref, bc_ref, out_ref, *, npt):
    prompt = pl.program_id(0) < npt
    x = x_ref[...]
    yb_in = jnp.where(prompt, ybp_ref[...], ybs_ref[...])
    o_in = jnp.where(prompt, op_ref[...], os_ref[...])
    out = jax.nn.sigmoid(_dot_nt(x, wga_ref[0]) + ba_ref[...]) * _dot(us_ref[...], wp_ref[0])
    out += jax.nn.sigmoid(_dot_nt(x, wgb_ref[0]) + bb_ref[...]) * _dot(yb_in, wpw_ref[0])
    out += jax.nn.sigmoid(_dot_nt(x, wgc_ref[0]) + bc_ref[...]) * _dot(o_in, wo_ref[0])
    out_ref[...] = out.astype(out_ref.dtype)


def _merge(xb, us, yb_p, yb_s, o_p, o_s, wp, wpw, wo, wgt, bg, layer, tm=512, tn=512):
    M = us.shape[0]
    D = wp.shape[2]
    TP, TS = yb_p.shape[0], yb_s.shape[0]
    tm, tn = _tile(math.gcd(TP, TS), tm), _tile(D, tn, LANES)
    npt = TP // tm
    nj = D // tn
    row = lambda i, j: (i, 0)
    prow = lambda i, j: (jnp.minimum(i, npt - 1), 0)
    srow = lambda i, j: (jnp.maximum(i - npt, 0), 0)
    col = lambda i, j: (layer, 0, j)
    gcol = lambda g: (lambda i, j: (0, g * nj + j))
    grow = lambda g: _wt_spec(layer, g * D, tn, wgt.shape[2], lambda i, j: j)
    return pl.pallas_call(
        functools.partial(_merge_kernel, npt=npt),
        grid=(M // tm, nj),
        in_specs=[pl.BlockSpec((tm, xb.shape[1]), row), pl.BlockSpec((tm, us.shape[1]), row),
                  pl.BlockSpec((tm, yb_p.shape[1]), prow), pl.BlockSpec((tm, yb_s.shape[1]), srow),
                  pl.BlockSpec((tm, o_p.shape[1]), prow), pl.BlockSpec((tm, o_s.shape[1]), srow),
                  pl.BlockSpec((1, wp.shape[1], tn), col), pl.BlockSpec((1, wpw.shape[1], tn), col),
                  pl.BlockSpec((1, wo.shape[1], tn), col),
                  grow(0), grow(1), grow(2),
                  pl.BlockSpec((1, tn), gcol(0)), pl.BlockSpec((1, tn), gcol(1)), pl.BlockSpec((1, tn), gcol(2))],
        out_specs=pl.BlockSpec((tm, tn), lambda i, j: (i, j)),
        out_shape=jax.ShapeDtypeStruct((M, D), BF16),
        compiler_params=_cp("parallel", "arbitrary"),
        name="merge",
    )(xb, us, yb_p, yb_s, o_p, o_s, wp, wpw, wo, wgt, wgt, wgt, bg, bg, bg)


def _outproj_kernel(m_ref, w_ref, x_ref, g_ref, b_ref, o_ref, ob_ref, *, alpha):
    y = _ln(alpha * x_ref[...] + _dot(m_ref[...], w_ref[0]), g_ref[...], b_ref[...])
    o_ref[...] = y
    ob_ref[...] = y.astype(ob_ref.dtype)


def _outproj_ln(merged, w, layer, x, g, b, alpha, tm=512):
    M, D = x.shape
    tm = _tile(M, tm)
    row = lambda i: (i, 0)
    full = lambda i: (0, 0)
    return pl.pallas_call(
        functools.partial(_outproj_kernel, alpha=alpha),
        grid=(M // tm,),
        in_specs=[pl.BlockSpec((tm, merged.shape[1]), row),
                  pl.BlockSpec((1,) + w.shape[1:], lambda i: (layer, 0, 0)),
                  pl.BlockSpec((tm, D), row), pl.BlockSpec((1, D), full), pl.BlockSpec((1, D), full)],
        out_specs=[pl.BlockSpec((tm, D), row), pl.BlockSpec((tm, D), row)],
        out_shape=[jax.ShapeDtypeStruct((M, D), F32), jax.ShapeDtypeStruct((M, D), BF16)],
        compiler_params=_cp("parallel"),
        name="outproj_ln",
    )(merged, w, x, g, b)


def _swiglu_partial(x, w1_ref, w3_ref, w2_ref):
    c = w1_ref.shape[3] // FFN_CHUNKS
    hs = []
    for k in range(FFN_CHUNKS):
        cols = slice(k * c, (k + 1) * c)
        h = jax.nn.silu(_dot(x, w1_ref[0, 0, :, cols])) * _dot(x, w3_ref[0, 0, :, cols])
        hs.append(h.astype(BF16))
    out = _dot(hs[0], w2_ref[0, 0, 0:c, :])
    for k in range(1, FFN_CHUNKS):
        out += _dot(hs[k], w2_ref[0, 0, k * c:(k + 1) * c, :])
    return out


def _ffn_dense_kernel(eid_ref, nu_ref, x_ref, w1_ref, w3_ref, w2_ref, r_ref, g_ref, b_ref,
                      o_ref, ob_ref, acc_ref, *, alpha):
    f = pl.program_id(1)

    @pl.when(f == 0)
    def _():
        acc_ref[...] = jnp.zeros_like(acc_ref)

    acc_ref[...] += _swiglu_partial(x_ref[...], w1_ref, w3_ref, w2_ref)

    @pl.when(f == pl.num_programs(1) - 1)
    def _():
        y = _ln(alpha * r_ref[...] + acc_ref[...], g_ref[...], b_ref[...])
        o_ref[...] = y
        ob_ref[...] = y.astype(ob_ref.dtype)


def _ffn_moe_kernel(eid_ref, nu_ref, idx0_ref, idxn_ref, x_hbm, w1_ref, w3_ref, w2_ref, o_ref,
                    xg_ref, xb_ref, sem, *, tm):
    i = pl.program_id(0)
    f = pl.program_id(1)
    n_used = nu_ref[0]
    slot = lax.rem(i, 2)

    def row_copy(t, r, s):
        return pltpu.make_async_copy(x_hbm.at[pl.ds(t, 1), :], xg_ref.at[s, pl.ds(r, 1), :], sem.at[s])

    def start_rows(idx_ref, s):
        def body(c, carry):
            for u in range(GATHER_UNROLL):
                r = c * GATHER_UNROLL + u
                row_copy(idx_ref[0, 0, r], r, s).start(priority=u % 2)
            return carry
        lax.fori_loop(0, tm // GATHER_UNROLL, body, 0)

    def wait_rows(s):
        def body(c, carry):
            for u in range(GATHER_UNROLL):
                row_copy(0, c * GATHER_UNROLL + u, s).wait()
            return carry
        lax.fori_loop(0, tm // GATHER_UNROLL, body, 0)

    @pl.when(f == 0)
    def _():
        o_ref[...] = jnp.zeros_like(o_ref)

    @pl.when((f == 0) & (i < n_used))
    def _():
        @pl.when(i == 0)
        def _():
            start_rows(idx0_ref, 0)

        wait_rows(slot)
        xb_ref[...] = xg_ref[slot].astype(BF16)

        @pl.when(i + 1 < n_used)
        def _():
            start_rows(idxn_ref, 1 - slot)

    @pl.when(i < n_used)
    def _():
        o_ref[...] += _swiglu_partial(xb_ref[...], w1_ref, w3_ref, w2_ref)


def _ffn_weight_specs(layer, D, tf):
    last = lambda i, nu: jnp.minimum(i, nu[0] - 1)
    w13 = lambda i, f, e, nu: (layer, e[last(i, nu)], 0, jnp.where(i < nu[0], f, 0))
    w2m = lambda i, f, e, nu: (layer, e[last(i, nu)], jnp.where(i < nu[0], f, 0), 0)
    return [pl.BlockSpec((1, 1, D, tf), w13), pl.BlockSpec((1, 1, D, tf), w13), pl.BlockSpec((1, 1, tf, D), w2m)]


def _ffn_dense(x, eid, n_used, w1, w3, w2, layer, tm, resid, g, b, alpha, tf=512):
    M, D = x.shape
    Fdim = w1.shape[3]
    tf = _tile(Fdim, tf, LANES)
    row = lambda i, f, e, nu: (i, 0)
    full = lambda i, f, e, nu: (0, 0)
    return pl.pallas_call(
        functools.partial(_ffn_dense_kernel, alpha=alpha),
        grid_spec=pltpu.PrefetchScalarGridSpec(
            num_scalar_prefetch=2, grid=(M // tm, Fdim // tf),
            in_specs=[pl.BlockSpec((tm, D), row)] + _ffn_weight_specs(layer, D, tf)
            + [pl.BlockSpec((tm, D), row), pl.BlockSpec((1, D), full), pl.BlockSpec((1, D), full)],
            out_specs=[pl.BlockSpec((tm, D), row), pl.BlockSpec((tm, D), row)],
            scratch_shapes=[pltpu.VMEM((tm, D), F32)]),
        out_shape=[jax.ShapeDtypeStruct((M, D), F32), jax.ShapeDtypeStruct((M, D), BF16)],
        compiler_params=_cp("arbitrary", "arbitrary"),
        name="ffn_dense",
    )(eid, n_used, x, w1, w3, w2, resid, g, b)


def _ffn_moe(x, slot_tok, eid, n_used, w1, w3, w2, layer, tm, tf=512):
    D = x.shape[1]
    n_slots = slot_tok.shape[0]
    nb = n_slots // tm
    Fdim = w1.shape[3]
    tf = _tile(Fdim, tf, LANES)
    idx3 = slot_tok.reshape(nb, 1, tm)
    smem = functools.partial(pl.BlockSpec, (1, 1, tm), memory_space=pltpu.SMEM)
    return pl.pallas_call(
        functools.partial(_ffn_moe_kernel, tm=tm),
        grid_spec=pltpu.PrefetchScalarGridSpec(
            num_scalar_prefetch=2, grid=(nb, Fdim // tf),
            in_specs=[smem(lambda i, f, e, nu: (0, 0, 0)),
                      smem(lambda i, f, e, nu: (jnp.minimum(i + 1, nb - 1), 0, 0)),
                      pl.BlockSpec(memory_space=pl.ANY)] + _ffn_weight_specs(layer, D, tf),
            out_specs=pl.BlockSpec((tm, D), lambda i, f, e, nu: (i, 0)),
            scratch_shapes=[pltpu.VMEM((2, tm, D), F32), pltpu.VMEM((tm, D), BF16),
                            pltpu.SemaphoreType.DMA((2,))]),
        out_shape=jax.ShapeDtypeStruct((n_slots, D), F32),
        compiler_params=_cp("arbitrary", "arbitrary"),
        name="ffn_moe",
    )(eid, n_used, idx3, idx3, x, w1, w3, w2)


def _router_kernel(x_ref, r_ref, o_ref, *, n_experts):
    logits = jnp.dot(x_ref[...], r_ref[...], preferred_element_type=F32,
                     precision=lax.Precision.HIGHEST)
    col = lax.broadcasted_iota(jnp.int32, logits.shape, 1)
    lg = jnp.where(col < n_experts, logits, -jnp.inf)
    m1 = jnp.max(lg, axis=-1, keepdims=True)
    i1 = jnp.min(jnp.where(lg == m1, col, LANES), axis=-1, keepdims=True)
    lg2 = jnp.where(col == i1, -jnp.inf, lg)
    m2 = jnp.max(lg2, axis=-1, keepdims=True)
    i2 = jnp.min(jnp.where(lg2 == m2, col, LANES), axis=-1, keepdims=True)
    e = jnp.exp(m2 - m1)
    g1 = 1.0 / (1.0 + e)
    g2 = e * g1
    out = jnp.where(col == 0, i1.astype(F32),
                    jnp.where(col == 1, i2.astype(F32),
                              jnp.where(col == 2, g1, jnp.where(col == 3, g2, 0.0))))
    o_ref[...] = out


def _router(x, router_pad, n_experts, tm=512):
    M, D = x.shape
    tm = _tile(M, tm)
    return pl.pallas_call(
        functools.partial(_router_kernel, n_experts=n_experts),
        grid=(M // tm,),
        in_specs=[pl.BlockSpec((tm, D), lambda i: (i, 0)), pl.BlockSpec((D, LANES), lambda i: (0, 0))],
        out_specs=pl.BlockSpec((tm, LANES), lambda i: (i, 0)),
        out_shape=jax.ShapeDtypeStruct((M, LANES), F32),
        compiler_params=_cp("parallel"),
        name="router",
    )(x, router_pad)


def _combine_kernel(idx0_ref, idxn_ref, x_ref, r_ref, g_ref, b_ref, y_hbm, o_ref, ob_ref, yg_ref, sem,
                    *, tm, alpha):
    i = pl.program_id(0)
    slot = lax.rem(i, 2)

    def row_copy(t, k, r, s):
        return pltpu.make_async_copy(y_hbm.at[pl.ds(t, 1), :], yg_ref.at[s, k, pl.ds(r, 1), :], sem.at[s])

    def start_rows(idx_ref, s):
        def body(c, carry):
            for u in range(GATHER_UNROLL):
                r = c * GATHER_UNROLL + u
                for k in range(TOP_K):
                    row_copy(idx_ref[0, k, r], k, r, s).start(priority=(u + k) % 2)
            return carry
        lax.fori_loop(0, tm // GATHER_UNROLL, body, 0)

    def wait_rows(s):
        def body(c, carry):
            for u in range(GATHER_UNROLL):
                for k in range(TOP_K):
                    row_copy(0, k, c * GATHER_UNROLL + u, s).wait()
            return carry
        lax.fori_loop(0, tm // GATHER_UNROLL, body, 0)

    @pl.when(i == 0)
    def _():
        start_rows(idx0_ref, 0)

    wait_rows(slot)

    @pl.when(i + 1 < pl.num_programs(0))
    def _():
        start_rows(idxn_ref, 1 - slot)

    r = r_ref[...]
    y = alpha * x_ref[...] + r[:, 2:3] * yg_ref[slot, 0] + r[:, 3:4] * yg_ref[slot, 1]
    y = _ln(y, g_ref[...], b_ref[...])
    o_ref[...] = y
    ob_ref[...] = y.astype(ob_ref.dtype)


def _combine(x, y, dest, route, g, b, alpha, tm=256):
    M, D = x.shape
    tm = _tile(M, tm)
    nt = M // tm
    idx3 = jnp.transpose(dest.reshape(nt, tm, TOP_K), (0, 2, 1))
    smem = functools.partial(pl.BlockSpec, (1, TOP_K, tm), memory_space=pltpu.SMEM)
    row = lambda i: (i, 0)
    full = lambda i: (0, 0)
    return pl.pallas_call(
        functools.partial(_combine_kernel, tm=tm, alpha=alpha),
        grid=(nt,),
        in_specs=[smem(lambda i: (0, 0, 0)), smem(lambda i: (jnp.minimum(i + 1, nt - 1), 0, 0)),
                  pl.BlockSpec((tm, D), row), pl.BlockSpec((tm, LANES), row),
                  pl.BlockSpec((1, D), full), pl.BlockSpec((1, D), full),
                  pl.BlockSpec(memory_space=pl.ANY)],
        out_specs=[pl.BlockSpec((tm, D), row), pl.BlockSpec((tm, D), row)],
        out_shape=[jax.ShapeDtypeStruct((M, D), F32), jax.ShapeDtypeStruct((M, D), BF16)],
        scratch_shapes=[pltpu.VMEM((2, TOP_K, tm, D), F32), pltpu.SemaphoreType.DMA((2,))],
        compiler_params=_cp("arbitrary"),
        name="combine",
    )(idx3, idx3, x, route, g, b, y)


def _moe_plan(route, n_experts, tm):
    T = route.shape[0]
    n_assign = T * TOP_K
    e_flat = route[:, :TOP_K].astype(jnp.int32).reshape(-1)
    onehot = (e_flat[:, None] == jnp.arange(n_experts, dtype=jnp.int32)[None, :]).astype(jnp.int32)
    csum = jnp.cumsum(onehot, axis=0)
    rank = jnp.take_along_axis(csum, e_flat[:, None], axis=1)[:, 0] - 1
    counts = csum[-1]
    padded = ((counts + tm - 1) // tm) * tm
    pad_end = jnp.cumsum(padded)
    pad_start = pad_end - padded
    dest = (pad_start[e_flat] + rank).astype(jnp.int32)
    n_blocks = -(-n_assign // tm) + n_experts
    n_slots = n_blocks * tm
    slot_tok = jnp.zeros((n_slots,), jnp.int32).at[dest].set(jnp.arange(n_assign, dtype=jnp.int32) // TOP_K)
    block_e = jnp.minimum(jnp.searchsorted(pad_end, jnp.arange(n_blocks, dtype=jnp.int32) * tm, side="right"),
                          n_experts - 1).astype(jnp.int32)
    n_used = (pad_end[-1:] // tm).astype(jnp.int32)
    return slot_tok, block_e, n_used, dest.reshape(T, TOP_K)


def _rot_half_cols(w):
    half = w.shape[-1] // 2
    return jnp.concatenate([-w[..., half:], w[..., :half]], axis=-1)


def kernel(x_prompt, x_sample, cache_mla_latent, cache_mla_krope, cache_conv, w_in, b_gate, gm_ln_g, gm_ln_b, gm_w_s, gm_b_s, gm_w_p, cv_w_dw, cv_b_dw, cv_ln_g, cv_ln_b, cv_w_pw, mla_q_norm_g, mla_kv_norm_g, mla_w_uq, mla_w_uk, mla_w_uv, mla_w_o, w_out, ln1_g, ln1_b, ln2_g, ln2_b, ffn_w1, ffn_w3, ffn_w2, moe_router, moe_w1, moe_w3, moe_w2):
    B, S, D = x_prompt.shape
    NB, L, _ = x_sample.shape
    depth = w_in.shape[0]
    past = cache_mla_latent.shape[2]
    GW = gm_ln_g.shape[1]
    CW = cv_ln_g.shape[1]
    RQ = mla_q_norm_g.shape[1]
    RKV = mla_kv_norm_g.shape[1]
    H = N_HEADS
    n_experts = moe_router.shape[2]
    alpha = float((2 * depth) ** 0.25)
    TP, TS = B * S, NB * L
    T = TP + TS
    off_cv = 2 * GW
    off_q = off_cv + 2 * CW
    off_kv = off_q + RQ
    off_kr = off_kv + RKV
    off_g = off_kr + QK_ROPE
    assert L <= GM_CHUNK and GM_CHUNK % L == 0 and S % GM_CHUNK == 0 and TS % GM_CHUNK == 0
    assert L >= CV_KERNEL - 1 and L % 8 == 0 and CV_KERNEL - 1 <= CONV_HALO

    w_in_t = jnp.swapaxes(w_in, 1, 2).astype(BF16)
    wkr_t = w_in_t[:, off_kr:off_g, :]
    half_r = QK_ROPE // 2
    wkr2_t = jnp.concatenate([wkr_t, -wkr_t[:, half_r:], wkr_t[:, :half_r]], axis=1)
    wg_t = w_in_t[:, off_g:, :]
    gm_w_p_b, cv_w_pw_b = gm_w_p.astype(BF16), cv_w_pw.astype(BF16)
    w_o_b, w_out_b = mla_w_o.astype(BF16), w_out.astype(BF16)
    ffn_w1_b, ffn_w3_b, ffn_w2_b = (w.astype(BF16)[:, None] for w in (ffn_w1, ffn_w3, ffn_w2))
    moe_w1_b, moe_w3_b, moe_w2_b = moe_w1.astype(BF16), moe_w3.astype(BF16), moe_w2.astype(BF16)
    zero_uv = jnp.zeros((1, off_cv), F32)

    uq = mla_w_uq.reshape(depth, RQ, H, QK_NOPE + QK_ROPE)
    uq_rope = uq[..., QK_NOPE:]
    wq3 = jnp.concatenate([uq[..., :QK_NOPE], uq_rope, _rot_half_cols(uq_rope)], axis=-1)
    wq3 = jnp.transpose(wq3, (0, 2, 1, 3)).astype(BF16)
    wkv3 = jnp.concatenate([mla_w_uk.reshape(depth, RKV, H, QK_NOPE),
                            mla_w_uv.reshape(depth, RKV, H, V_DIM)], axis=-1)
    wkv3 = jnp.transpose(wkv3, (0, 2, 1, 3)).astype(BF16)
    wq3t = jnp.swapaxes(wq3, 2, 3)

    causal = jnp.tril(jnp.ones((GM_CHUNK, GM_CHUNK), bool))
    ws_p = jnp.where(causal, gm_w_s, 0.0)
    reps = GM_CHUNK // L
    ws_l = jnp.where(causal[:L, :L], gm_w_s[:, :, :L, :L], 0.0)
    ws_s = jnp.einsum("ab,lgij->lgaibj", jnp.eye(reps, dtype=F32), ws_l).reshape(depth, GM_GROUPS, GM_CHUNK, GM_CHUNK)
    ws2 = jnp.stack([ws_p, ws_s], axis=1).astype(BF16)
    gd = GW // GM_GROUPS
    bs_p = jnp.repeat(jnp.transpose(gm_b_s, (0, 2, 1)), gd, axis=2)
    bs_s = jnp.repeat(jnp.tile(jnp.transpose(gm_b_s[:, :, :L], (0, 2, 1)), (1, reps, 1)), gd, axis=2)
    bs2 = jnp.stack([bs_p, bs_s], axis=1)

    router_pad = jnp.pad(moe_router, ((0, 0), (0, 0), (0, LANES - n_experts)))

    half = QK_ROPE // 2
    inv = ROPE_THETA ** (-jnp.arange(half, dtype=F32) / half)
    pos = jnp.concatenate([jnp.tile(jnp.arange(S), B), jnp.tile(past + jnp.arange(L), NB)]).astype(F32)
    ang = pos[:, None] * inv[None, :]
    cos = jnp.tile(jnp.cos(ang), (1, 2))
    sin = jnp.tile(jnp.sin(ang), (1, 2))
    cos_t, sin_t = cos[:TP].T, sin[:TP].T
    scale_log2e = ATTN_SCALE * math.log2(math.e)

    cache_kr_t = jnp.swapaxes(cache_mla_krope, 2, 3)
    hist_s = jnp.pad(cache_conv, ((0, 0), (0, 0), (CONV_HALO - (CV_KERNEL - 1), 0), (0, 0)))

    x = jnp.concatenate([x_prompt.reshape(TP, D), x_sample.reshape(TS, D)], axis=0)
    xb = x.astype(BF16)
    row2 = lambda a: a.reshape(1, -1)

    tm_dense = _tile(T, 512)
    eid_dense = jnp.zeros((T // tm_dense,), jnp.int32)
    nu_dense = jnp.full((1,), T // tm_dense, jnp.int32)

    outs = {k: [] for k in ("lat_p", "kr_p", "conv_p", "lat_s", "kr_s", "conv_s", "v_s")}
    for l in range(depth):
        uvg = _mm_act(xb, w_in_t, l, 0, off_cv, zero_uv, "gelu", F32)
        glu = _mm_glu(xb, w_in_t, l, off_cv, off_cv + CW, CW)
        cqn, ckv, kr = _latents(xb, w_in_t, wkr2_t, l, off_q, off_kv, row2(mla_q_norm_g[l]),
                                row2(mla_kv_norm_g[l]), cos, sin)

        us, v_ln = _gmlp(uvg, ws2[l], bs2[l], row2(gm_ln_g[l]), row2(gm_ln_b[l]), TP)

        cv_args = (cv_w_dw[l], row2(cv_b_dw[l]), row2(cv_ln_g[l]), row2(cv_ln_b[l]))
        yb_p = _conv(glu, glu, 0, B, S, *cv_args, zero_first=True)
        yb_s = _conv(glu, hist_s[l], TP, NB, L, *cv_args, zero_first=False)

        qt_p = _qproj_t(cqn, wq3t[l], cos_t, sin_t, B, S, scale_log2e)
        q_s = _qproj(cqn, wq3[l], cos, sin, TP, NB, L)
        k_p, vt_p = _kvproj(ckv, kr, wkv3[l], B, S)
        o_p = _attn_prompt(qt_p, k_p, vt_p)
        o_s = _attn_sample(q_s, cache_mla_latent, cache_kr_t, l, ckv, kr, wkv3[l], TP)

        merged = _merge(xb, us, yb_p, yb_s, o_p, o_s, gm_w_p_b, cv_w_pw_b, w_o_b, wg_t, row2(b_gate[l]), l)
        x, xb = _outproj_ln(merged, w_out_b, l, x, row2(ln1_g[l]), row2(ln1_b[l]), alpha)

        j = l // 2
        if l % 2 == 0:
            x, xb = _ffn_dense(xb, eid_dense, nu_dense, ffn_w1_b, ffn_w3_b, ffn_w2_b, j, tm_dense,
                               x, row2(ln2_g[l]), row2(ln2_b[l]), alpha)
        else:
            route = _router(x, router_pad[j], n_experts)
            slot_tok, block_e, n_used, dest = _moe_plan(route, n_experts, MOE_TM)
            y = _ffn_moe(x, slot_tok, block_e, n_used, moe_w1_b, moe_w3_b, moe_w2_b, j, MOE_TM)
            x, xb = _combine(x, y, dest, route, row2(ln2_g[l]), row2(ln2_b[l]), alpha)

        outs["lat_p"].append(ckv[:TP].reshape(B, S, RKV))
        outs["kr_p"].append(kr[:TP].reshape(B, S, QK_ROPE))
        outs["conv_p"].append(jnp.stack([glu[(s + 1) * S - (CV_KERNEL - 1):(s + 1) * S] for s in range(B)]))
        outs["lat_s"].append(ckv[TP:].reshape(NB, L, RKV))
        outs["kr_s"].append(kr[TP:].reshape(NB, L, QK_ROPE))
        outs["conv_s"].append(glu[TP:].reshape(NB, L, CW)[:, L - (CV_KERNEL - 1):])
        outs["v_s"].append(v_ln[TP:].reshape(NB, L, GW))

    return (x[:TP].reshape(B, S, D), x[TP:].reshape(NB, L, D),
            jnp.stack(outs["lat_p"]), jnp.stack(outs["kr_p"]), jnp.stack(outs["conv_p"]),
            jnp.stack(outs["lat_s"]), jnp.stack(outs["kr_s"]), jnp.stack(outs["conv_s"]),
            jnp.stack(outs["v_s"]))
```

```python
import functools
import math

import jax
import jax.numpy as jnp
from jax import lax
from jax.experimental import pallas as pl
from jax.experimental.pallas import tpu as pltpu

F32 = jnp.float32
BF16 = jnp.bfloat16

CHUNK = 64
GM_GROUPS = 8
GM_CHUNK = 128
CV_KERNEL = 31
N_HEADS = 16
QK_NOPE = 128
QK_ROPE = 64
V_DIM = 128
QK_PAD = 256
V_ONES = 16
ROPE_THETA = 10000.0
TOP_K = 2
LN_EPS = 1e-5
RMS_EPS = 1e-6
ATTN_SCALE = (QK_NOPE + QK_ROPE) ** -0.5

V7X_VMEM_LIMIT_BYTES = 56 * 1024 * 1024
LANES = 128
SUBLANES = 8
CONV_HALO = 32
MOE_TM = 512
FFN_CHUNKS = 2
GATHER_UNROLL = 8


def _cp(*sem):
    return pltpu.CompilerParams(dimension_semantics=sem,
                                vmem_limit_bytes=V7X_VMEM_LIMIT_BYTES)


def _tile(n, pref, mult=8):
    if n <= pref:
        return n
    for t in range(pref, 0, -1):
        if n % t == 0 and t % mult == 0:
            return t
    return n


def _ln(x, g, b):
    mu = jnp.mean(x, axis=-1, keepdims=True)
    xc = x - mu
    var = jnp.mean(xc * xc, axis=-1, keepdims=True)
    return xc * lax.rsqrt(var + LN_EPS) * g + b


def _rms(x, g):
    ms = jnp.mean(x * x, axis=-1, keepdims=True)
    return x * lax.rsqrt(ms + RMS_EPS) * g


def _dot(a, b):
    return jnp.dot(a, b, preferred_element_type=F32)


def _dot_nt(a, b):
    return lax.dot_general(a, b, (((1,), (1,)), ((), ())), preferred_element_type=F32)


def _wt_spec(layer, row0, tn, K, jmap):
    assert row0 % tn == 0
    return pl.BlockSpec((1, tn, K), lambda *ij: (layer, row0 // tn + jmap(*ij), 0))


def _mm_act_kernel(x_ref, wt_ref, b_ref, o_ref, *, act):
    acc = _dot_nt(x_ref[...], wt_ref[0]) + b_ref[...]
    if act == "gelu":
        acc = jax.nn.gelu(acc)
    elif act == "sigmoid":
        acc = jax.nn.sigmoid(acc)
    o_ref[...] = acc.astype(o_ref.dtype)


def _mm_act(x, wt, layer, row0, N, b, act, out_dtype, tm=1024, tn=1024):
    M, K = x.shape
    tm, tn = _tile(M, tm), _tile(N, tn, LANES)
    return pl.pallas_call(
        functools.partial(_mm_act_kernel, act=act),
        grid=(M // tm, N // tn),
        in_specs=[pl.BlockSpec((tm, K), lambda i, j: (i, 0)),
                  _wt_spec(layer, row0, tn, K, lambda i, j: j),
                  pl.BlockSpec((1, tn), lambda i, j: (0, j))],
        out_specs=pl.BlockSpec((tm, tn), lambda i, j: (i, j)),
        out_shape=jax.ShapeDtypeStruct((M, N), out_dtype),
        compiler_params=_cp("parallel", "arbitrary"),
        name="mm_" + act,
    )(x, wt, b)


def _mm_glu_kernel(x_ref, wa_ref, wb_ref, o_ref):
    x = x_ref[...]
    o_ref[...] = _dot_nt(x, wa_ref[0]) * jax.nn.sigmoid(_dot_nt(x, wb_ref[0]))


def _mm_glu(x, wt, layer, row_a, row_b, N, tm=1024, tn=512):
    M, K = x.shape
    tm, tn = _tile(M, tm), _tile(N, tn, LANES)
    return pl.pallas_call(
        _mm_glu_kernel,
        grid=(M // tm, N // tn),
        in_specs=[pl.BlockSpec((tm, K), lambda i, j: (i, 0)),
                  _wt_spec(layer, row_a, tn, K, lambda i, j: j),
                  _wt_spec(layer, row_b, tn, K, lambda i, j: j)],
        out_specs=pl.BlockSpec((tm, tn), lambda i, j: (i, j)),
        out_shape=jax.ShapeDtypeStruct((M, N), F32),
        compiler_params=_cp("parallel", "arbitrary"),
        name="mm_glu",
    )(x, wt, wt)


def _latent_kernel(x_ref, wq_ref, wkv_ref, wkr_ref, gq_ref, gkv_ref, cos_ref, sin_ref,
                   cq_ref, ckv_ref, kr_ref):
    x = x_ref[...]
    cq_ref[...] = _rms(_dot_nt(x, wq_ref[0]), gq_ref[...]).astype(cq_ref.dtype)
    ckv_ref[...] = _rms(_dot_nt(x, wkv_ref[0]), gkv_ref[...])
    r = _dot_nt(x, wkr_ref[0])
    kr_ref[...] = r[:, :QK_ROPE] * cos_ref[...] + r[:, QK_ROPE:] * sin_ref[...]


def _latents(x, wt, wkr2t, layer, row_q, row_kv, gq, gkv, cos, sin, tm=512):
    M, K = x.shape
    Rq, Rkv = gq.shape[1], gkv.shape[1]
    tm = _tile(M, tm)
    row = lambda i: (i, 0)
    full = lambda i: (0, 0)
    return pl.pallas_call(
        _latent_kernel,
        grid=(M // tm,),
        in_specs=[pl.BlockSpec((tm, K), row),
                  _wt_spec(layer, row_q, Rq, K, lambda i: 0), _wt_spec(layer, row_kv, Rkv, K, lambda i: 0),
                  pl.BlockSpec((1, 2 * QK_ROPE, K), lambda i: (layer, 0, 0)),
                  pl.BlockSpec((1, Rq), full), pl.BlockSpec((1, Rkv), full),
                  pl.BlockSpec((tm, QK_ROPE), row), pl.BlockSpec((tm, QK_ROPE), row)],
        out_specs=[pl.BlockSpec((tm, Rq), row), pl.BlockSpec((tm, Rkv), row),
                   pl.BlockSpec((tm, QK_ROPE), row)],
        out_shape=[jax.ShapeDtypeStruct((M, Rq), BF16), jax.ShapeDtypeStruct((M, Rkv), F32),
                   jax.ShapeDtypeStruct((M, QK_ROPE), F32)],
        compiler_params=_cp("parallel"),
        name="latents",
    )(x, wt, wt, wkr2t, gq, gkv, cos, sin)


def _gmlp_kernel(uv_ref, ws_ref, bs_ref, g_ref, b_ref, us_ref, v_ref, *, n_chunks, groups):
    W = v_ref.shape[1]
    gd = W // groups
    vn = _ln(uv_ref[:, W:], g_ref[...], b_ref[...])
    v_ref[...] = vn
    vb = vn.astype(BF16)
    for c in range(n_chunks):
        r0 = c * GM_CHUNK
        for g in range(groups):
            c0 = g * gd
            s = _dot(ws_ref[0, g], vb[r0:r0 + GM_CHUNK, c0:c0 + gd]) + bs_ref[0, :, c0:c0 + gd]
            u = uv_ref[r0:r0 + GM_CHUNK, c0:c0 + gd]
            us_ref[r0:r0 + GM_CHUNK, c0:c0 + gd] = (u * s).astype(us_ref.dtype)


def _gmlp(uvg, ws2, bs2, g, b, n_prompt_rows, tm=512):
    M, W2 = uvg.shape
    W = W2 // 2
    tm = _tile(math.gcd(n_prompt_rows, M - n_prompt_rows), tm, GM_CHUNK)
    npt = n_prompt_rows // tm
    sel = lambda i: (jnp.minimum(i // npt, 1), 0, 0, 0)
    sel3 = lambda i: (jnp.minimum(i // npt, 1), 0, 0)
    row = lambda i: (i, 0)
    full = lambda i: (0, 0)
    return pl.pallas_call(
        functools.partial(_gmlp_kernel, n_chunks=tm // GM_CHUNK, groups=GM_GROUPS),
        grid=(M // tm,),
        in_specs=[pl.BlockSpec((tm, W2), row),
                  pl.BlockSpec((1, GM_GROUPS, GM_CHUNK, GM_CHUNK), sel),
                  pl.BlockSpec((1, GM_CHUNK, W), sel3),
                  pl.BlockSpec((1, W), full), pl.BlockSpec((1, W), full)],
        out_specs=[pl.BlockSpec((tm, W), row), pl.BlockSpec((tm, W), row)],
        out_shape=[jax.ShapeDtypeStruct((M, W), BF16), jax.ShapeDtypeStruct((M, W), F32)],
        compiler_params=_cp("parallel"),
        name="gmlp",
    )(uvg, ws2, bs2, g, b)


def _conv_kernel(cur_ref, halo_ref, w_ref, bdw_ref, g_ref, b_ref, o_ref, full_ref, sh_ref, acc_ref,
                 *, tm, zero_first):
    C = cur_ref.shape[1]
    halo = halo_ref[...].reshape(CONV_HALO, C)
    if zero_first:
        halo = jnp.where(pl.program_id(1) == 0, 0.0, halo)
    full_ref[0:CONV_HALO, :] = halo
    full_ref[CONV_HALO:CONV_HALO + tm, :] = cur_ref[...]
    off = CONV_HALO - (CV_KERNEL - 1)
    ns = sh_ref.shape[1]

    def lane_block(c, carry):
        c0 = pl.multiple_of(c * LANES, LANES)
        sh_ref[0] = full_ref[:, pl.ds(c0, LANES)]
        for r in range(1, SUBLANES):
            sh_ref[r, :ns - SUBLANES, :] = full_ref[pl.ds(r, ns - SUBLANES), pl.ds(c0, LANES)]
        wk = w_ref[:, pl.ds(c0, LANES)]
        wrows = [jnp.broadcast_to(wk[k:k + 1, :], (SUBLANES, LANES)) for k in range(CV_KERNEL)]
        for rb in range(tm // SUBLANES):
            acc = jnp.zeros((SUBLANES, LANES), F32)
            for k in range(CV_KERNEL):
                a, r = divmod(off + k, SUBLANES)
                acc = acc + sh_ref[r, pl.ds((rb + a) * SUBLANES, SUBLANES), :] * wrows[k]
            acc_ref[pl.ds(rb * SUBLANES, SUBLANES), pl.ds(c0, LANES)] = acc
        return carry

    lax.fori_loop(0, C // LANES, lane_block, 0)
    y = _ln(acc_ref[...] + bdw_ref[...], g_ref[...], b_ref[...])
    o_ref[...] = (y * jax.nn.sigmoid(y)).astype(o_ref.dtype)


def _conv(glu, halo_src, row0, n_seq, L, w_dw, b_dw, g, b, zero_first, tm=256):
    C = glu.shape[1]
    tm = _tile(L, tm, CONV_HALO)
    lt = L // tm
    hb = tm // CONV_HALO
    assert row0 % tm == 0
    r0 = row0 // tm
    if zero_first:
        halo_spec = pl.BlockSpec((CONV_HALO, C), lambda s, i: (jnp.maximum((r0 + s * lt + i) * hb - 1, 0), 0))
    else:
        halo_spec = pl.BlockSpec((1, CONV_HALO, C), lambda s, i: (s, 0, 0))
    full = lambda s, i: (0, 0)
    return pl.pallas_call(
        functools.partial(_conv_kernel, tm=tm, zero_first=zero_first),
        grid=(n_seq, lt),
        in_specs=[pl.BlockSpec((tm, C), lambda s, i: (r0 + s * lt + i, 0)),
                  halo_spec,
                  pl.BlockSpec((CV_KERNEL, C), full),
                  pl.BlockSpec((1, C), full), pl.BlockSpec((1, C), full), pl.BlockSpec((1, C), full)],
        out_specs=pl.BlockSpec((tm, C), lambda s, i: (s * lt + i, 0)),
        out_shape=jax.ShapeDtypeStruct((n_seq * L, C), BF16),
        scratch_shapes=[pltpu.VMEM((CONV_HALO + tm, C), F32),
                        pltpu.VMEM((SUBLANES, CONV_HALO + tm, LANES), F32),
                        pltpu.VMEM((tm, C), F32)],
        compiler_params=_cp("parallel", "arbitrary"),
        name="conv_prompt" if zero_first else "conv_sample",
    )(glu, halo_src, w_dw, b_dw, g, b)


def _qproj_kernel(cq_ref, w_ref, cos_ref, sin_ref, q_ref, *, hg, nb, L):
    cq = cq_ref[...]
    cos, sin = cos_ref[...], sin_ref[...]
    for h in range(hg):
        r = _dot(cq, w_ref[h])
        nope = r[:, :QK_NOPE] * ATTN_SCALE
        rp = (r[:, QK_NOPE:QK_NOPE + QK_ROPE] * cos + r[:, QK_NOPE + QK_ROPE:] * sin) * ATTN_SCALE
        if nb == 1:
            q_ref[0, h, :, :QK_NOPE] = nope.astype(q_ref.dtype)
            q_ref[0, h, :, QK_NOPE:] = rp.astype(q_ref.dtype)
        else:
            q_ref[:, h, :, :QK_NOPE] = nope.reshape(nb, L, QK_NOPE).astype(q_ref.dtype)
            q_ref[:, h, :, QK_NOPE:] = rp.reshape(nb, L, QK_ROPE).astype(q_ref.dtype)


def _qproj(cqn, wq3, cos, sin, row0, n_seq, L, hg=4, tm=512):
    R = cqn.shape[1]
    H = wq3.shape[0]
    hg = _tile(H, hg, 1)
    dq = QK_NOPE + QK_ROPE
    if L >= tm:
        tm = _tile(L, tm)
        nb, lt = 1, L // tm
        out_spec = pl.BlockSpec((1, hg, tm, dq), lambda i, h: (i // lt, h, i % lt, 0))
    else:
        nb = _tile(n_seq, max(tm // L, 1), 1)
        tm = nb * L
        out_spec = pl.BlockSpec((nb, hg, L, dq), lambda i, h: (i, h, 0, 0))
    assert row0 % tm == 0
    r0 = row0 // tm
    row = lambda i, h: (r0 + i, 0)
    return pl.pallas_call(
        functools.partial(_qproj_kernel, hg=hg, nb=nb, L=L),
        grid=(n_seq * L // tm, H // hg),
        in_specs=[pl.BlockSpec((tm, R), row),
                  pl.BlockSpec((hg, R, wq3.shape[2]), lambda i, h: (h, 0, 0)),
                  pl.BlockSpec((tm, QK_ROPE), row), pl.BlockSpec((tm, QK_ROPE), row)],
        out_specs=out_spec,
        out_shape=jax.ShapeDtypeStruct((n_seq, H, L, dq), BF16),
        compiler_params=_cp("parallel", "arbitrary"),
        name="qproj",
    )(cqn, wq3, cos, sin)


def _qproj_t_kernel(cq_ref, w_ref, cos_ref, sin_ref, q_ref, *, hg, scale):
    cq = cq_ref[...]
    cos, sin = cos_ref[...], sin_ref[...]
    for h in range(hg):
        r = _dot_nt(w_ref[h], cq)
        rp = r[QK_NOPE:QK_NOPE + QK_ROPE] * cos + r[QK_NOPE + QK_ROPE:] * sin
        q_ref[0, h, :QK_NOPE, :] = (r[:QK_NOPE] * scale).astype(q_ref.dtype)
        q_ref[0, h, QK_NOPE:QK_NOPE + QK_ROPE, :] = (rp * scale).astype(q_ref.dtype)
        q_ref[0, h, QK_NOPE + QK_ROPE:, :] = jnp.zeros((QK_PAD - QK_NOPE - QK_ROPE, cq.shape[0]), q_ref.dtype)


def _qproj_t(cqn, wq3t, cos_t, sin_t, n_seq, L, scale, hg=8, tm=512):
    R = cqn.shape[1]
    H = wq3t.shape[0]
    hg = _tile(H, hg, 1)
    tm = _tile(L, tm, LANES)
    lt = L // tm
    dq = QK_PAD
    return pl.pallas_call(
        functools.partial(_qproj_t_kernel, hg=hg, scale=scale),
        grid=(n_seq * lt, H // hg),
        in_specs=[pl.BlockSpec((tm, R), lambda i, h: (i, 0)),
                  pl.BlockSpec((hg, wq3t.shape[1], R), lambda i, h: (h, 0, 0)),
                  pl.BlockSpec((QK_ROPE, tm), lambda i, h: (0, i)),
                  pl.BlockSpec((QK_ROPE, tm), lambda i, h: (0, i))],
        out_specs=pl.BlockSpec((1, hg, dq, tm), lambda i, h: (i // lt, h, 0, i % lt)),
        out_shape=jax.ShapeDtypeStruct((n_seq, H, dq, L), BF16),
        compiler_params=_cp("parallel", "arbitrary"),
        name="qproj_t",
    )(cqn, wq3t, cos_t, sin_t)


def _kvproj_kernel(ckv_ref, kr_ref, w_ref, k_ref, vt_ref, *, hg):
    ckv = ckv_ref[...].astype(BF16)
    kr = kr_ref[...].astype(BF16)
    for h in range(hg):
        r = _dot(ckv, w_ref[h])
        k_ref[0, h, :, :QK_NOPE] = r[:, :QK_NOPE].astype(k_ref.dtype)
        k_ref[0, h, :, QK_NOPE:QK_NOPE + QK_ROPE] = kr
        k_ref[0, h, :, QK_NOPE + QK_ROPE:] = jnp.zeros((kr.shape[0], QK_PAD - QK_NOPE - QK_ROPE), k_ref.dtype)
        vt_ref[0, h, :V_DIM, :] = r[:, QK_NOPE:].T.astype(vt_ref.dtype)
        vt_ref[0, h, V_DIM:, :] = jnp.ones((V_ONES, ckv.shape[0]), vt_ref.dtype)


def _kvproj(ckv, kr, wkv3, n_seq, L, hg=8, tm=512):
    R = ckv.shape[1]
    H = wkv3.shape[0]
    hg = _tile(H, hg, 1)
    tm = _tile(L, tm, LANES)
    lt = L // tm
    dq = QK_PAD
    row = lambda i, h: (i, 0)
    wmap = lambda i, h: (h, 0, 0)
    return pl.pallas_call(
        functools.partial(_kvproj_kernel, hg=hg),
        grid=(n_seq * lt, H // hg),
        in_specs=[pl.BlockSpec((tm, R), row), pl.BlockSpec((tm, QK_ROPE), row),
                  pl.BlockSpec((hg, R, wkv3.shape[2]), wmap)],
        out_specs=[pl.BlockSpec((1, hg, tm, dq), lambda i, h: (i // lt, h, i % lt, 0)),
                   pl.BlockSpec((1, hg, V_DIM + V_ONES, tm), lambda i, h: (i // lt, h, 0, i % lt))],
        out_shape=[jax.ShapeDtypeStruct((n_seq, H, L, dq), BF16),
                   jax.ShapeDtypeStruct((n_seq, H, V_DIM + V_ONES, L), BF16)],
        compiler_params=_cp("parallel", "arbitrary"),
        name="kvproj",
    )(ckv, kr, wkv3)


def _attn_prompt_kernel(qt_ref, k_ref, vt_ref, o_ref, s_ref, p_ref, acc_ref, *, tq, tk, hg):
    qi = pl.program_id(2)
    d0 = pl.multiple_of(qi * tq, tq)
    d1 = pl.multiple_of(d0 + tk, tk)
    all_q = slice(0, tq)
    late_q = slice(tk, tq)

    def scores(g, j0, slot, cols=all_q):
        s = _dot(k_ref[0, g, pl.ds(j0, tk), :], qt_ref[0, g, :, cols])
        s_ref[slot, g, :, cols] = s
        return jnp.max(s, axis=0, keepdims=True)

    def softmax(g, slot, mx, m, mask=None, cols=all_q):
        s = s_ref[slot, g, :, cols]
        if mask is not None:
            s = jnp.where(mask, s, -jnp.inf)
            mx = jnp.max(s, axis=0, keepdims=True)
        m_new = jnp.maximum(m, mx)
        p_ref[slot, g, :, cols] = jnp.exp2(s - m_new).astype(BF16)
        return jnp.exp2(m - m_new), m_new

    def accumulate(g, j0, slot, alpha, cols=all_q):
        acc_ref[g, :, cols] = (alpha * acc_ref[g, :, cols]
                               + _dot(vt_ref[0, g, :, pl.ds(j0, tk)], p_ref[slot, g, :, cols]))

    def step(j0, slot, state, mask=None, next_cols=all_q):
        jp = pl.multiple_of(jnp.maximum(j0 - tk, 0), tk)
        new = []
        for g in range(hg):
            mx, a_prev, m = state[g]
            mx_next = scores(g, pl.multiple_of(j0 + tk, tk), 1 - slot, next_cols)
            accumulate(g, jp, 1 - slot, a_prev)
            alpha, m = softmax(g, slot, mx, m, mask)
            new.append((mx_next, alpha, m))
        return tuple(new)

    def pair(i, state):
        j0 = pl.multiple_of(2 * i * tk, tk)
        state = step(j0, 0, state)
        return step(pl.multiple_of(j0 + tk, tk), 1, state)

    p_ref[1] = jnp.zeros_like(p_ref[1])
    acc_ref[...] = jnp.zeros_like(acc_ref)
    init = tuple((scores(g, 0, 0), jnp.ones((1, tq), F32), jnp.full((1, tq), -jnp.inf, F32))
                 for g in range(hg))
    n_pairs = qi * (tq // (2 * tk))
    odd = lax.rem(n_pairs, 2)
    state = lax.cond(odd == 1, lambda st: pair(0, st), lambda st: st, init)
    state = lax.fori_loop(0, n_pairs // 2, lambda i, st: pair(2 * i + odd + 1, pair(2 * i + odd, st)), state)

    key_chunk = lax.broadcasted_iota(jnp.int32, (tk, tq), 0) // CHUNK
    qry_chunk = lax.broadcasted_iota(jnp.int32, (tk, tq), 1) // CHUNK
    visible = key_chunk <= qry_chunk
    state = step(d0, 0, state, mask=visible, next_cols=late_q)
    for g in range(hg):
        _, a_prev, m = state[g]
        accumulate(g, d0, 0, a_prev)
        alpha, _ = softmax(g, 1, None, m[:, late_q], visible[:, :tq - tk], late_q)
        accumulate(g, d1, 1, alpha, late_q)
        o = acc_ref[g, :V_DIM, :] / acc_ref[g, V_DIM:V_DIM + 1, :]
        o_ref[:, g * V_DIM:(g + 1) * V_DIM] = o.T.astype(o_ref.dtype)


def _attn_prompt(qt, k, vt, tq=512, hg=2):
    B, H, dq, S = qt.shape
    tq = _tile(S, tq, 2 * LANES)
    tk = tq // 2
    hg = _tile(H, hg, 1)
    return pl.pallas_call(
        functools.partial(_attn_prompt_kernel, tq=tq, tk=tk, hg=hg),
        grid=(B, H // hg, S // tq),
        in_specs=[pl.BlockSpec((1, hg, dq, tq), lambda b, h, i: (b, h, 0, i)),
                  pl.BlockSpec((1, hg, S, dq), lambda b, h, i: (b, h, 0, 0)),
                  pl.BlockSpec((1, hg, vt.shape[2], S), lambda b, h, i: (b, h, 0, 0))],
        out_specs=pl.BlockSpec((tq, hg * V_DIM), lambda b, h, i: (b * (S // tq) + i, h)),
        out_shape=jax.ShapeDtypeStruct((B * S, H * V_DIM), BF16),
        scratch_shapes=[pltpu.VMEM((2, hg, tk, tq), F32), pltpu.VMEM((2, hg, tk, tq), BF16),
                        pltpu.VMEM((hg, vt.shape[2], tq), F32)],
        compiler_params=_cp("parallel", "parallel", "arbitrary"),
        name="attn_prompt",
    )(qt, k, vt)


def _attn_sample_kernel(q_ref, plat_ref, pkrt_ref, nlat_ref, nkr_ref, w_ref, o_ref, ql_ref, qr_ref,
                        *, H, L, kc):
    P = plat_ref.shape[2]
    for h in range(H):
        qh = q_ref[0, h]
        ql_ref[h * L:(h + 1) * L, :] = _dot_nt(qh[:, :QK_NOPE], w_ref[h, :, :QK_NOPE]).astype(BF16)
        qr_ref[h * L:(h + 1) * L, :] = qh[:, QK_NOPE:]
    ql = ql_ref[...]
    qr = qr_ref[...]

    def step(lat, s_rope, carry):
        m, l, acc = carry
        s = _dot_nt(ql, lat) + s_rope
        m_new = jnp.maximum(m, jnp.max(s, axis=-1, keepdims=True))
        alpha = jnp.exp(m - m_new)
        p = jnp.exp(s - m_new)
        l = alpha * l + jnp.sum(p, axis=-1, keepdims=True)
        acc = alpha * acc + _dot(p.astype(BF16), lat)
        return m_new, l, acc

    R = ql.shape[1]
    carry = (jnp.full((H * L, 1), -jnp.inf, F32), jnp.zeros((H * L, 1), F32), jnp.zeros((H * L, R), F32))
    for c in range(P // kc):
        krt = pkrt_ref[0, 0, :, c * kc:(c + 1) * kc].astype(BF16)
        carry = step(plat_ref[0, 0, c * kc:(c + 1) * kc, :].astype(BF16), _dot(qr, krt), carry)
    m, l, acc = step(nlat_ref[...].astype(BF16), _dot_nt(qr, nkr_ref[...].astype(BF16)), carry)
    ol = (acc / l).astype(BF16)
    for h in range(H):
        o_ref[:, h * V_DIM:(h + 1) * V_DIM] = _dot(ol[h * L:(h + 1) * L, :], w_ref[h, :, QK_NOPE:]).astype(o_ref.dtype)


def _attn_sample(q, cache_lat, cache_kr_t, layer, ckv, kr, wkv3, row0):
    n_req, H, L, dq = q.shape
    P, R = cache_lat.shape[2], cache_lat.shape[3]
    kc = _tile(P, 1024)
    assert row0 % L == 0
    r0 = row0 // L
    return pl.pallas_call(
        functools.partial(_attn_sample_kernel, H=H, L=L, kc=kc),
        grid=(n_req,),
        in_specs=[pl.BlockSpec((1, H, L, dq), lambda b: (b, 0, 0, 0)),
                  pl.BlockSpec((1, 1, P, R), lambda b: (layer, b, 0, 0)),
                  pl.BlockSpec((1, 1, QK_ROPE, P), lambda b: (layer, b, 0, 0)),
                  pl.BlockSpec((L, R), lambda b: (r0 + b, 0)),
                  pl.BlockSpec((L, QK_ROPE), lambda b: (r0 + b, 0)),
                  pl.BlockSpec(wkv3.shape, lambda b: (0, 0, 0))],
        out_specs=pl.BlockSpec((L, H * V_DIM), lambda b: (b, 0)),
        out_shape=jax.ShapeDtypeStruct((n_req * L, H * V_DIM), BF16),
        scratch_shapes=[pltpu.VMEM((H * L, R), BF16), pltpu.VMEM((H * L, QK_ROPE), BF16)],
        compiler_params=_cp("parallel"),
        name="attn_sample",
    )(q, cache_lat, cache_kr_t, ckv, kr, wkv3)


def _merge_kernel(x_ref, us_ref, ybp_ref, ybs_ref, op_ref, os_ref, wp_ref, wpw_ref, wo_ref,
                  wga_ref, wgb_ref, wgc_ref, ba_ref, bb_ref, bc_ref, out_ref, *, npt):
    prompt = pl.program_id(0) < npt
    x = x_ref[...]
    yb_in = jnp.where(prompt, ybp_ref[...], ybs_ref[...])
    o_in = jnp.where(prompt, op_ref[...], os_ref[...])
    out = jax.nn.sigmoid(_dot_nt(x, wga_ref[0]) + ba_ref[...]) * _dot(us_ref[...], wp_ref[0])
    out += jax.nn.sigmoid(_dot_nt(x, wgb_ref[0]) + bb_ref[...]) * _dot(yb_in, wpw_ref[0])
    out += jax.nn.sigmoid(_dot_nt(x, wgc_ref[0]) + bc_ref[...]) * _dot(o_in, wo_ref[0])
    out_ref[...] = out.astype(out_ref.dtype)


def _merge(xb, us, yb_p, yb_s, o_p, o_s, wp, wpw, wo, wgt, bg, layer, tm=512, tn=512):
    M = us.shape[0]
    D = wp.shape[2]
    TP, TS = yb_p.shape[0], yb_s.shape[0]
    tm, tn = _tile(math.gcd(TP, TS), tm), _tile(D, tn, LANES)
    npt = TP // tm
    nj = D // tn
    row = lambda i, j: (i, 0)
    prow = lambda i, j: (jnp.minimum(i, npt - 1), 0)
    srow = lambda i, j: (jnp.maximum(i - npt, 0), 0)
    col = lambda i, j: (layer, 0, j)
    gcol = lambda g: (lambda i, j: (0, g * nj + j))
    grow = lambda g: _wt_spec(layer, g * D, tn, wgt.shape[2], lambda i, j: j)
    return pl.pallas_call(
        functools.partial(_merge_kernel, npt=npt),
        grid=(M // tm, nj),
        in_specs=[pl.BlockSpec((tm, xb.shape[1]), row), pl.BlockSpec((tm, us.shape[1]), row),
                  pl.BlockSpec((tm, yb_p.shape[1]), prow), pl.BlockSpec((tm, yb_s.shape[1]), srow),
                  pl.BlockSpec((tm, o_p.shape[1]), prow), pl.BlockSpec((tm, o_s.shape[1]), srow),
                  pl.BlockSpec((1, wp.shape[1], tn), col), pl.BlockSpec((1, wpw.shape[1], tn), col),
                  pl.BlockSpec((1, wo.shape[1], tn), col),
                  grow(0), grow(1), grow(2),
                  pl.BlockSpec((1, tn), gcol(0)), pl.BlockSpec((1, tn), gcol(1)), pl.BlockSpec((1, tn), gcol(2))],
        out_specs=pl.BlockSpec((tm, tn), lambda i, j: (i, j)),
        out_shape=jax.ShapeDtypeStruct((M, D), BF16),
        compiler_params=_cp("parallel", "arbitrary"),
        name="merge",
    )(xb, us, yb_p, yb_s, o_p, o_s, wp, wpw, wo, wgt, wgt, wgt, bg, bg, bg)


def _outproj_kernel(m_ref, w_ref, x_ref, g_ref, b_ref, o_ref, ob_ref, *, alpha):
    y = _ln(alpha * x_ref[...] + _dot(m_ref[...], w_ref[0]), g_ref[...], b_ref[...])
    o_ref[...] = y
    ob_ref[...] = y.astype(ob_ref.dtype)


def _outproj_ln(merged, w, layer, x, g, b, alpha, tm=512):
    M, D = x.shape
    tm = _tile(M, tm)
    row = lambda i: (i, 0)
    full = lambda i: (0, 0)
    return pl.pallas_call(
        functools.partial(_outproj_kernel, alpha=alpha),
        grid=(M // tm,),
        in_specs=[pl.BlockSpec((tm, merged.shape[1]), row),
                  pl.BlockSpec((1,) + w.shape[1:], lambda i: (layer, 0, 0)),
                  pl.BlockSpec((tm, D), row), pl.BlockSpec((1, D), full), pl.BlockSpec((1, D), full)],
        out_specs=[pl.BlockSpec((tm, D), row), pl.BlockSpec((tm, D), row)],
        out_shape=[jax.ShapeDtypeStruct((M, D), F32), jax.ShapeDtypeStruct((M, D), BF16)],
        compiler_params=_cp("parallel"),
        name="outproj_ln",
    )(merged, w, x, g, b)


def _swiglu_partial(x, w1_ref, w3_ref, w2_ref):
    tf = w1_ref.shape[3]
    n_chunks = FFN_CHUNKS if tf % (FFN_CHUNKS * LANES) == 0 else 1
    c = tf // n_chunks
    hs = []
    for k in range(n_chunks):
        cols = slice(k * c, (k + 1) * c)
        h = jax.nn.silu(_dot(x, w1_ref[0, 0, :, cols])) * _dot(x, w3_ref[0, 0, :, cols])
        hs.append(h.astype(BF16))
    out = _dot(hs[0], w2_ref[0, 0, 0:c, :])
    for k in range(1, n_chunks):
        out += _dot(hs[k], w2_ref[0, 0, k * c:(k + 1) * c, :])
    return out


def _ffn_dense_kernel(eid_ref, nu_ref, x_ref, w1_ref, w3_ref, w2_ref, r_ref, g_ref, b_ref,
                      o_ref, ob_ref, acc_ref, *, alpha):
    f = pl.program_id(1)

    @pl.when(f == 0)
    def _():
        acc_ref[...] = jnp.zeros_like(acc_ref)

    acc_ref[...] += _swiglu_partial(x_ref[...], w1_ref, w3_ref, w2_ref)

    @pl.when(f == pl.num_programs(1) - 1)
    def _():
        y = _ln(alpha * r_ref[...] + acc_ref[...], g_ref[...], b_ref[...])
        o_ref[...] = y
        ob_ref[...] = y.astype(ob_ref.dtype)


def _ffn_moe_kernel(eid_ref, nu_ref, idx0_ref, idxn_ref, x_hbm, w1_ref, w3_ref, w2_ref, o_ref,
                    xg_ref, xb_ref, sem, *, tm):
    i = pl.program_id(0)
    f = pl.program_id(1)
    n_used = nu_ref[0]

    def row_copy(t, r):
        return pltpu.make_async_copy(x_hbm.at[pl.ds(t, 1), :], xg_ref.at[pl.ds(r, 1), :], sem)

    def start_rows(idx_ref):
        def body(c, carry):
            for u in range(GATHER_UNROLL):
                r = c * GATHER_UNROLL + u
                row_copy(idx_ref[0, 0, r], r).start(priority=u % 2)
            return carry
        lax.fori_loop(0, tm // GATHER_UNROLL, body, 0)

    def wait_rows():
        def body(c, carry):
            for u in range(GATHER_UNROLL):
                row_copy(0, c * GATHER_UNROLL + u).wait()
            return carry
        lax.fori_loop(0, tm // GATHER_UNROLL, body, 0)

    @pl.when(f == 0)
    def _():
        o_ref[...] = jnp.zeros_like(o_ref)

    @pl.when((f == 0) & (i < n_used))
    def _():
        @pl.when(i == 0)
        def _():
            start_rows(idx0_ref)

        wait_rows()
        xb_ref[...] = xg_ref[...].astype(BF16)

    @pl.when((f == jnp.minimum(1, pl.num_programs(1) - 1)) & (i + 1 < n_used))
    def _():
        start_rows(idxn_ref)

    @pl.when(i < n_used)
    def _():
        o_ref[...] += _swiglu_partial(xb_ref[...], w1_ref, w3_ref, w2_ref)


def _ffn_weight_specs(layer, D, tf):
    last = lambda i, nu: jnp.minimum(i, nu[0] - 1)
    w13 = lambda i, f, e, nu: (layer, e[last(i, nu)], 0, jnp.where(i < nu[0], f, 0))
    w2m = lambda i, f, e, nu: (layer, e[last(i, nu)], jnp.where(i < nu[0], f, 0), 0)
    return [pl.BlockSpec((1, 1, D, tf), w13), pl.BlockSpec((1, 1, D, tf), w13), pl.BlockSpec((1, 1, tf, D), w2m)]


def _ffn_dense(x, eid, n_used, w1, w3, w2, layer, tm, resid, g, b, alpha, tf=512):
    M, D = x.shape
    Fdim = w1.shape[3]
    tf = _tile(Fdim, tf, LANES)
    row = lambda i, f, e, nu: (i, 0)
    full = lambda i, f, e, nu: (0, 0)
    return pl.pallas_call(
        functools.partial(_ffn_dense_kernel, alpha=alpha),
        grid_spec=pltpu.PrefetchScalarGridSpec(
            num_scalar_prefetch=2, grid=(M // tm, Fdim // tf),
            in_specs=[pl.BlockSpec((tm, D), row)] + _ffn_weight_specs(layer, D, tf)
            + [pl.BlockSpec((tm, D), row), pl.BlockSpec((1, D), full), pl.BlockSpec((1, D), full)],
            out_specs=[pl.BlockSpec((tm, D), row), pl.BlockSpec((tm, D), row)],
            scratch_shapes=[pltpu.VMEM((tm, D), F32)]),
        out_shape=[jax.ShapeDtypeStruct((M, D), F32), jax.ShapeDtypeStruct((M, D), BF16)],
        compiler_params=_cp("arbitrary", "arbitrary"),
        name="ffn_dense",
    )(eid, n_used, x, w1, w3, w2, resid, g, b)


def _ffn_moe(x, slot_tok, eid, n_used, w1, w3, w2, layer, tm, tf=1408):
    D = x.shape[1]
    n_slots = slot_tok.shape[0]
    nb = n_slots // tm
    Fdim = w1.shape[3]
    tf = _tile(Fdim, tf, LANES)
    idx3 = slot_tok.reshape(nb, 1, tm)
    smem = functools.partial(pl.BlockSpec, (1, 1, tm), memory_space=pltpu.SMEM)
    return pl.pallas_call(
        functools.partial(_ffn_moe_kernel, tm=tm),
        grid_spec=pltpu.PrefetchScalarGridSpec(
            num_scalar_prefetch=2, grid=(nb, Fdim // tf),
            in_specs=[smem(lambda i, f, e, nu: (0, 0, 0)),
                      smem(lambda i, f, e, nu: (jnp.minimum(i + 1, nb - 1), 0, 0)),
                      pl.BlockSpec(memory_space=pl.ANY)] + _ffn_weight_specs(layer, D, tf),
            out_specs=pl.BlockSpec((tm, D), lambda i, f, e, nu: (i, 0)),
            scratch_shapes=[pltpu.VMEM((tm, D), F32), pltpu.VMEM((tm, D), BF16),
                            pltpu.SemaphoreType.DMA(())]),
        out_shape=jax.ShapeDtypeStruct((n_slots, D), F32),
        compiler_params=_cp("arbitrary", "arbitrary"),
        name="ffn_moe",
    )(eid, n_used, idx3, idx3, x, w1, w3, w2)


def _router_kernel(x_ref, r_ref, o_ref, *, n_experts):
    logits = jnp.dot(x_ref[...], r_ref[...], preferred_element_type=F32,
                     precision=lax.Precision.HIGHEST)
    col = lax.broadcasted_iota(jnp.int32, logits.shape, 1)
    lg = jnp.where(col < n_experts, logits, -jnp.inf)
    m1 = jnp.max(lg, axis=-1, keepdims=True)
    i1 = jnp.min(jnp.where(lg == m1, col, LANES), axis=-1, keepdims=True)
    lg2 = jnp.where(col == i1, -jnp.inf, lg)
    m2 = jnp.max(lg2, axis=-1, keepdims=True)
    i2 = jnp.min(jnp.where(lg2 == m2, col, LANES), axis=-1, keepdims=True)
    e = jnp.exp(m2 - m1)
    g1 = 1.0 / (1.0 + e)
    g2 = e * g1
    out = jnp.where(col == 0, i1.astype(F32),
                    jnp.where(col == 1, i2.astype(F32),
                              jnp.where(col == 2, g1, jnp.where(col == 3, g2, 0.0))))
    o_ref[...] = out


def _router(x, router_pad, n_experts, tm=512):
    M, D = x.shape
    tm = _tile(M, tm)
    return pl.pallas_call(
        functools.partial(_router_kernel, n_experts=n_experts),
        grid=(M // tm,),
        in_specs=[pl.BlockSpec((tm, D), lambda i: (i, 0)), pl.BlockSpec((D, LANES), lambda i: (0, 0))],
        out_specs=pl.BlockSpec((tm, LANES), lambda i: (i, 0)),
        out_shape=jax.ShapeDtypeStruct((M, LANES), F32),
        compiler_params=_cp("parallel"),
        name="router",
    )(x, router_pad)


def _combine_kernel(idx0_ref, idxn_ref, x_ref, r_ref, g_ref, b_ref, y_hbm, o_ref, ob_ref, yg_ref, sem,
                    *, tm, alpha):
    i = pl.program_id(0)
    slot = lax.rem(i, 2)

    def row_copy(t, k, r, s):
        return pltpu.make_async_copy(y_hbm.at[pl.ds(t, 1), :], yg_ref.at[s, k, pl.ds(r, 1), :], sem.at[s])

    def start_rows(idx_ref, s):
        def body(c, carry):
            for u in range(GATHER_UNROLL):
                r = c * GATHER_UNROLL + u
                for k in range(TOP_K):
                    row_copy(idx_ref[0, k, r], k, r, s).start(priority=(u + k) % 2)
            return carry
        lax.fori_loop(0, tm // GATHER_UNROLL, body, 0)

    def wait_rows(s):
        def body(c, carry):
            for u in range(GATHER_UNROLL):
                for k in range(TOP_K):
                    row_copy(0, k, c * GATHER_UNROLL + u, s).wait()
            return carry
        lax.fori_loop(0, tm // GATHER_UNROLL, body, 0)

    @pl.when(i == 0)
    def _():
        start_rows(idx0_ref, 0)

    wait_rows(slot)

    @pl.when(i + 1 < pl.num_programs(0))
    def _():
        start_rows(idxn_ref, 1 - slot)

    r = r_ref[...]
    y = alpha * x_ref[...] + r[:, 2:3] * yg_ref[slot, 0] + r[:, 3:4] * yg_ref[slot, 1]
    y = _ln(y, g_ref[...], b_ref[...])
    o_ref[...] = y
    ob_ref[...] = y.astype(ob_ref.dtype)


def _combine(x, y, dest, route, g, b, alpha, tm=256):
    M, D = x.shape
    tm = _tile(M, tm)
    nt = M // tm
    idx3 = jnp.transpose(dest.reshape(nt, tm, TOP_K), (0, 2, 1))
    smem = functools.partial(pl.BlockSpec, (1, TOP_K, tm), memory_space=pltpu.SMEM)
    row = lambda i: (i, 0)
    full = lambda i: (0, 0)
    return pl.pallas_call(
        functools.partial(_combine_kernel, tm=tm, alpha=alpha),
        grid=(nt,),
        in_specs=[smem(lambda i: (0, 0, 0)), smem(lambda i: (jnp.minimum(i + 1, nt - 1), 0, 0)),
                  pl.BlockSpec((tm, D), row), pl.BlockSpec((tm, LANES), row),
                  pl.BlockSpec((1, D), full), pl.BlockSpec((1, D), full),
                  pl.BlockSpec(memory_space=pl.ANY)],
        out_specs=[pl.BlockSpec((tm, D), row), pl.BlockSpec((tm, D), row)],
        out_shape=[jax.ShapeDtypeStruct((M, D), F32), jax.ShapeDtypeStruct((M, D), BF16)],
        scratch_shapes=[pltpu.VMEM((2, TOP_K, tm, D), F32), pltpu.SemaphoreType.DMA((2,))],
        compiler_params=_cp("arbitrary"),
        name="combine",
    )(idx3, idx3, x, route, g, b, y)


def _moe_plan(route, n_experts, tm):
    T = route.shape[0]
    n_assign = T * TOP_K
    e_flat = route[:, :TOP_K].astype(jnp.int32).reshape(-1)
    onehot = (e_flat[:, None] == jnp.arange(n_experts, dtype=jnp.int32)[None, :]).astype(jnp.int32)
    csum = jnp.cumsum(onehot, axis=0)
    rank = jnp.take_along_axis(csum, e_flat[:, None], axis=1)[:, 0] - 1
    counts = csum[-1]
    padded = ((counts + tm - 1) // tm) * tm
    pad_end = jnp.cumsum(padded)
    pad_start = pad_end - padded
    dest = (pad_start[e_flat] + rank).astype(jnp.int32)
    n_blocks = -(-n_assign // tm) + n_experts
    n_slots = n_blocks * tm
    slot_tok = jnp.zeros((n_slots,), jnp.int32).at[dest].set(jnp.arange(n_assign, dtype=jnp.int32) // TOP_K)
    block_e = jnp.minimum(jnp.searchsorted(pad_end, jnp.arange(n_blocks, dtype=jnp.int32) * tm, side="right"),
                          n_experts - 1).astype(jnp.int32)
    n_used = (pad_end[-1:] // tm).astype(jnp.int32)
    return slot_tok, block_e, n_used, dest.reshape(T, TOP_K)


def _rot_half_cols(w):
    half = w.shape[-1] // 2
    return jnp.concatenate([-w[..., half:], w[..., :half]], axis=-1)


def kernel(x_prompt, x_sample, cache_mla_latent, cache_mla_krope, cache_conv, w_in, b_gate, gm_ln_g, gm_ln_b, gm_w_s, gm_b_s, gm_w_p, cv_w_dw, cv_b_dw, cv_ln_g, cv_ln_b, cv_w_pw, mla_q_norm_g, mla_kv_norm_g, mla_w_uq, mla_w_uk, mla_w_uv, mla_w_o, w_out, ln1_g, ln1_b, ln2_g, ln2_b, ffn_w1, ffn_w3, ffn_w2, moe_router, moe_w1, moe_w3, moe_w2):
    B, S, D = x_prompt.shape
    NB, L, _ = x_sample.shape
    depth = w_in.shape[0]
    past = cache_mla_latent.shape[2]
    GW = gm_ln_g.shape[1]
    CW = cv_ln_g.shape[1]
    RQ = mla_q_norm_g.shape[1]
    RKV = mla_kv_norm_g.shape[1]
    H = N_HEADS
    n_experts = moe_router.shape[2]
    alpha = float((2 * depth) ** 0.25)
    TP, TS = B * S, NB * L
    T = TP + TS
    off_cv = 2 * GW
    off_q = off_cv + 2 * CW
    off_kv = off_q + RQ
    off_kr = off_kv + RKV
    off_g = off_kr + QK_ROPE
    assert L <= GM_CHUNK and GM_CHUNK % L == 0 and S % GM_CHUNK == 0 and TS % GM_CHUNK == 0
    assert L >= CV_KERNEL - 1 and L % 8 == 0 and CV_KERNEL - 1 <= CONV_HALO

    w_in_t = jnp.swapaxes(w_in, 1, 2).astype(BF16)
    wkr_t = w_in_t[:, off_kr:off_g, :]
    half_r = QK_ROPE // 2
    wkr2_t = jnp.concatenate([wkr_t, -wkr_t[:, half_r:], wkr_t[:, :half_r]], axis=1)
    wg_t = w_in_t[:, off_g:, :]
    gm_w_p_b, cv_w_pw_b = gm_w_p.astype(BF16), cv_w_pw.astype(BF16)
    w_o_b, w_out_b = mla_w_o.astype(BF16), w_out.astype(BF16)
    ffn_w1_b, ffn_w3_b, ffn_w2_b = (w.astype(BF16)[:, None] for w in (ffn_w1, ffn_w3, ffn_w2))
    moe_w1_b, moe_w3_b, moe_w2_b = moe_w1.astype(BF16), moe_w3.astype(BF16), moe_w2.astype(BF16)
    zero_uv = jnp.zeros((1, off_cv), F32)

    uq = mla_w_uq.reshape(depth, RQ, H, QK_NOPE + QK_ROPE)
    uq_rope = uq[..., QK_NOPE:]
    wq3 = jnp.concatenate([uq[..., :QK_NOPE], uq_rope, _rot_half_cols(uq_rope)], axis=-1)
    wq3 = jnp.transpose(wq3, (0, 2, 1, 3)).astype(BF16)
    wkv3 = jnp.concatenate([mla_w_uk.reshape(depth, RKV, H, QK_NOPE),
                            mla_w_uv.reshape(depth, RKV, H, V_DIM)], axis=-1)
    wkv3 = jnp.transpose(wkv3, (0, 2, 1, 3)).astype(BF16)
    wq3t = jnp.swapaxes(wq3, 2, 3)

    causal = jnp.tril(jnp.ones((GM_CHUNK, GM_CHUNK), bool))
    ws_p = jnp.where(causal, gm_w_s, 0.0)
    reps = GM_CHUNK // L
    ws_l = jnp.where(causal[:L, :L], gm_w_s[:, :, :L, :L], 0.0)
    ws_s = jnp.einsum("ab,lgij->lgaibj", jnp.eye(reps, dtype=F32), ws_l).reshape(depth, GM_GROUPS, GM_CHUNK, GM_CHUNK)
    ws2 = jnp.stack([ws_p, ws_s], axis=1).astype(BF16)
    gd = GW // GM_GROUPS
    bs_p = jnp.repeat(jnp.transpose(gm_b_s, (0, 2, 1)), gd, axis=2)
    bs_s = jnp.repeat(jnp.tile(jnp.transpose(gm_b_s[:, :, :L], (0, 2, 1)), (1, reps, 1)), gd, axis=2)
    bs2 = jnp.stack([bs_p, bs_s], axis=1)

    router_pad = jnp.pad(moe_router, ((0, 0), (0, 0), (0, LANES - n_experts)))

    half = QK_ROPE // 2
    inv = ROPE_THETA ** (-jnp.arange(half, dtype=F32) / half)
    pos = jnp.concatenate([jnp.tile(jnp.arange(S), B), jnp.tile(past + jnp.arange(L), NB)]).astype(F32)
    ang = pos[:, None] * inv[None, :]
    cos = jnp.tile(jnp.cos(ang), (1, 2))
    sin = jnp.tile(jnp.sin(ang), (1, 2))
    cos_t, sin_t = cos[:TP].T, sin[:TP].T
    scale_log2e = ATTN_SCALE * math.log2(math.e)

    cache_kr_t = jnp.swapaxes(cache_mla_krope, 2, 3)
    hist_s = jnp.pad(cache_conv, ((0, 0), (0, 0), (CONV_HALO - (CV_KERNEL - 1), 0), (0, 0)))

    x = jnp.concatenate([x_prompt.reshape(TP, D), x_sample.reshape(TS, D)], axis=0)
    xb = x.astype(BF16)
    row2 = lambda a: a.reshape(1, -1)

    tm_dense = _tile(T, 512)
    eid_dense = jnp.zeros((T // tm_dense,), jnp.int32)
    nu_dense = jnp.full((1,), T // tm_dense, jnp.int32)

    outs = {k: [] for k in ("lat_p", "kr_p", "conv_p", "lat_s", "kr_s", "conv_s", "v_s")}
    for l in range(depth):
        uvg = _mm_act(xb, w_in_t, l, 0, off_cv, zero_uv, "gelu", F32)
        glu = _mm_glu(xb, w_in_t, l, off_cv, off_cv + CW, CW)
        cqn, ckv, kr = _latents(xb, w_in_t, wkr2_t, l, off_q, off_kv, row2(mla_q_norm_g[l]),
                                row2(mla_kv_norm_g[l]), cos, sin)

        us, v_ln = _gmlp(uvg, ws2[l], bs2[l], row2(gm_ln_g[l]), row2(gm_ln_b[l]), TP)

        cv_args = (cv_w_dw[l], row2(cv_b_dw[l]), row2(cv_ln_g[l]), row2(cv_ln_b[l]))
        yb_p = _conv(glu, glu, 0, B, S, *cv_args, zero_first=True)
        yb_s = _conv(glu, hist_s[l], TP, NB, L, *cv_args, zero_first=False)

        qt_p = _qproj_t(cqn, wq3t[l], cos_t, sin_t, B, S, scale_log2e)
        q_s = _qproj(cqn, wq3[l], cos, sin, TP, NB, L)
        k_p, vt_p = _kvproj(ckv, kr, wkv3[l], B, S)
        o_p = _attn_prompt(qt_p, k_p, vt_p)
        o_s = _attn_sample(q_s, cache_mla_latent, cache_kr_t, l, ckv, kr, wkv3[l], TP)

        merged = _merge(xb, us, yb_p, yb_s, o_p, o_s, gm_w_p_b, cv_w_pw_b, w_o_b, wg_t, row2(b_gate[l]), l)
        x, xb = _outproj_ln(merged, w_out_b, l, x, row2(ln1_g[l]), row2(ln1_b[l]), alpha)

        j = l // 2
        if l % 2 == 0:
            x, xb = _ffn_dense(xb, eid_dense, nu_dense, ffn_w1_b, ffn_w3_b, ffn_w2_b, j, tm_dense,
                               x, row2(ln2_g[l]), row2(ln2_b[l]), alpha)
        else:
            route = _router(x, router_pad[j], n_experts)
            slot_tok, block_e, n_used, dest = _moe_plan(route, n_experts, MOE_TM)
            y = _ffn_moe(x, slot_tok, block_e, n_used, moe_w1_b, moe_w3_b, moe_w2_b, j, MOE_TM)
            x, xb = _combine(x, y, dest, route, row2(ln2_g[l]), row2(ln2_b[l]), alpha)

        outs["lat_p"].append(ckv[:TP].reshape(B, S, RKV))
        outs["kr_p"].append(kr[:TP].reshape(B, S, QK_ROPE))
        outs["conv_p"].append(jnp.stack([glu[(s + 1) * S - (CV_KERNEL - 1):(s + 1) * S] for s in range(B)]))
        outs["lat_s"].append(ckv[TP:].reshape(NB, L, RKV))
        outs["kr_s"].append(kr[TP:].reshape(NB, L, QK_ROPE))
        outs["conv_s"].append(glu[TP:].reshape(NB, L, CW)[:, L - (CV_KERNEL - 1):])
        outs["v_s"].append(v_ln[TP:].reshape(NB, L, GW))

    return (x[:TP].reshape(B, S, D), x[TP:].reshape(NB, L, D),
            jnp.stack(outs["lat_p"]), jnp.stack(outs["kr_p"]), jnp.stack(outs["conv_p"]),
            jnp.stack(outs["lat_s"]), jnp.stack(outs["kr_s"]), jnp.stack(outs["conv_s"]),
            jnp.stack(outs["v_s"]))
```

```python
import functools
import math

import jax
import jax.numpy as jnp
from jax import lax
from jax.experimental import pallas as pl
from jax.experimental.pallas import tpu as pltpu

F32 = jnp.float32
BF16 = jnp.bfloat16

CHUNK = 64
GM_GROUPS = 8
GM_CHUNK = 128
CV_KERNEL = 31
N_HEADS = 16
QK_NOPE = 128
QK_ROPE = 64
V_DIM = 128
QK_PAD = 256
V_ONES = 16
ROPE_THETA = 10000.0
TOP_K = 2
LN_EPS = 1e-5
RMS_EPS = 1e-6
ATTN_SCALE = (QK_NOPE + QK_ROPE) ** -0.5

V7X_VMEM_LIMIT_BYTES = 56 * 1024 * 1024
LANES = 128
SUBLANES = 8
CONV_HALO = 32
MOE_TM = 512
FFN_CHUNKS = 2
GATHER_UNROLL = 8


def _cp(*sem):
    return pltpu.CompilerParams(dimension_semantics=sem,
                                vmem_limit_bytes=V7X_VMEM_LIMIT_BYTES)


def _tile(n, pref, mult=8):
    if n <= pref:
        return n
    for t in range(pref, 0, -1):
        if n % t == 0 and t % mult == 0:
            return t
    return n


def _ln(x, g, b):
    mu = jnp.mean(x, axis=-1, keepdims=True)
    xc = x - mu
    var = jnp.mean(xc * xc, axis=-1, keepdims=True)
    return xc * lax.rsqrt(var + LN_EPS) * g + b


def _rms(x, g):
    ms = jnp.mean(x * x, axis=-1, keepdims=True)
    return x * lax.rsqrt(ms + RMS_EPS) * g


def _dot(a, b):
    return jnp.dot(a, b, preferred_element_type=F32)


def _dot_nt(a, b):
    return lax.dot_general(a, b, (((1,), (1,)), ((), ())), preferred_element_type=F32)


def _wt_spec(layer, row0, tn, K, jmap):
    assert row0 % tn == 0
    return pl.BlockSpec((1, tn, K), lambda *ij: (layer, row0 // tn + jmap(*ij), 0))


def _mm_act_kernel(x_ref, wt_ref, b_ref, o_ref, *, act):
    acc = _dot_nt(x_ref[...], wt_ref[0]) + b_ref[...]
    if act == "gelu":
        acc = jax.nn.gelu(acc)
    elif act == "sigmoid":
        acc = jax.nn.sigmoid(acc)
    o_ref[...] = acc.astype(o_ref.dtype)


def _mm_act(x, wt, layer, row0, N, b, act, out_dtype, tm=1024, tn=1024):
    M, K = x.shape
    tm, tn = _tile(M, tm), _tile(N, tn, LANES)
    return pl.pallas_call(
        functools.partial(_mm_act_kernel, act=act),
        grid=(M // tm, N // tn),
        in_specs=[pl.BlockSpec((tm, K), lambda i, j: (i, 0)),
                  _wt_spec(layer, row0, tn, K, lambda i, j: j),
                  pl.BlockSpec((1, tn), lambda i, j: (0, j))],
        out_specs=pl.BlockSpec((tm, tn), lambda i, j: (i, j)),
        out_shape=jax.ShapeDtypeStruct((M, N), out_dtype),
        compiler_params=_cp("parallel", "arbitrary"),
        name="mm_" + act,
    )(x, wt, b)


def _mm_glu_kernel(x_ref, wa_ref, wb_ref, o_ref):
    x = x_ref[...]
    o_ref[...] = _dot_nt(x, wa_ref[0]) * jax.nn.sigmoid(_dot_nt(x, wb_ref[0]))


def _mm_glu(x, wt, layer, row_a, row_b, N, tm=1024, tn=512):
    M, K = x.shape
    tm, tn = _tile(M, tm), _tile(N, tn, LANES)
    return pl.pallas_call(
        _mm_glu_kernel,
        grid=(M // tm, N // tn),
        in_specs=[pl.BlockSpec((tm, K), lambda i, j: (i, 0)),
                  _wt_spec(layer, row_a, tn, K, lambda i, j: j),
                  _wt_spec(layer, row_b, tn, K, lambda i, j: j)],
        out_specs=pl.BlockSpec((tm, tn), lambda i, j: (i, j)),
        out_shape=jax.ShapeDtypeStruct((M, N), F32),
        compiler_params=_cp("parallel", "arbitrary"),
        name="mm_glu",
    )(x, wt, wt)


def _latent_kernel(x_ref, wq_ref, wkv_ref, wkr_ref, gq_ref, gkv_ref, cos_ref, sin_ref,
                   cq_ref, ckv_ref, kr_ref):
    x = x_ref[...]
    cq_ref[...] = _rms(_dot_nt(x, wq_ref[0]), gq_ref[...]).astype(cq_ref.dtype)
    ckv_ref[...] = _rms(_dot_nt(x, wkv_ref[0]), gkv_ref[...])
    r = _dot_nt(x, wkr_ref[0])
    kr_ref[...] = r[:, :QK_ROPE] * cos_ref[...] + r[:, QK_ROPE:] * sin_ref[...]


def _latents(x, wt, wkr2t, layer, row_q, row_kv, gq, gkv, cos, sin, tm=512):
    M, K = x.shape
    Rq, Rkv = gq.shape[1], gkv.shape[1]
    tm = _tile(M, tm)
    row = lambda i: (i, 0)
    full = lambda i: (0, 0)
    return pl.pallas_call(
        _latent_kernel,
        grid=(M // tm,),
        in_specs=[pl.BlockSpec((tm, K), row),
                  _wt_spec(layer, row_q, Rq, K, lambda i: 0), _wt_spec(layer, row_kv, Rkv, K, lambda i: 0),
                  pl.BlockSpec((1, 2 * QK_ROPE, K), lambda i: (layer, 0, 0)),
                  pl.BlockSpec((1, Rq), full), pl.BlockSpec((1, Rkv), full),
                  pl.BlockSpec((tm, QK_ROPE), row), pl.BlockSpec((tm, QK_ROPE), row)],
        out_specs=[pl.BlockSpec((tm, Rq), row), pl.BlockSpec((tm, Rkv), row),
                   pl.BlockSpec((tm, QK_ROPE), row)],
        out_shape=[jax.ShapeDtypeStruct((M, Rq), BF16), jax.ShapeDtypeStruct((M, Rkv), F32),
                   jax.ShapeDtypeStruct((M, QK_ROPE), F32)],
        compiler_params=_cp("parallel"),
        name="latents",
    )(x, wt, wt, wkr2t, gq, gkv, cos, sin)


def _gmlp_kernel(uv_ref, ws_ref, bs_ref, g_ref, b_ref, us_ref, v_ref, *, n_chunks, groups):
    W = v_ref.shape[1]
    gd = W // groups
    vn = _ln(uv_ref[:, W:], g_ref[...], b_ref[...])
    v_ref[...] = vn
    vb = vn.astype(BF16)
    for c in range(n_chunks):
        r0 = c * GM_CHUNK
        for g in range(groups):
            c0 = g * gd
            s = _dot(ws_ref[0, g], vb[r0:r0 + GM_CHUNK, c0:c0 + gd]) + bs_ref[0, :, c0:c0 + gd]
            u = uv_ref[r0:r0 + GM_CHUNK, c0:c0 + gd]
            us_ref[r0:r0 + GM_CHUNK, c0:c0 + gd] = (u * s).astype(us_ref.dtype)


def _gmlp(uvg, ws2, bs2, g, b, n_prompt_rows, tm=512):
    M, W2 = uvg.shape
    W = W2 // 2
    tm = _tile(math.gcd(n_prompt_rows, M - n_prompt_rows), tm, GM_CHUNK)
    npt = n_prompt_rows // tm
    sel = lambda i: (jnp.minimum(i // npt, 1), 0, 0, 0)
    sel3 = lambda i: (jnp.minimum(i // npt, 1), 0, 0)
    row = lambda i: (i, 0)
    full = lambda i: (0, 0)
    return pl.pallas_call(
        functools.partial(_gmlp_kernel, n_chunks=tm // GM_CHUNK, groups=GM_GROUPS),
        grid=(M // tm,),
        in_specs=[pl.BlockSpec((tm, W2), row),
                  pl.BlockSpec((1, GM_GROUPS, GM_CHUNK, GM_CHUNK), sel),
                  pl.BlockSpec((1, GM_CHUNK, W), sel3),
                  pl.BlockSpec((1, W), full), pl.BlockSpec((1, W), full)],
        out_specs=[pl.BlockSpec((tm, W), row), pl.BlockSpec((tm, W), row)],
        out_shape=[jax.ShapeDtypeStruct((M, W), BF16), jax.ShapeDtypeStruct((M, W), F32)],
        compiler_params=_cp("parallel"),
        name="gmlp",
    )(uvg, ws2, bs2, g, b)


def _conv_kernel(cur_ref, halo_ref, w_ref, bdw_ref, g_ref, b_ref, o_ref, full_ref, sh_ref, acc_ref,
                 *, tm, zero_first):
    C = cur_ref.shape[1]
    halo = halo_ref[...].reshape(CONV_HALO, C)
    if zero_first:
        halo = jnp.where(pl.program_id(1) == 0, 0.0, halo)
    full_ref[0:CONV_HALO, :] = halo
    full_ref[CONV_HALO:CONV_HALO + tm, :] = cur_ref[...]
    off = CONV_HALO - (CV_KERNEL - 1)
    ns = sh_ref.shape[1]

    def lane_block(c, carry):
        c0 = pl.multiple_of(c * LANES, LANES)
        sh_ref[0] = full_ref[:, pl.ds(c0, LANES)]
        for r in range(1, SUBLANES):
            sh_ref[r, :ns - SUBLANES, :] = full_ref[pl.ds(r, ns - SUBLANES), pl.ds(c0, LANES)]
        wk = w_ref[:, pl.ds(c0, LANES)]
        wrows = [jnp.broadcast_to(wk[k:k + 1, :], (SUBLANES, LANES)) for k in range(CV_KERNEL)]
        for rb in range(tm // SUBLANES):
            acc = jnp.zeros((SUBLANES, LANES), F32)
            for k in range(CV_KERNEL):
                a, r = divmod(off + k, SUBLANES)
                acc = acc + sh_ref[r, pl.ds((rb + a) * SUBLANES, SUBLANES), :] * wrows[k]
            acc_ref[pl.ds(rb * SUBLANES, SUBLANES), pl.ds(c0, LANES)] = acc
        return carry

    lax.fori_loop(0, C // LANES, lane_block, 0)
    y = _ln(acc_ref[...] + bdw_ref[...], g_ref[...], b_ref[...])
    o_ref[...] = (y * jax.nn.sigmoid(y)).astype(o_ref.dtype)


def _conv(glu, halo_src, row0, n_seq, L, w_dw, b_dw, g, b, zero_first, tm=256):
    C = glu.shape[1]
    tm = _tile(L, tm, CONV_HALO)
    lt = L // tm
    hb = tm // CONV_HALO
    assert row0 % tm == 0
    r0 = row0 // tm
    if zero_first:
        halo_spec = pl.BlockSpec((CONV_HALO, C), lambda s, i: (jnp.maximum((r0 + s * lt + i) * hb - 1, 0), 0))
    else:
        halo_spec = pl.BlockSpec((1, CONV_HALO, C), lambda s, i: (s, 0, 0))
    full = lambda s, i: (0, 0)
    return pl.pallas_call(
        functools.partial(_conv_kernel, tm=tm, zero_first=zero_first),
        grid=(n_seq, lt),
        in_specs=[pl.BlockSpec((tm, C), lambda s, i: (r0 + s * lt + i, 0)),
                  halo_spec,
                  pl.BlockSpec((CV_KERNEL, C), full),
                  pl.BlockSpec((1, C), full), pl.BlockSpec((1, C), full), pl.BlockSpec((1, C), full)],
        out_specs=pl.BlockSpec((tm, C), lambda s, i: (s * lt + i, 0)),
        out_shape=jax.ShapeDtypeStruct((n_seq * L, C), BF16),
        scratch_shapes=[pltpu.VMEM((CONV_HALO + tm, C), F32),
                        pltpu.VMEM((SUBLANES, CONV_HALO + tm, LANES), F32),
                        pltpu.VMEM((tm, C), F32)],
        compiler_params=_cp("parallel", "arbitrary"),
        name="conv_prompt" if zero_first else "conv_sample",
    )(glu, halo_src, w_dw, b_dw, g, b)


def _qproj_kernel(cq_ref, w_ref, cos_ref, sin_ref, q_ref, *, hg, nb, L):
    cq = cq_ref[...]
    cos, sin = cos_ref[...], sin_ref[...]
    for h in range(hg):
        r = _dot(cq, w_ref[h])
        nope = r[:, :QK_NOPE] * ATTN_SCALE
        rp = (r[:, QK_NOPE:QK_NOPE + QK_ROPE] * cos + r[:, QK_NOPE + QK_ROPE:] * sin) * ATTN_SCALE
        if nb == 1:
            q_ref[0, h, :, :QK_NOPE] = nope.astype(q_ref.dtype)
            q_ref[0, h, :, QK_NOPE:] = rp.astype(q_ref.dtype)
        else:
            q_ref[:, h, :, :QK_NOPE] = nope.reshape(nb, L, QK_NOPE).astype(q_ref.dtype)
            q_ref[:, h, :, QK_NOPE:] = rp.reshape(nb, L, QK_ROPE).astype(q_ref.dtype)


def _qproj(cqn, wq3, cos, sin, row0, n_seq, L, hg=4, tm=512):
    R = cqn.shape[1]
    H = wq3.shape[0]
    hg = _tile(H, hg, 1)
    dq = QK_NOPE + QK_ROPE
    if L >= tm:
        tm = _tile(L, tm)
        nb, lt = 1, L // tm
        out_spec = pl.BlockSpec((1, hg, tm, dq), lambda i, h: (i // lt, h, i % lt, 0))
    else:
        nb = _tile(n_seq, max(tm // L, 1), 1)
        tm = nb * L
        out_spec = pl.BlockSpec((nb, hg, L, dq), lambda i, h: (i, h, 0, 0))
    assert row0 % tm == 0
    r0 = row0 // tm
    row = lambda i, h: (r0 + i, 0)
    return pl.pallas_call(
        functools.partial(_qproj_kernel, hg=hg, nb=nb, L=L),
        grid=(n_seq * L // tm, H // hg),
        in_specs=[pl.BlockSpec((tm, R), row),
                  pl.BlockSpec((hg, R, wq3.shape[2]), lambda i, h: (h, 0, 0)),
                  pl.BlockSpec((tm, QK_ROPE), row), pl.BlockSpec((tm, QK_ROPE), row)],
        out_specs=out_spec,
        out_shape=jax.ShapeDtypeStruct((n_seq, H, L, dq), BF16),
        compiler_params=_cp("parallel", "arbitrary"),
        name="qproj",
    )(cqn, wq3, cos, sin)


def _qproj_t_kernel(cq_ref, w_ref, cos_ref, sin_ref, q_ref, *, hg, scale):
    cq = cq_ref[...]
    cos, sin = cos_ref[...], sin_ref[...]
    for h in range(hg):
        r = _dot_nt(w_ref[h], cq)
        rp = r[QK_NOPE:QK_NOPE + QK_ROPE] * cos + r[QK_NOPE + QK_ROPE:] * sin
        q_ref[0, h, :QK_NOPE, :] = (r[:QK_NOPE] * scale).astype(q_ref.dtype)
        q_ref[0, h, QK_NOPE:QK_NOPE + QK_ROPE, :] = (rp * scale).astype(q_ref.dtype)
        q_ref[0, h, QK_NOPE + QK_ROPE:, :] = jnp.zeros((QK_PAD - QK_NOPE - QK_ROPE, cq.shape[0]), q_ref.dtype)


def _qproj_t(cqn, wq3t, cos_t, sin_t, n_seq, L, scale, hg=8, tm=512):
    R = cqn.shape[1]
    H = wq3t.shape[0]
    hg = _tile(H, hg, 1)
    tm = _tile(L, tm, LANES)
    lt = L // tm
    dq = QK_PAD
    return pl.pallas_call(
        functools.partial(_qproj_t_kernel, hg=hg, scale=scale),
        grid=(n_seq * lt, H // hg),
        in_specs=[pl.BlockSpec((tm, R), lambda i, h: (i, 0)),
                  pl.BlockSpec((hg, wq3t.shape[1], R), lambda i, h: (h, 0, 0)),
                  pl.BlockSpec((QK_ROPE, tm), lambda i, h: (0, i)),
                  pl.BlockSpec((QK_ROPE, tm), lambda i, h: (0, i))],
        out_specs=pl.BlockSpec((1, hg, dq, tm), lambda i, h: (i // lt, h, 0, i % lt)),
        out_shape=jax.ShapeDtypeStruct((n_seq, H, dq, L), BF16),
        compiler_params=_cp("parallel", "arbitrary"),
        name="qproj_t",
    )(cqn, wq3t, cos_t, sin_t)


def _kvproj_kernel(ckv_ref, kr_ref, w_ref, k_ref, vt_ref, *, hg):
    ckv = ckv_ref[...].astype(BF16)
    kr = kr_ref[...].astype(BF16)
    for h in range(hg):
        r = _dot(ckv, w_ref[h])
        k_ref[0, h, :, :QK_NOPE] = r[:, :QK_NOPE].astype(k_ref.dtype)
        k_ref[0, h, :, QK_NOPE:QK_NOPE + QK_ROPE] = kr
        k_ref[0, h, :, QK_NOPE + QK_ROPE:] = jnp.zeros((kr.shape[0], QK_PAD - QK_NOPE - QK_ROPE), k_ref.dtype)
        vt_ref[0, h, :V_DIM, :] = r[:, QK_NOPE:].T.astype(vt_ref.dtype)
        vt_ref[0, h, V_DIM:, :] = jnp.ones((V_ONES, ckv.shape[0]), vt_ref.dtype)


def _kvproj(ckv, kr, wkv3, n_seq, L, hg=8, tm=512):
    R = ckv.shape[1]
    H = wkv3.shape[0]
    hg = _tile(H, hg, 1)
    tm = _tile(L, tm, LANES)
    lt = L // tm
    dq = QK_PAD
    row = lambda i, h: (i, 0)
    wmap = lambda i, h: (h, 0, 0)
    return pl.pallas_call(
        functools.partial(_kvproj_kernel, hg=hg),
        grid=(n_seq * lt, H // hg),
        in_specs=[pl.BlockSpec((tm, R), row), pl.BlockSpec((tm, QK_ROPE), row),
                  pl.BlockSpec((hg, R, wkv3.shape[2]), wmap)],
        out_specs=[pl.BlockSpec((1, hg, tm, dq), lambda i, h: (i // lt, h, i % lt, 0)),
                   pl.BlockSpec((1, hg, V_DIM + V_ONES, tm), lambda i, h: (i // lt, h, 0, i % lt))],
        out_shape=[jax.ShapeDtypeStruct((n_seq, H, L, dq), BF16),
                   jax.ShapeDtypeStruct((n_seq, H, V_DIM + V_ONES, L), BF16)],
        compiler_params=_cp("parallel", "arbitrary"),
        name="kvproj",
    )(ckv, kr, wkv3)


def _attn_prompt_kernel(qt_ref, k_ref, vt_ref, o_ref, s_ref, p_ref, acc_ref, *, tq, tk, hg):
    qi = pl.program_id(2)
    d0 = pl.multiple_of(qi * tq, tq)
    d1 = pl.multiple_of(d0 + tk, tk)
    all_q = slice(0, tq)
    late_q = slice(tk, tq)

    def scores(g, j0, slot, cols=all_q):
        s = _dot(k_ref[0, g, pl.ds(j0, tk), :], qt_ref[0, g, :, cols])
        s_ref[slot, g, :, cols] = s
        return jnp.max(s, axis=0, keepdims=True)

    def softmax(g, slot, mx, m, mask=None, cols=all_q):
        s = s_ref[slot, g, :, cols]
        if mask is not None:
            s = jnp.where(mask, s, -jnp.inf)
            mx = jnp.max(s, axis=0, keepdims=True)
        m_new = jnp.maximum(m, mx)
        p_ref[slot, g, :, cols] = jnp.exp2(s - m_new).astype(BF16)
        return jnp.exp2(m - m_new), m_new

    def accumulate(g, j0, slot, alpha, cols=all_q):
        acc_ref[g, :, cols] = (alpha * acc_ref[g, :, cols]
                               + _dot(vt_ref[0, g, :, pl.ds(j0, tk)], p_ref[slot, g, :, cols]))

    def step(j0, slot, state, mask=None, next_cols=all_q):
        jp = pl.multiple_of(jnp.maximum(j0 - tk, 0), tk)
        new = []
        for g in range(hg):
            mx, a_prev, m = state[g]
            mx_next = scores(g, pl.multiple_of(j0 + tk, tk), 1 - slot, next_cols)
            accumulate(g, jp, 1 - slot, a_prev)
            alpha, m = softmax(g, slot, mx, m, mask)
            new.append((mx_next, alpha, m))
        return tuple(new)

    def pair(i, state):
        j0 = pl.multiple_of(2 * i * tk, tk)
        state = step(j0, 0, state)
        return step(pl.multiple_of(j0 + tk, tk), 1, state)

    p_ref[1] = jnp.zeros_like(p_ref[1])
    acc_ref[...] = jnp.zeros_like(acc_ref)
    init = tuple((scores(g, 0, 0), jnp.ones((1, tq), F32), jnp.full((1, tq), -jnp.inf, F32))
                 for g in range(hg))
    n_pairs = qi * (tq // (2 * tk))
    odd = lax.rem(n_pairs, 2)
    state = lax.cond(odd == 1, lambda st: pair(0, st), lambda st: st, init)
    state = lax.fori_loop(0, n_pairs // 2, lambda i, st: pair(2 * i + odd + 1, pair(2 * i + odd, st)), state)

    key_chunk = lax.broadcasted_iota(jnp.int32, (tk, tq), 0) // CHUNK
    qry_chunk = lax.broadcasted_iota(jnp.int32, (tk, tq), 1) // CHUNK
    visible = key_chunk <= qry_chunk
    state = step(d0, 0, state, mask=visible, next_cols=late_q)
    for g in range(hg):
        _, a_prev, m = state[g]
        accumulate(g, d0, 0, a_prev)
        alpha, _ = softmax(g, 1, None, m[:, late_q], visible[:, :tq - tk], late_q)
        accumulate(g, d1, 1, alpha, late_q)
        o = acc_ref[g, :V_DIM, :] / acc_ref[g, V_DIM:V_DIM + 1, :]
        o_ref[:, g * V_DIM:(g + 1) * V_DIM] = o.T.astype(o_ref.dtype)


def _attn_prompt(qt, k, vt, tq=512, hg=2):
    B, H, dq, S = qt.shape
    tq = _tile(S, tq, 2 * LANES)
    tk = tq // 2
    hg = _tile(H, hg, 1)
    return pl.pallas_call(
        functools.partial(_attn_prompt_kernel, tq=tq, tk=tk, hg=hg),
        grid=(B, H // hg, S // tq),
        in_specs=[pl.BlockSpec((1, hg, dq, tq), lambda b, h, i: (b, h, 0, i)),
                  pl.BlockSpec((1, hg, S, dq), lambda b, h, i: (b, h, 0, 0)),
                  pl.BlockSpec((1, hg, vt.shape[2], S), lambda b, h, i: (b, h, 0, 0))],
        out_specs=pl.BlockSpec((tq, hg * V_DIM), lambda b, h, i: (b * (S // tq) + i, h)),
        out_shape=jax.ShapeDtypeStruct((B * S, H * V_DIM), BF16),
        scratch_shapes=[pltpu.VMEM((2, hg, tk, tq), F32), pltpu.VMEM((2, hg, tk, tq), BF16),
                        pltpu.VMEM((hg, vt.shape[2], tq), F32)],
        compiler_params=_cp("parallel", "parallel", "arbitrary"),
        name="attn_prompt",
    )(qt, k, vt)


def _attn_sample_kernel(q_ref, plat_ref, pkrt_ref, nlat_ref, nkr_ref, w_ref, o_ref, ql_ref, qr_ref,
                        *, H, L, kc):
    P = plat_ref.shape[2]
    for h in range(H):
        qh = q_ref[0, h]
        ql_ref[h * L:(h + 1) * L, :] = _dot_nt(qh[:, :QK_NOPE], w_ref[h, :, :QK_NOPE]).astype(BF16)
        qr_ref[h * L:(h + 1) * L, :] = qh[:, QK_NOPE:]
    ql = ql_ref[...]
    qr = qr_ref[...]

    def step(lat, s_rope, carry):
        m, l, acc = carry
        s = _dot_nt(ql, lat) + s_rope
        m_new = jnp.maximum(m, jnp.max(s, axis=-1, keepdims=True))
        alpha = jnp.exp(m - m_new)
        p = jnp.exp(s - m_new)
        l = alpha * l + jnp.sum(p, axis=-1, keepdims=True)
        acc = alpha * acc + _dot(p.astype(BF16), lat)
        return m_new, l, acc

    R = ql.shape[1]
    carry = (jnp.full((H * L, 1), -jnp.inf, F32), jnp.zeros((H * L, 1), F32), jnp.zeros((H * L, R), F32))
    for c in range(P // kc):
        krt = pkrt_ref[0, 0, :, c * kc:(c + 1) * kc].astype(BF16)
        carry = step(plat_ref[0, 0, c * kc:(c + 1) * kc, :].astype(BF16), _dot(qr, krt), carry)
    m, l, acc = step(nlat_ref[...].astype(BF16), _dot_nt(qr, nkr_ref[...].astype(BF16)), carry)
    ol = (acc / l).astype(BF16)
    for h in range(H):
        o_ref[:, h * V_DIM:(h + 1) * V_DIM] = _dot(ol[h * L:(h + 1) * L, :], w_ref[h, :, QK_NOPE:]).astype(o_ref.dtype)


def _attn_sample(q, cache_lat, cache_kr_t, layer, ckv, kr, wkv3, row0):
    n_req, H, L, dq = q.shape
    P, R = cache_lat.shape[2], cache_lat.shape[3]
    kc = _tile(P, 1024)
    assert row0 % L == 0
    r0 = row0 // L
    return pl.pallas_call(
        functools.partial(_attn_sample_kernel, H=H, L=L, kc=kc),
        grid=(n_req,),
        in_specs=[pl.BlockSpec((1, H, L, dq), lambda b: (b, 0, 0, 0)),
                  pl.BlockSpec((1, 1, P, R), lambda b: (layer, b, 0, 0)),
                  pl.BlockSpec((1, 1, QK_ROPE, P), lambda b: (layer, b, 0, 0)),
                  pl.BlockSpec((L, R), lambda b: (r0 + b, 0)),
                  pl.BlockSpec((L, QK_ROPE), lambda b: (r0 + b, 0)),
                  pl.BlockSpec(wkv3.shape, lambda b: (0, 0, 0))],
        out_specs=pl.BlockSpec((L, H * V_DIM), lambda b: (b, 0)),
        out_shape=jax.ShapeDtypeStruct((n_req * L, H * V_DIM), BF16),
        scratch_shapes=[pltpu.VMEM((H * L, R), BF16), pltpu.VMEM((H * L, QK_ROPE), BF16)],
        compiler_params=_cp("parallel"),
        name="attn_sample",
    )(q, cache_lat, cache_kr_t, ckv, kr, wkv3)


def _merge_kernel(x_ref, us_ref, ybp_ref, ybs_ref, op_ref, os_ref, wp_ref, wpw_ref, wo_ref,
                  wga_ref, wgb_ref, wgc_ref, ba_ref, bb_ref, bc_ref, out_ref, *, npt):
    prompt = pl.program_id(0) < npt
    x = x_ref[...]
    yb_in = jnp.where(prompt, ybp_ref[...], ybs_ref[...])
    o_in = jnp.where(prompt, op_ref[...], os_ref[...])
    out = jax.nn.sigmoid(_dot_nt(x, wga_ref[0]) + ba_ref[...]) * _dot(us_ref[...], wp_ref[0])
    out += jax.nn.sigmoid(_dot_nt(x, wgb_ref[0]) + bb_ref[...]) * _dot(yb_in, wpw_ref[0])
    out += jax.nn.sigmoid(_dot_nt(x, wgc_ref[0]) + bc_ref[...]) * _dot(o_in, wo_ref[0])
    out_ref[...] = out.astype(out_ref.dtype)


def _merge(xb, us, yb_p, yb_s, o_p, o_s, wp, wpw, wo, wgt, bg, layer, tm=512, tn=512):
    M = us.shape[0]
    D = wp.shape[2]
    TP, TS = yb_p.shape[0], yb_s.shape[0]
    tm, tn = _tile(math.gcd(TP, TS), tm), _tile(D, tn, LANES)
    npt = TP // tm
    nj = D // tn
    row = lambda i, j: (i, 0)
    prow = lambda i, j: (jnp.minimum(i, npt - 1), 0)
    srow = lambda i, j: (jnp.maximum(i - npt, 0), 0)
    col = lambda i, j: (layer, 0, j)
    gcol = lambda g: (lambda i, j: (0, g * nj + j))
    grow = lambda g: _wt_spec(layer, g * D, tn, wgt.shape[2], lambda i, j: j)
    return pl.pallas_call(
        functools.partial(_merge_kernel, npt=npt),
        grid=(M // tm, nj),
        in_specs=[pl.BlockSpec((tm, xb.shape[1]), row), pl.BlockSpec((tm, us.shape[1]), row),
                  pl.BlockSpec((tm, yb_p.shape[1]), prow), pl.BlockSpec((tm, yb_s.shape[1]), srow),
                  pl.BlockSpec((tm, o_p.shape[1]), prow), pl.BlockSpec((tm, o_s.shape[1]), srow),
                  pl.BlockSpec((1, wp.shape[1], tn), col), pl.BlockSpec((1, wpw.shape[1], tn), col),
                  pl.BlockSpec((1, wo.shape[1], tn), col),
                  grow(0), grow(1), grow(2),
                  pl.BlockSpec((1, tn), gcol(0)), pl.BlockSpec((1, tn), gcol(1)), pl.BlockSpec((1, tn), gcol(2))],
        out_specs=pl.BlockSpec((tm, tn), lambda i, j: (i, j)),
        out_shape=jax.ShapeDtypeStruct((M, D), BF16),
        compiler_params=_cp("parallel", "arbitrary"),
        name="merge",
    )(xb, us, yb_p, yb_s, o_p, o_s, wp, wpw, wo, wgt, wgt, wgt, bg, bg, bg)


def _outproj_kernel(m_ref, w_ref, x_ref, g_ref, b_ref, o_ref, ob_ref, *, alpha):
    y = _ln(alpha * x_ref[...] + _dot(m_ref[...], w_ref[0]), g_ref[...], b_ref[...])
    o_ref[...] = y
    ob_ref[...] = y.astype(ob_ref.dtype)


def _outproj_ln(merged, w, layer, x, g, b, alpha, tm=512):
    M, D = x.shape
    tm = _tile(M, tm)
    row = lambda i: (i, 0)
    full = lambda i: (0, 0)
    return pl.pallas_call(
        functools.partial(_outproj_kernel, alpha=alpha),
        grid=(M // tm,),
        in_specs=[pl.BlockSpec((tm, merged.shape[1]), row),
                  pl.BlockSpec((1,) + w.shape[1:], lambda i: (layer, 0, 0)),
                  pl.BlockSpec((tm, D), row), pl.BlockSpec((1, D), full), pl.BlockSpec((1, D), full)],
        out_specs=[pl.BlockSpec((tm, D), row), pl.BlockSpec((tm, D), row)],
        out_shape=[jax.ShapeDtypeStruct((M, D), F32), jax.ShapeDtypeStruct((M, D), BF16)],
        compiler_params=_cp("parallel"),
        name="outproj_ln",
    )(merged, w, x, g, b)


def _swiglu_partial(x, w1_ref, w3_ref, w2_ref):
    tf = w1_ref.shape[3]
    n_chunks = FFN_CHUNKS if tf % (FFN_CHUNKS * LANES) == 0 else 1
    c = tf // n_chunks
    hs = []
    for k in range(n_chunks):
        cols = slice(k * c, (k + 1) * c)
        h = jax.nn.silu(_dot(x, w1_ref[0, 0, :, cols])) * _dot(x, w3_ref[0, 0, :, cols])
        hs.append(h.astype(BF16))
    out = _dot(hs[0], w2_ref[0, 0, 0:c, :])
    for k in range(1, n_chunks):
        out += _dot(hs[k], w2_ref[0, 0, k * c:(k + 1) * c, :])
    return out


def _ffn_dense_kernel(eid_ref, nu_ref, x_ref, w1_ref, w3_ref, w2_ref, r_ref, g_ref, b_ref,
                      o_ref, ob_ref, acc_ref, *, alpha):
    f = pl.program_id(1)

    @pl.when(f == 0)
    def _():
        acc_ref[...] = jnp.zeros_like(acc_ref)

    acc_ref[...] += _swiglu_partial(x_ref[...], w1_ref, w3_ref, w2_ref)

    @pl.when(f == pl.num_programs(1) - 1)
    def _():
        y = _ln(alpha * r_ref[...] + acc_ref[...], g_ref[...], b_ref[...])
        o_ref[...] = y
        ob_ref[...] = y.astype(ob_ref.dtype)


def _ffn_moe_kernel(eid_ref, nu_ref, idx0_ref, idxn_ref, x_hbm, w1_ref, w3_ref, w2_ref, o_ref,
                    xg_ref, xb_ref, sem, *, tm):
    i = pl.program_id(0)
    f = pl.program_id(1)
    n_used = nu_ref[0]

    def row_copy(t, r):
        return pltpu.make_async_copy(x_hbm.at[pl.ds(t, 1), :], xg_ref.at[pl.ds(r, 1), :], sem)

    def start_rows(idx_ref):
        def body(c, carry):
            for u in range(GATHER_UNROLL):
                r = c * GATHER_UNROLL + u
                row_copy(idx_ref[0, 0, r], r).start(priority=u % 2)
            return carry
        lax.fori_loop(0, tm // GATHER_UNROLL, body, 0)

    def wait_rows():
        def body(c, carry):
            for u in range(GATHER_UNROLL):
                row_copy(0, c * GATHER_UNROLL + u).wait()
            return carry
        lax.fori_loop(0, tm // GATHER_UNROLL, body, 0)

    @pl.when(f == 0)
    def _():
        o_ref[...] = jnp.zeros_like(o_ref)

    @pl.when((f == 0) & (i < n_used))
    def _():
        @pl.when(i == 0)
        def _():
            start_rows(idx0_ref)

        wait_rows()
        xb_ref[...] = xg_ref[...].astype(BF16)

    @pl.when((f == jnp.minimum(1, pl.num_programs(1) - 1)) & (i + 1 < n_used))
    def _():
        start_rows(idxn_ref)

    @pl.when(i < n_used)
    def _():
        o_ref[...] += _swiglu_partial(xb_ref[...], w1_ref, w3_ref, w2_ref)


def _ffn_weight_specs(layer, D, tf):
    last = lambda i, nu: jnp.minimum(i, nu[0] - 1)
    w13 = lambda i, f, e, nu: (layer, e[last(i, nu)], 0, jnp.where(i < nu[0], f, 0))
    w2m = lambda i, f, e, nu: (layer, e[last(i, nu)], jnp.where(i < nu[0], f, 0), 0)
    return [pl.BlockSpec((1, 1, D, tf), w13), pl.BlockSpec((1, 1, D, tf), w13), pl.BlockSpec((1, 1, tf, D), w2m)]


def _ffn_dense(x, eid, n_used, w1, w3, w2, layer, tm, resid, g, b, alpha, tf=512):
    M, D = x.shape
    Fdim = w1.shape[3]
    tf = _tile(Fdim, tf, LANES)
    row = lambda i, f, e, nu: (i, 0)
    full = lambda i, f, e, nu: (0, 0)
    return pl.pallas_call(
        functools.partial(_ffn_dense_kernel, alpha=alpha),
        grid_spec=pltpu.PrefetchScalarGridSpec(
            num_scalar_prefetch=2, grid=(M // tm, Fdim // tf),
            in_specs=[pl.BlockSpec((tm, D), row)] + _ffn_weight_specs(layer, D, tf)
            + [pl.BlockSpec((tm, D), row), pl.BlockSpec((1, D), full), pl.BlockSpec((1, D), full)],
            out_specs=[pl.BlockSpec((tm, D), row), pl.BlockSpec((tm, D), row)],
            scratch_shapes=[pltpu.VMEM((tm, D), F32)]),
        out_shape=[jax.ShapeDtypeStruct((M, D), F32), jax.ShapeDtypeStruct((M, D), BF16)],
        compiler_params=_cp("arbitrary", "arbitrary"),
        name="ffn_dense",
    )(eid, n_used, x, w1, w3, w2, resid, g, b)


def _ffn_moe(x, slot_tok, eid, n_used, w1, w3, w2, layer, tm, tf=512):
    D = x.shape[1]
    n_slots = slot_tok.shape[0]
    nb = n_slots // tm
    Fdim = w1.shape[3]
    tf = _tile(Fdim, tf, LANES)
    idx3 = slot_tok.reshape(nb, 1, tm)
    smem = functools.partial(pl.BlockSpec, (1, 1, tm), memory_space=pltpu.SMEM)
    return pl.pallas_call(
        functools.partial(_ffn_moe_kernel, tm=tm),
        grid_spec=pltpu.PrefetchScalarGridSpec(
            num_scalar_prefetch=2, grid=(nb, Fdim // tf),
            in_specs=[smem(lambda i, f, e, nu: (0, 0, 0)),
                      smem(lambda i, f, e, nu: (jnp.minimum(i + 1, nb - 1), 0, 0)),
                      pl.BlockSpec(memory_space=pl.ANY)] + _ffn_weight_specs(layer, D, tf),
            out_specs=pl.BlockSpec((tm, D), lambda i, f, e, nu: (i, 0)),
            scratch_shapes=[pltpu.VMEM((tm, D), F32), pltpu.VMEM((tm, D), BF16),
                            pltpu.SemaphoreType.DMA(())]),
        out_shape=jax.ShapeDtypeStruct((n_slots, D), F32),
        compiler_params=_cp("arbitrary", "arbitrary"),
        name="ffn_moe",
    )(eid, n_used, idx3, idx3, x, w1, w3, w2)


def _router_kernel(x_ref, r_ref, o_ref, *, n_experts):
    logits = jnp.dot(x_ref[...], r_ref[...], preferred_element_type=F32,
                     precision=lax.Precision.HIGHEST)
    col = lax.broadcasted_iota(jnp.int32, logits.shape, 1)
    lg = jnp.where(col < n_experts, logits, -jnp.inf)
    m1 = jnp.max(lg, axis=-1, keepdims=True)
    i1 = jnp.min(jnp.where(lg == m1, col, LANES), axis=-1, keepdims=True)
    lg2 = jnp.where(col == i1, -jnp.inf, lg)
    m2 = jnp.max(lg2, axis=-1, keepdims=True)
    i2 = jnp.min(jnp.where(lg2 == m2, col, LANES), axis=-1, keepdims=True)
    e = jnp.exp(m2 - m1)
    g1 = 1.0 / (1.0 + e)
    g2 = e * g1
    out = jnp.where(col == 0, i1.astype(F32),
                    jnp.where(col == 1, i2.astype(F32),
                              jnp.where(col == 2, g1, jnp.where(col == 3, g2, 0.0))))
    o_ref[...] = out


def _router(x, router_pad, n_experts, tm=512):
    M, D = x.shape
    tm = _tile(M, tm)
    return pl.pallas_call(
        functools.partial(_router_kernel, n_experts=n_experts),
        grid=(M // tm,),
        in_specs=[pl.BlockSpec((tm, D), lambda i: (i, 0)), pl.BlockSpec((D, LANES), lambda i: (0, 0))],
        out_specs=pl.BlockSpec((tm, LANES), lambda i: (i, 0)),
        out_shape=jax.ShapeDtypeStruct((M, LANES), F32),
        compiler_params=_cp("parallel"),
        name="router",
    )(x, router_pad)


def _combine_kernel(idx0_ref, idxn_ref, x_ref, r_ref, g_ref, b_ref, y_hbm, o_ref, ob_ref, yg_ref, sem,
                    *, tm, alpha):
    i = pl.program_id(0)
    slot = lax.rem(i, 2)

    def row_copy(t, k, r, s):
        return pltpu.make_async_copy(y_hbm.at[pl.ds(t, 1), :], yg_ref.at[s, k, pl.ds(r, 1), :], sem.at[s])

    def start_rows(idx_ref, s):
        def body(c, carry):
            for u in range(GATHER_UNROLL):
                r = c * GATHER_UNROLL + u
                for k in range(TOP_K):
                    row_copy(idx_ref[0, k, r], k, r, s).start(priority=(u + k) % 2)
            return carry
        lax.fori_loop(0, tm // GATHER_UNROLL, body, 0)

    def wait_rows(s):
        def body(c, carry):
            for u in range(GATHER_UNROLL):
                for k in range(TOP_K):
                    row_copy(0, k, c * GATHER_UNROLL + u, s).wait()
            return carry
        lax.fori_loop(0, tm // GATHER_UNROLL, body, 0)

    @pl.when(i == 0)
    def _():
        start_rows(idx0_ref, 0)

    wait_rows(slot)

    @pl.when(i + 1 < pl.num_programs(0))
    def _():
        start_rows(idxn_ref, 1 - slot)

    r = r_ref[...]
    y = alpha * x_ref[...] + r[:, 2:3] * yg_ref[slot, 0] + r[:, 3:4] * yg_ref[slot, 1]
    y = _ln(y, g_ref[...], b_ref[...])
    o_ref[...] = y
    ob_ref[...] = y.astype(ob_ref.dtype)


def _combine(x, y, dest, route, g, b, alpha, tm=256):
    M, D = x.shape
    tm = _tile(M, tm)
    nt = M // tm
    idx3 = jnp.transpose(dest.reshape(nt, tm, TOP_K), (0, 2, 1))
    smem = functools.partial(pl.BlockSpec, (1, TOP_K, tm), memory_space=pltpu.SMEM)
    row = lambda i: (i, 0)
    full = lambda i: (0, 0)
    return pl.pallas_call(
        functools.partial(_combine_kernel, tm=tm, alpha=alpha),
        grid=(nt,),
        in_specs=[smem(lambda i: (0, 0, 0)), smem(lambda i: (jnp.minimum(i + 1, nt - 1), 0, 0)),
                  pl.BlockSpec((tm, D), row), pl.BlockSpec((tm, LANES), row),
                  pl.BlockSpec((1, D), full), pl.BlockSpec((1, D), full),
                  pl.BlockSpec(memory_space=pl.ANY)],
        out_specs=[pl.BlockSpec((tm, D), row), pl.BlockSpec((tm, D), row)],
        out_shape=[jax.ShapeDtypeStruct((M, D), F32), jax.ShapeDtypeStruct((M, D), BF16)],
        scratch_shapes=[pltpu.VMEM((2, TOP_K, tm, D), F32), pltpu.SemaphoreType.DMA((2,))],
        compiler_params=_cp("arbitrary"),
        name="combine",
    )(idx3, idx3, x, route, g, b, y)


def _moe_plan(route, n_experts, tm):
    T = route.shape[0]
    n_assign = T * TOP_K
    e_flat = route[:, :TOP_K].astype(jnp.int32).reshape(-1)
    onehot = (e_flat[:, None] == jnp.arange(n_experts, dtype=jnp.int32)[None, :]).astype(jnp.int32)
    csum = jnp.cumsum(onehot, axis=0)
    rank = jnp.take_along_axis(csum, e_flat[:, None], axis=1)[:, 0] - 1
    counts = csum[-1]
    padded = ((counts + tm - 1) // tm) * tm
    pad_end = jnp.cumsum(padded)
    pad_start = pad_end - padded
    dest = (pad_start[e_flat] + rank).astype(jnp.int32)
    n_blocks = -(-n_assign // tm) + n_experts
    n_slots = n_blocks * tm
    slot_tok = jnp.zeros((n_slots,), jnp.int32).at[dest].set(jnp.arange(n_assign, dtype=jnp.int32) // TOP_K)
    block_e = jnp.minimum(jnp.searchsorted(pad_end, jnp.arange(n_blocks, dtype=jnp.int32) * tm, side="right"),
                          n_experts - 1).astype(jnp.int32)
    n_used = (pad_end[-1:] // tm).astype(jnp.int32)
    return slot_tok, block_e, n_used, dest.reshape(T, TOP_K)


def _rot_half_cols(w):
    half = w.shape[-1] // 2
    return jnp.concatenate([-w[..., half:], w[..., :half]], axis=-1)


def kernel(x_prompt, x_sample, cache_mla_latent, cache_mla_krope, cache_conv, w_in, b_gate, gm_ln_g, gm_ln_b, gm_w_s, gm_b_s, gm_w_p, cv_w_dw, cv_b_dw, cv_ln_g, cv_ln_b, cv_w_pw, mla_q_norm_g, mla_kv_norm_g, mla_w_uq, mla_w_uk, mla_w_uv, mla_w_o, w_out, ln1_g, ln1_b, ln2_g, ln2_b, ffn_w1, ffn_w3, ffn_w2, moe_router, moe_w1, moe_w3, moe_w2):
    B, S, D = x_prompt.shape
    NB, L, _ = x_sample.shape
    depth = w_in.shape[0]
    past = cache_mla_latent.shape[2]
    GW = gm_ln_g.shape[1]
    CW = cv_ln_g.shape[1]
    RQ = mla_q_norm_g.shape[1]
    RKV = mla_kv_norm_g.shape[1]
    H = N_HEADS
    n_experts = moe_router.shape[2]
    alpha = float((2 * depth) ** 0.25)
    TP, TS = B * S, NB * L
    T = TP + TS
    off_cv = 2 * GW
    off_q = off_cv + 2 * CW
    off_kv = off_q + RQ
    off_kr = off_kv + RKV
    off_g = off_kr + QK_ROPE
    assert L <= GM_CHUNK and GM_CHUNK % L == 0 and S % GM_CHUNK == 0 and TS % GM_CHUNK == 0
    assert L >= CV_KERNEL - 1 and L % 8 == 0 and CV_KERNEL - 1 <= CONV_HALO

    w_in_t = jnp.swapaxes(w_in, 1, 2).astype(BF16)
    wkr_t = w_in_t[:, off_kr:off_g, :]
    half_r = QK_ROPE // 2
    wkr2_t = jnp.concatenate([wkr_t, -wkr_t[:, half_r:], wkr_t[:, :half_r]], axis=1)
    wg_t = w_in_t[:, off_g:, :]
    gm_w_p_b, cv_w_pw_b = gm_w_p.astype(BF16), cv_w_pw.astype(BF16)
    w_o_b, w_out_b = mla_w_o.astype(BF16), w_out.astype(BF16)
    ffn_w1_b, ffn_w3_b, ffn_w2_b = (w.astype(BF16)[:, None] for w in (ffn_w1, ffn_w3, ffn_w2))
    moe_w1_b, moe_w3_b, moe_w2_b = moe_w1.astype(BF16), moe_w3.astype(BF16), moe_w2.astype(BF16)
    zero_uv = jnp.zeros((1, off_cv), F32)

    uq = mla_w_uq.reshape(depth, RQ, H, QK_NOPE + QK_ROPE)
    uq_rope = uq[..., QK_NOPE:]
    wq3 = jnp.concatenate([uq[..., :QK_NOPE], uq_rope, _rot_half_cols(uq_rope)], axis=-1)
    wq3 = jnp.transpose(wq3, (0, 2, 1, 3)).astype(BF16)
    wkv3 = jnp.concatenate([mla_w_uk.reshape(depth, RKV, H, QK_NOPE),
                            mla_w_uv.reshape(depth, RKV, H, V_DIM)], axis=-1)
    wkv3 = jnp.transpose(wkv3, (0, 2, 1, 3)).astype(BF16)
    wq3t = jnp.swapaxes(wq3, 2, 3)

    causal = jnp.tril(jnp.ones((GM_CHUNK, GM_CHUNK), bool))
    ws_p = jnp.where(causal, gm_w_s, 0.0)
    reps = GM_CHUNK // L
    ws_l = jnp.where(causal[:L, :L], gm_w_s[:, :, :L, :L], 0.0)
    ws_s = jnp.einsum("ab,lgij->lgaibj", jnp.eye(reps, dtype=F32), ws_l).reshape(depth, GM_GROUPS, GM_CHUNK, GM_CHUNK)
    ws2 = jnp.stack([ws_p, ws_s], axis=1).astype(BF16)
    gd = GW // GM_GROUPS
    bs_p = jnp.repeat(jnp.transpose(gm_b_s, (0, 2, 1)), gd, axis=2)
    bs_s = jnp.repeat(jnp.tile(jnp.transpose(gm_b_s[:, :, :L], (0, 2, 1)), (1, reps, 1)), gd, axis=2)
    bs2 = jnp.stack([bs_p, bs_s], axis=1)

    router_pad = jnp.pad(moe_router, ((0, 0), (0, 0), (0, LANES - n_experts)))

    half = QK_ROPE // 2
    inv = ROPE_THETA ** (-jnp.arange(half, dtype=F32) / half)
    pos = jnp.concatenate([jnp.tile(jnp.arange(S), B), jnp.tile(past + jnp.arange(L), NB)]).astype(F32)
    ang = pos[:, None] * inv[None, :]
    cos = jnp.tile(jnp.cos(ang), (1, 2))
    sin = jnp.tile(jnp.sin(ang), (1, 2))
    cos_t, sin_t = cos[:TP].T, sin[:TP].T
    scale_log2e = ATTN_SCALE * math.log2(math.e)

    cache_kr_t = jnp.swapaxes(cache_mla_krope, 2, 3)
    hist_s = jnp.pad(cache_conv, ((0, 0), (0, 0), (CONV_HALO - (CV_KERNEL - 1), 0), (0, 0)))

    x = jnp.concatenate([x_prompt.reshape(TP, D), x_sample.reshape(TS, D)], axis=0)
    xb = x.astype(BF16)
    row2 = lambda a: a.reshape(1, -1)

    tm_dense = _tile(T, 512)
    eid_dense = jnp.zeros((T // tm_dense,), jnp.int32)
    nu_dense = jnp.full((1,), T // tm_dense, jnp.int32)

    outs = {k: [] for k in ("lat_p", "kr_p", "conv_p", "lat_s", "kr_s", "conv_s", "v_s")}
    for l in range(depth):
        uvg = _mm_act(xb, w_in_t, l, 0, off_cv, zero_uv, "gelu", F32)
        glu = _mm_glu(xb, w_in_t, l, off_cv, off_cv + CW, CW)
        cqn, ckv, kr = _latents(xb, w_in_t, wkr2_t, l, off_q, off_kv, row2(mla_q_norm_g[l]),
                                row2(mla_kv_norm_g[l]), cos, sin)

        us, v_ln = _gmlp(uvg, ws2[l], bs2[l], row2(gm_ln_g[l]), row2(gm_ln_b[l]), TP)

        cv_args = (cv_w_dw[l], row2(cv_b_dw[l]), row2(cv_ln_g[l]), row2(cv_ln_b[l]))
        yb_p = _conv(glu, glu, 0, B, S, *cv_args, zero_first=True)
        yb_s = _conv(glu, hist_s[l], TP, NB, L, *cv_args, zero_first=False)

        qt_p = _qproj_t(cqn, wq3t[l], cos_t, sin_t, B, S, scale_log2e)
        q_s = _qproj(cqn, wq3[l], cos, sin, TP, NB, L)
        k_p, vt_p = _kvproj(ckv, kr, wkv3[l], B, S)
        o_p = _attn_prompt(qt_p, k_p, vt_p)
        o_s = _attn_sample(q_s, cache_mla_latent, cache_kr_t, l, ckv, kr, wkv3[l], TP)

        merged = _merge(xb, us, yb_p, yb_s, o_p, o_s, gm_w_p_b, cv_w_pw_b, w_o_b, wg_t, row2(b_gate[l]), l)
        x, xb = _outproj_ln(merged, w_out_b, l, x, row2(ln1_g[l]), row2(ln1_b[l]), alpha)

        j = l // 2
        if l % 2 == 0:
            x, xb = _ffn_dense(xb, eid_dense, nu_dense, ffn_w1_b, ffn_w3_b, ffn_w2_b, j, tm_dense,
                               x, row2(ln2_g[l]), row2(ln2_b[l]), alpha)
        else:
            route = _router(x, router_pad[j], n_experts)
            slot_tok, block_e, n_used, dest = _moe_plan(route, n_experts, MOE_TM)
            y = _ffn_moe(x, slot_tok, block_e, n_used, moe_w1_b, moe_w3_b, moe_w2_b, j, MOE_TM)
            x, xb = _combine(x, y, dest, route, row2(ln2_g[l]), row2(ln2_b[l]), alpha)

        outs["lat_p"].append(ckv[:TP].reshape(B, S, RKV))
        outs["kr_p"].append(kr[:TP].reshape(B, S, QK_ROPE))
        outs["conv_p"].append(jnp.stack([glu[(s + 1) * S - (CV_KERNEL - 1):(s + 1) * S] for s in range(B)]))
        outs["lat_s"].append(ckv[TP:].reshape(NB, L, RKV))
        outs["kr_s"].append(kr[TP:].reshape(NB, L, QK_ROPE))
        outs["conv_s"].append(glu[TP:].reshape(NB, L, CW)[:, L - (CV_KERNEL - 1):])
        outs["v_s"].append(v_ln[TP:].reshape(NB, L, GW))

    return (x[:TP].reshape(B, S, D), x[TP:].reshape(NB, L, D),
            jnp.stack(outs["lat_p"]), jnp.stack(outs["kr_p"]), jnp.stack(outs["conv_p"]),
            jnp.stack(outs["lat_s"]), jnp.stack(outs["kr_s"]), jnp.stack(outs["conv_s"]),
            jnp.stack(outs["v_s"]))
```

```python
import functools
import math

import jax
import jax.numpy as jnp
from jax import lax
from jax.experimental import pallas as pl
from jax.experimental.pallas import tpu as pltpu

F32 = jnp.float32
BF16 = jnp.bfloat16

CHUNK = 64
GM_GROUPS = 8
GM_CHUNK = 128
CV_KERNEL = 31
N_HEADS = 16
QK_NOPE = 128
QK_ROPE = 64
V_DIM = 128
QK_PAD = 256
V_ONES = 16
ROPE_THETA = 10000.0
TOP_K = 2
LN_EPS = 1e-5
RMS_EPS = 1e-6
ATTN_SCALE = (QK_NOPE + QK_ROPE) ** -0.5

V7X_VMEM_LIMIT_BYTES = 56 * 1024 * 1024
LANES = 128
SUBLANES = 8
CONV_HALO = 32
MOE_TM = 512
FFN_CHUNKS = 2
GATHER_UNROLL = 8


def _cp(*sem):
    return pltpu.CompilerParams(dimension_semantics=sem,
                                vmem_limit_bytes=V7X_VMEM_LIMIT_BYTES)


def _tile(n, pref, mult=8):
    if n <= pref:
        return n
    for t in range(pref, 0, -1):
        if n % t == 0 and t % mult == 0:
            return t
    return n


def _ln(x, g, b):
    mu = jnp.mean(x, axis=-1, keepdims=True)
    xc = x - mu
    var = jnp.mean(xc * xc, axis=-1, keepdims=True)
    return xc * lax.rsqrt(var + LN_EPS) * g + b


def _rms(x, g):
    ms = jnp.mean(x * x, axis=-1, keepdims=True)
    return x * lax.rsqrt(ms + RMS_EPS) * g


def _dot(a, b):
    return jnp.dot(a, b, preferred_element_type=F32)


def _dot_nt(a, b):
    return lax.dot_general(a, b, (((1,), (1,)), ((), ())), preferred_element_type=F32)


def _wt_spec(layer, row0, tn, K, jmap):
    assert row0 % tn == 0
    return pl.BlockSpec((1, tn, K), lambda *ij: (layer, row0 // tn + jmap(*ij), 0))


def _mm_act_kernel(x_ref, wt_ref, b_ref, o_ref, *, act):
    acc = _dot_nt(x_ref[...], wt_ref[0]) + b_ref[...]
    if act == "gelu":
        acc = jax.nn.gelu(acc)
    elif act == "sigmoid":
        acc = jax.nn.sigmoid(acc)
    o_ref[...] = acc.astype(o_ref.dtype)


def _mm_act(x, wt, layer, row0, N, b, act, out_dtype, tm=1024, tn=1024):
    M, K = x.shape
    tm, tn = _tile(M, tm), _tile(N, tn, LANES)
    return pl.pallas_call(
        functools.partial(_mm_act_kernel, act=act),
        grid=(M // tm, N // tn),
        in_specs=[pl.BlockSpec((tm, K), lambda i, j: (i, 0)),
                  _wt_spec(layer, row0, tn, K, lambda i, j: j),
                  pl.BlockSpec((1, tn), lambda i, j: (0, j))],
        out_specs=pl.BlockSpec((tm, tn), lambda i, j: (i, j)),
        out_shape=jax.ShapeDtypeStruct((M, N), out_dtype),
        compiler_params=_cp("parallel", "arbitrary"),
        name="mm_" + act,
    )(x, wt, b)


def _mm_glu_kernel(x_ref, wa_ref, wb_ref, o_ref):
    x = x_ref[...]
    o_ref[...] = _dot_nt(x, wa_ref[0]) * jax.nn.sigmoid(_dot_nt(x, wb_ref[0]))


def _mm_glu(x, wt, layer, row_a, row_b, N, tm=1024, tn=512):
    M, K = x.shape
    tm, tn = _tile(M, tm), _tile(N, tn, LANES)
    return pl.pallas_call(
        _mm_glu_kernel,
        grid=(M // tm, N // tn),
        in_specs=[pl.BlockSpec((tm, K), lambda i, j: (i, 0)),
                  _wt_spec(layer, row_a, tn, K, lambda i, j: j),
                  _wt_spec(layer, row_b, tn, K, lambda i, j: j)],
        out_specs=pl.BlockSpec((tm, tn), lambda i, j: (i, j)),
        out_shape=jax.ShapeDtypeStruct((M, N), F32),
        compiler_params=_cp("parallel", "arbitrary"),
        name="mm_glu",
    )(x, wt, wt)


def _latent_kernel(x_ref, wq_ref, wkv_ref, wkr_ref, gq_ref, gkv_ref, cos_ref, sin_ref,
                   cq_ref, ckv_ref, kr_ref):
    x = x_ref[...]
    cq_ref[...] = _rms(_dot_nt(x, wq_ref[0]), gq_ref[...]).astype(cq_ref.dtype)
    ckv_ref[...] = _rms(_dot_nt(x, wkv_ref[0]), gkv_ref[...])
    r = _dot_nt(x, wkr_ref[0])
    kr_ref[...] = r[:, :QK_ROPE] * cos_ref[...] + r[:, QK_ROPE:] * sin_ref[...]


def _latents(x, wt, wkr2t, layer, row_q, row_kv, gq, gkv, cos, sin, tm=512):
    M, K = x.shape
    Rq, Rkv = gq.shape[1], gkv.shape[1]
    tm = _tile(M, tm)
    row = lambda i: (i, 0)
    full = lambda i: (0, 0)
    return pl.pallas_call(
        _latent_kernel,
        grid=(M // tm,),
        in_specs=[pl.BlockSpec((tm, K), row),
                  _wt_spec(layer, row_q, Rq, K, lambda i: 0), _wt_spec(layer, row_kv, Rkv, K, lambda i: 0),
                  pl.BlockSpec((1, 2 * QK_ROPE, K), lambda i: (layer, 0, 0)),
                  pl.BlockSpec((1, Rq), full), pl.BlockSpec((1, Rkv), full),
                  pl.BlockSpec((tm, QK_ROPE), row), pl.BlockSpec((tm, QK_ROPE), row)],
        out_specs=[pl.BlockSpec((tm, Rq), row), pl.BlockSpec((tm, Rkv), row),
                   pl.BlockSpec((tm, QK_ROPE), row)],
        out_shape=[jax.ShapeDtypeStruct((M, Rq), BF16), jax.ShapeDtypeStruct((M, Rkv), F32),
                   jax.ShapeDtypeStruct((M, QK_ROPE), F32)],
        compiler_params=_cp("parallel"),
        name="latents",
    )(x, wt, wt, wkr2t, gq, gkv, cos, sin)


def _gmlp_kernel(uv_ref, ws_ref, bs_ref, g_ref, b_ref, us_ref, v_ref, *, n_chunks, groups):
    W = v_ref.shape[1]
    gd = W // groups
    vn = _ln(uv_ref[:, W:], g_ref[...], b_ref[...])
    v_ref[...] = vn
    vb = vn.astype(BF16)
    for c in range(n_chunks):
        r0 = c * GM_CHUNK
        for g in range(groups):
            c0 = g * gd
            s = _dot(ws_ref[0, g], vb[r0:r0 + GM_CHUNK, c0:c0 + gd]) + bs_ref[0, :, c0:c0 + gd]
            u = uv_ref[r0:r0 + GM_CHUNK, c0:c0 + gd]
            us_ref[r0:r0 + GM_CHUNK, c0:c0 + gd] = (u * s).astype(us_ref.dtype)


def _gmlp(uvg, ws2, bs2, g, b, n_prompt_rows, tm=512):
    M, W2 = uvg.shape
    W = W2 // 2
    tm = _tile(math.gcd(n_prompt_rows, M - n_prompt_rows), tm, GM_CHUNK)
    npt = n_prompt_rows // tm
    sel = lambda i: (jnp.minimum(i // npt, 1), 0, 0, 0)
    sel3 = lambda i: (jnp.minimum(i // npt, 1), 0, 0)
    row = lambda i: (i, 0)
    full = lambda i: (0, 0)
    return pl.pallas_call(
        functools.partial(_gmlp_kernel, n_chunks=tm // GM_CHUNK, groups=GM_GROUPS),
        grid=(M // tm,),
        in_specs=[pl.BlockSpec((tm, W2), row),
                  pl.BlockSpec((1, GM_GROUPS, GM_CHUNK, GM_CHUNK), sel),
                  pl.BlockSpec((1, GM_CHUNK, W), sel3),
                  pl.BlockSpec((1, W), full), pl.BlockSpec((1, W), full)],
        out_specs=[pl.BlockSpec((tm, W), row), pl.BlockSpec((tm, W), row)],
        out_shape=[jax.ShapeDtypeStruct((M, W), BF16), jax.ShapeDtypeStruct((M, W), F32)],
        compiler_params=_cp("parallel"),
        name="gmlp",
    )(uvg, ws2, bs2, g, b)


def _conv_kernel(cur_ref, halo_ref, w_ref, bdw_ref, g_ref, b_ref, o_ref, full_ref, sh_ref, acc_ref,
                 *, tm, zero_first):
    C = cur_ref.shape[1]
    halo = halo_ref[...].reshape(CONV_HALO, C)
    if zero_first:
        halo = jnp.where(pl.program_id(1) == 0, 0.0, halo)
    full_ref[0:CONV_HALO, :] = halo
    full_ref[CONV_HALO:CONV_HALO + tm, :] = cur_ref[...]
    off = CONV_HALO - (CV_KERNEL - 1)
    ns = sh_ref.shape[1]

    def lane_block(c, carry):
        c0 = pl.multiple_of(c * LANES, LANES)
        sh_ref[0] = full_ref[:, pl.ds(c0, LANES)]
        for r in range(1, SUBLANES):
            sh_ref[r, :ns - SUBLANES, :] = full_ref[pl.ds(r, ns - SUBLANES), pl.ds(c0, LANES)]
        wk = w_ref[:, pl.ds(c0, LANES)]
        wrows = [jnp.broadcast_to(wk[k:k + 1, :], (SUBLANES, LANES)) for k in range(CV_KERNEL)]
        for rb in range(tm // SUBLANES):
            acc = jnp.zeros((SUBLANES, LANES), F32)
            for k in range(CV_KERNEL):
                a, r = divmod(off + k, SUBLANES)
                acc = acc + sh_ref[r, pl.ds((rb + a) * SUBLANES, SUBLANES), :] * wrows[k]
            acc_ref[pl.ds(rb * SUBLANES, SUBLANES), pl.ds(c0, LANES)] = acc
        return carry

    lax.fori_loop(0, C // LANES, lane_block, 0)
    y = _ln(acc_ref[...] + bdw_ref[...], g_ref[...], b_ref[...])
    o_ref[...] = (y * jax.nn.sigmoid(y)).astype(o_ref.dtype)


def _conv(glu, halo_src, row0, n_seq, L, w_dw, b_dw, g, b, zero_first, tm=256):
    C = glu.shape[1]
    tm = _tile(L, tm, CONV_HALO)
    lt = L // tm
    hb = tm // CONV_HALO
    assert row0 % tm == 0
    r0 = row0 // tm
    if zero_first:
        halo_spec = pl.BlockSpec((CONV_HALO, C), lambda s, i: (jnp.maximum((r0 + s * lt + i) * hb - 1, 0), 0))
    else:
        halo_spec = pl.BlockSpec((1, CONV_HALO, C), lambda s, i: (s, 0, 0))
    full = lambda s, i: (0, 0)
    return pl.pallas_call(
        functools.partial(_conv_kernel, tm=tm, zero_first=zero_first),
        grid=(n_seq, lt),
        in_specs=[pl.BlockSpec((tm, C), lambda s, i: (r0 + s * lt + i, 0)),
                  halo_spec,
                  pl.BlockSpec((CV_KERNEL, C), full),
                  pl.BlockSpec((1, C), full), pl.BlockSpec((1, C), full), pl.BlockSpec((1, C), full)],
        out_specs=pl.BlockSpec((tm, C), lambda s, i: (s * lt + i, 0)),
        out_shape=jax.ShapeDtypeStruct((n_seq * L, C), BF16),
        scratch_shapes=[pltpu.VMEM((CONV_HALO + tm, C), F32),
                        pltpu.VMEM((SUBLANES, CONV_HALO + tm, LANES), F32),
                        pltpu.VMEM((tm, C), F32)],
        compiler_params=_cp("parallel", "arbitrary"),
        name="conv_prompt" if zero_first else "conv_sample",
    )(glu, halo_src, w_dw, b_dw, g, b)


def _qproj_kernel(cq_ref, w_ref, cos_ref, sin_ref, q_ref, *, hg, nb, L):
    cq = cq_ref[...]
    cos, sin = cos_ref[...], sin_ref[...]
    for h in range(hg):
        r = _dot(cq, w_ref[h])
        nope = r[:, :QK_NOPE] * ATTN_SCALE
        rp = (r[:, QK_NOPE:QK_NOPE + QK_ROPE] * cos + r[:, QK_NOPE + QK_ROPE:] * sin) * ATTN_SCALE
        if nb == 1:
            q_ref[0, h, :, :QK_NOPE] = nope.astype(q_ref.dtype)
            q_ref[0, h, :, QK_NOPE:] = rp.astype(q_ref.dtype)
        else:
            q_ref[:, h, :, :QK_NOPE] = nope.reshape(nb, L, QK_NOPE).astype(q_ref.dtype)
            q_ref[:, h, :, QK_NOPE:] = rp.reshape(nb, L, QK_ROPE).astype(q_ref.dtype)


def _qproj(cqn, wq3, cos, sin, row0, n_seq, L, hg=4, tm=512):
    R = cqn.shape[1]
    H = wq3.shape[0]
    hg = _tile(H, hg, 1)
    dq = QK_NOPE + QK_ROPE
    if L >= tm:
        tm = _tile(L, tm)
        nb, lt = 1, L // tm
        out_spec = pl.BlockSpec((1, hg, tm, dq), lambda i, h: (i // lt, h, i % lt, 0))
    else:
        nb = _tile(n_seq, max(tm // L, 1), 1)
        tm = nb * L
        out_spec = pl.BlockSpec((nb, hg, L, dq), lambda i, h: (i, h, 0, 0))
    assert row0 % tm == 0
    r0 = row0 // tm
    row = lambda i, h: (r0 + i, 0)
    return pl.pallas_call(
        functools.partial(_qproj_kernel, hg=hg, nb=nb, L=L),
        grid=(n_seq * L // tm, H // hg),
        in_specs=[pl.BlockSpec((tm, R), row),
                  pl.BlockSpec((hg, R, wq3.shape[2]), lambda i, h: (h, 0, 0)),
                  pl.BlockSpec((tm, QK_ROPE), row), pl.BlockSpec((tm, QK_ROPE), row)],
        out_specs=out_spec,
        out_shape=jax.ShapeDtypeStruct((n_seq, H, L, dq), BF16),
        compiler_params=_cp("parallel", "arbitrary"),
        name="qproj",
    )(cqn, wq3, cos, sin)


def _qproj_t_kernel(cq_ref, w_ref, cos_ref, sin_ref, q_ref, *, hg, scale):
    cq = cq_ref[...]
    cos, sin = cos_ref[...], sin_ref[...]
    for h in range(hg):
        r = _dot_nt(w_ref[h], cq)
        rp = r[QK_NOPE:QK_NOPE + QK_ROPE] * cos + r[QK_NOPE + QK_ROPE:] * sin
        q_ref[0, h, :QK_NOPE, :] = (r[:QK_NOPE] * scale).astype(q_ref.dtype)
        q_ref[0, h, QK_NOPE:QK_NOPE + QK_ROPE, :] = (rp * scale).astype(q_ref.dtype)
        q_ref[0, h, QK_NOPE + QK_ROPE:, :] = jnp.zeros((QK_PAD - QK_NOPE - QK_ROPE, cq.shape[0]), q_ref.dtype)


def _qproj_t(cqn, wq3t, cos_t, sin_t, n_seq, L, scale, hg=8, tm=512):
    R = cqn.shape[1]
    H = wq3t.shape[0]
    hg = _tile(H, hg, 1)
    tm = _tile(L, tm, LANES)
    lt = L // tm
    dq = QK_PAD
    return pl.pallas_call(
        functools.partial(_qproj_t_kernel, hg=hg, scale=scale),
        grid=(n_seq * lt, H // hg),
        in_specs=[pl.BlockSpec((tm, R), lambda i, h: (i, 0)),
                  pl.BlockSpec((hg, wq3t.shape[1], R), lambda i, h: (h, 0, 0)),
                  pl.BlockSpec((QK_ROPE, tm), lambda i, h: (0, i)),
                  pl.BlockSpec((QK_ROPE, tm), lambda i, h: (0, i))],
        out_specs=pl.BlockSpec((1, hg, dq, tm), lambda i, h: (i // lt, h, 0, i % lt)),
        out_shape=jax.ShapeDtypeStruct((n_seq, H, dq, L), BF16),
        compiler_params=_cp("parallel", "arbitrary"),
        name="qproj_t",
    )(cqn, wq3t, cos_t, sin_t)


def _kvproj_kernel(ckv_ref, kr_ref, w_ref, k_ref, vt_ref, *, hg):
    ckv = ckv_ref[...].astype(BF16)
    kr = kr_ref[...].astype(BF16)
    for h in range(hg):
        r = _dot(ckv, w_ref[h])
        k_ref[0, h, :, :QK_NOPE] = r[:, :QK_NOPE].astype(k_ref.dtype)
        k_ref[0, h, :, QK_NOPE:QK_NOPE + QK_ROPE] = kr
        k_ref[0, h, :, QK_NOPE + QK_ROPE:] = jnp.zeros((kr.shape[0], QK_PAD - QK_NOPE - QK_ROPE), k_ref.dtype)
        vt_ref[0, h, :V_DIM, :] = r[:, QK_NOPE:].T.astype(vt_ref.dtype)
        vt_ref[0, h, V_DIM:, :] = jnp.ones((V_ONES, ckv.shape[0]), vt_ref.dtype)


def _kvproj(ckv, kr, wkv3, n_seq, L, hg=8, tm=512):
    R = ckv.shape[1]
    H = wkv3.shape[0]
    hg = _tile(H, hg, 1)
    tm = _tile(L, tm, LANES)
    lt = L // tm
    dq = QK_PAD
    row = lambda i, h: (i, 0)
    wmap = lambda i, h: (h, 0, 0)
    return pl.pallas_call(
        functools.partial(_kvproj_kernel, hg=hg),
        grid=(n_seq * lt, H // hg),
        in_specs=[pl.BlockSpec((tm, R), row), pl.BlockSpec((tm, QK_ROPE), row),
                  pl.BlockSpec((hg, R, wkv3.shape[2]), wmap)],
        out_specs=[pl.BlockSpec((1, hg, tm, dq), lambda i, h: (i // lt, h, i % lt, 0)),
                   pl.BlockSpec((1, hg, V_DIM + V_ONES, tm), lambda i, h: (i // lt, h, 0, i % lt))],
        out_shape=[jax.ShapeDtypeStruct((n_seq, H, L, dq), BF16),
                   jax.ShapeDtypeStruct((n_seq, H, V_DIM + V_ONES, L), BF16)],
        compiler_params=_cp("parallel", "arbitrary"),
        name="kvproj",
    )(ckv, kr, wkv3)


def _attn_prompt_kernel(qt_ref, k_ref, vt_ref, o_ref, s_ref, p_ref, acc_ref, *, tq, tk, hg):
    qi = pl.program_id(2)
    d0 = pl.multiple_of(qi * tq, tq)
    d1 = pl.multiple_of(d0 + tk, tk)
    all_q = slice(0, tq)
    late_q = slice(tk, tq)

    def scores(g, j0, slot, cols=all_q):
        s = _dot(k_ref[0, g, pl.ds(j0, tk), :], qt_ref[0, g, :, cols])
        s_ref[slot, g, :, cols] = s
        return jnp.max(s, axis=0, keepdims=True)

    def softmax(g, slot, mx, m, mask=None, cols=all_q):
        s = s_ref[slot, g, :, cols]
        if mask is not None:
            s = jnp.where(mask, s, -jnp.inf)
            mx = jnp.max(s, axis=0, keepdims=True)
        m_new = jnp.maximum(m, mx)
        p_ref[slot, g, :, cols] = jnp.exp2(s - m_new).astype(BF16)
        return jnp.exp2(m - m_new), m_new

    def accumulate(g, j0, slot, alpha, cols=all_q):
        acc_ref[g, :, cols] = (alpha * acc_ref[g, :, cols]
                               + _dot(vt_ref[0, g, :, pl.ds(j0, tk)], p_ref[slot, g, :, cols]))

    def step(j0, slot, state, mask=None, next_cols=all_q):
        jp = pl.multiple_of(jnp.maximum(j0 - tk, 0), tk)
        new = []
        for g in range(hg):
            mx, a_prev, m = state[g]
            mx_next = scores(g, pl.multiple_of(j0 + tk, tk), 1 - slot, next_cols)
            accumulate(g, jp, 1 - slot, a_prev)
            alpha, m = softmax(g, slot, mx, m, mask)
            new.append((mx_next, alpha, m))
        return tuple(new)

    def pair(i, state):
        j0 = pl.multiple_of(2 * i * tk, tk)
        state = step(j0, 0, state)
        return step(pl.multiple_of(j0 + tk, tk), 1, state)

    p_ref[1] = jnp.zeros_like(p_ref[1])
    acc_ref[...] = jnp.zeros_like(acc_ref)
    init = tuple((scores(g, 0, 0), jnp.ones((1, tq), F32), jnp.full((1, tq), -jnp.inf, F32))
                 for g in range(hg))
    def two_pairs(i0, state):
        return pair(i0 + 1, pair(i0, state))

    n_pairs = qi * (tq // (2 * tk))
    odd = lax.rem(n_pairs, 2)
    half = lax.rem(n_pairs // 2, 2)
    state = lax.cond(odd == 1, lambda st: pair(0, st), lambda st: st, init)
    state = lax.cond(half == 1, lambda st: two_pairs(odd, st), lambda st: st, state)
    base = odd + 2 * half
    state = lax.fori_loop(0, n_pairs // 4,
                          lambda i, st: two_pairs(4 * i + base + 2, two_pairs(4 * i + base, st)), state)

    key_chunk = lax.broadcasted_iota(jnp.int32, (tk, tq), 0) // CHUNK
    qry_chunk = lax.broadcasted_iota(jnp.int32, (tk, tq), 1) // CHUNK
    visible = key_chunk <= qry_chunk
    state = step(d0, 0, state, mask=visible, next_cols=late_q)
    for g in range(hg):
        _, a_prev, m = state[g]
        accumulate(g, d0, 0, a_prev)
        alpha, _ = softmax(g, 1, None, m[:, late_q], visible[:, :tq - tk], late_q)
        accumulate(g, d1, 1, alpha, late_q)
        o = acc_ref[g, :V_DIM, :] / acc_ref[g, V_DIM:V_DIM + 1, :]
        o_ref[:, g * V_DIM:(g + 1) * V_DIM] = o.T.astype(o_ref.dtype)


def _attn_prompt(qt, k, vt, tq=512, hg=2):
    B, H, dq, S = qt.shape
    tq = _tile(S, tq, 2 * LANES)
    tk = tq // 2
    hg = _tile(H, hg, 1)
    return pl.pallas_call(
        functools.partial(_attn_prompt_kernel, tq=tq, tk=tk, hg=hg),
        grid=(B, H // hg, S // tq),
        in_specs=[pl.BlockSpec((1, hg, dq, tq), lambda b, h, i: (b, h, 0, i)),
                  pl.BlockSpec((1, hg, S, dq), lambda b, h, i: (b, h, 0, 0)),
                  pl.BlockSpec((1, hg, vt.shape[2], S), lambda b, h, i: (b, h, 0, 0))],
        out_specs=pl.BlockSpec((tq, hg * V_DIM), lambda b, h, i: (b * (S // tq) + i, h)),
        out_shape=jax.ShapeDtypeStruct((B * S, H * V_DIM), BF16),
        scratch_shapes=[pltpu.VMEM((2, hg, tk, tq), F32), pltpu.VMEM((2, hg, tk, tq), BF16),
                        pltpu.VMEM((hg, vt.shape[2], tq), F32)],
        compiler_params=_cp("parallel", "parallel", "arbitrary"),
        name="attn_prompt",
    )(qt, k, vt)


def _attn_sample_kernel(q_ref, plat_ref, pkrt_ref, nlat_ref, nkr_ref, w_ref, o_ref, ql_ref, qr_ref,
                        *, H, L, kc):
    P = plat_ref.shape[2]
    for h in range(H):
        qh = q_ref[0, h]
        ql_ref[h * L:(h + 1) * L, :] = _dot_nt(qh[:, :QK_NOPE], w_ref[h, :, :QK_NOPE]).astype(BF16)
        qr_ref[h * L:(h + 1) * L, :] = qh[:, QK_NOPE:]
    ql = ql_ref[...]
    qr = qr_ref[...]

    def step(lat, s_rope, carry):
        m, l, acc = carry
        s = _dot_nt(ql, lat) + s_rope
        m_new = jnp.maximum(m, jnp.max(s, axis=-1, keepdims=True))
        alpha = jnp.exp(m - m_new)
        p = jnp.exp(s - m_new)
        l = alpha * l + jnp.sum(p, axis=-1, keepdims=True)
        acc = alpha * acc + _dot(p.astype(BF16), lat)
        return m_new, l, acc

    R = ql.shape[1]
    carry = (jnp.full((H * L, 1), -jnp.inf, F32), jnp.zeros((H * L, 1), F32), jnp.zeros((H * L, R), F32))
    for c in range(P // kc):
        krt = pkrt_ref[0, 0, :, c * kc:(c + 1) * kc].astype(BF16)
        carry = step(plat_ref[0, 0, c * kc:(c + 1) * kc, :].astype(BF16), _dot(qr, krt), carry)
    m, l, acc = step(nlat_ref[...].astype(BF16), _dot_nt(qr, nkr_ref[...].astype(BF16)), carry)
    ol = (acc / l).astype(BF16)
    for h in range(H):
        o_ref[:, h * V_DIM:(h + 1) * V_DIM] = _dot(ol[h * L:(h + 1) * L, :], w_ref[h, :, QK_NOPE:]).astype(o_ref.dtype)


def _attn_sample(q, cache_lat, cache_kr_t, layer, ckv, kr, wkv3, row0):
    n_req, H, L, dq = q.shape
    P, R = cache_lat.shape[2], cache_lat.shape[3]
    kc = _tile(P, 1024)
    assert row0 % L == 0
    r0 = row0 // L
    return pl.pallas_call(
        functools.partial(_attn_sample_kernel, H=H, L=L, kc=kc),
        grid=(n_req,),
        in_specs=[pl.BlockSpec((1, H, L, dq), lambda b: (b, 0, 0, 0)),
                  pl.BlockSpec((1, 1, P, R), lambda b: (layer, b, 0, 0)),
                  pl.BlockSpec((1, 1, QK_ROPE, P), lambda b: (layer, b, 0, 0)),
                  pl.BlockSpec((L, R), lambda b: (r0 + b, 0)),
                  pl.BlockSpec((L, QK_ROPE), lambda b: (r0 + b, 0)),
                  pl.BlockSpec(wkv3.shape, lambda b: (0, 0, 0))],
        out_specs=pl.BlockSpec((L, H * V_DIM), lambda b: (b, 0)),
        out_shape=jax.ShapeDtypeStruct((n_req * L, H * V_DIM), BF16),
        scratch_shapes=[pltpu.VMEM((H * L, R), BF16), pltpu.VMEM((H * L, QK_ROPE), BF16)],
        compiler_params=_cp("parallel"),
        name="attn_sample",
    )(q, cache_lat, cache_kr_t, ckv, kr, wkv3)


def _merge_kernel(x_ref, us_ref, ybp_ref, ybs_ref, op_ref, os_ref, wp_ref, wpw_ref, wo_ref,
                  wga_ref, wgb_ref, wgc_ref, ba_ref, bb_ref, bc_ref, out_ref, *, npt):
    prompt = pl.program_id(0) < npt
    x = x_ref[...]
    yb_in = jnp.where(prompt, ybp_ref[...], ybs_ref[...])
    o_in = jnp.where(prompt, op_ref[...], os_ref[...])
    out = jax.nn.sigmoid(_dot_nt(x, wga_ref[0]) + ba_ref[...]) * _dot(us_ref[...], wp_ref[0])
    out += jax.nn.sigmoid(_dot_nt(x, wgb_ref[0]) + bb_ref[...]) * _dot(yb_in, wpw_ref[0])
    out += jax.nn.sigmoid(_dot_nt(x, wgc_ref[0]) + bc_ref[...]) * _dot(o_in, wo_ref[0])
    out_ref[...] = out.astype(out_ref.dtype)


def _merge(xb, us, yb_p, yb_s, o_p, o_s, wp, wpw, wo, wgt, bg, layer, tm=512, tn=512):
    M = us.shape[0]
    D = wp.shape[2]
    TP, TS = yb_p.shape[0], yb_s.shape[0]
    tm, tn = _tile(math.gcd(TP, TS), tm), _tile(D, tn, LANES)
    npt = TP // tm
    nj = D // tn
    row = lambda i, j: (i, 0)
    prow = lambda i, j: (jnp.minimum(i, npt - 1), 0)
    srow = lambda i, j: (jnp.maximum(i - npt, 0), 0)
    col = lambda i, j: (layer, 0, j)
    gcol = lambda g: (lambda i, j: (0, g * nj + j))
    grow = lambda g: _wt_spec(layer, g * D, tn, wgt.shape[2], lambda i, j: j)
    return pl.pallas_call(
        functools.partial(_merge_kernel, npt=npt),
        grid=(M // tm, nj),
        in_specs=[pl.BlockSpec((tm, xb.shape[1]), row), pl.BlockSpec((tm, us.shape[1]), row),
                  pl.BlockSpec((tm, yb_p.shape[1]), prow), pl.BlockSpec((tm, yb_s.shape[1]), srow),
                  pl.BlockSpec((tm, o_p.shape[1]), prow), pl.BlockSpec((tm, o_s.shape[1]), srow),
                  pl.BlockSpec((1, wp.shape[1], tn), col), pl.BlockSpec((1, wpw.shape[1], tn), col),
                  pl.BlockSpec((1, wo.shape[1], tn), col),
                  grow(0), grow(1), grow(2),
                  pl.BlockSpec((1, tn), gcol(0)), pl.BlockSpec((1, tn), gcol(1)), pl.BlockSpec((1, tn), gcol(2))],
        out_specs=pl.BlockSpec((tm, tn), lambda i, j: (i, j)),
        out_shape=jax.ShapeDtypeStruct((M, D), BF16),
        compiler_params=_cp("parallel", "arbitrary"),
        name="merge",
    )(xb, us, yb_p, yb_s, o_p, o_s, wp, wpw, wo, wgt, wgt, wgt, bg, bg, bg)


def _outproj_kernel(m_ref, w_ref, x_ref, g_ref, b_ref, o_ref, ob_ref, *, alpha):
    y = _ln(alpha * x_ref[...] + _dot(m_ref[...], w_ref[0]), g_ref[...], b_ref[...])
    o_ref[...] = y
    ob_ref[...] = y.astype(ob_ref.dtype)


def _outproj_ln(merged, w, layer, x, g, b, alpha, tm=512):
    M, D = x.shape
    tm = _tile(M, tm)
    row = lambda i: (i, 0)
    full = lambda i: (0, 0)
    return pl.pallas_call(
        functools.partial(_outproj_kernel, alpha=alpha),
        grid=(M // tm,),
        in_specs=[pl.BlockSpec((tm, merged.shape[1]), row),
                  pl.BlockSpec((1,) + w.shape[1:], lambda i: (layer, 0, 0)),
                  pl.BlockSpec((tm, D), row), pl.BlockSpec((1, D), full), pl.BlockSpec((1, D), full)],
        out_specs=[pl.BlockSpec((tm, D), row), pl.BlockSpec((tm, D), row)],
        out_shape=[jax.ShapeDtypeStruct((M, D), F32), jax.ShapeDtypeStruct((M, D), BF16)],
        compiler_params=_cp("parallel"),
        name="outproj_ln",
    )(merged, w, x, g, b)


def _swiglu_partial(x, w1_ref, w3_ref, w2_ref):
    tf = w1_ref.shape[3]
    n_chunks = FFN_CHUNKS if tf % (FFN_CHUNKS * LANES) == 0 else 1
    c = tf // n_chunks
    hs = []
    for k in range(n_chunks):
        cols = slice(k * c, (k + 1) * c)
        h = jax.nn.silu(_dot(x, w1_ref[0, 0, :, cols])) * _dot(x, w3_ref[0, 0, :, cols])
        hs.append(h.astype(BF16))
    out = _dot(hs[0], w2_ref[0, 0, 0:c, :])
    for k in range(1, n_chunks):
        out += _dot(hs[k], w2_ref[0, 0, k * c:(k + 1) * c, :])
    return out


def _ffn_dense_kernel(eid_ref, nu_ref, x_ref, w1_ref, w3_ref, w2_ref, r_ref, g_ref, b_ref,
                      o_ref, ob_ref, acc_ref, *, alpha):
    f = pl.program_id(1)

    @pl.when(f == 0)
    def _():
        acc_ref[...] = jnp.zeros_like(acc_ref)

    acc_ref[...] += _swiglu_partial(x_ref[...], w1_ref, w3_ref, w2_ref)

    @pl.when(f == pl.num_programs(1) - 1)
    def _():
        y = _ln(alpha * r_ref[...] + acc_ref[...], g_ref[...], b_ref[...])
        o_ref[...] = y
        ob_ref[...] = y.astype(ob_ref.dtype)


def _ffn_moe_kernel(eid_ref, nu_ref, idx0_ref, idxn_ref, x_hbm, w1_ref, w3_ref, w2_ref, o_ref,
                    xg_ref, xb_ref, sem, *, tm):
    i = pl.program_id(0)
    f = pl.program_id(1)
    n_used = nu_ref[0]

    def row_copy(t, r):
        return pltpu.make_async_copy(x_hbm.at[pl.ds(t, 1), :], xg_ref.at[pl.ds(r, 1), :], sem)

    def start_rows(idx_ref):
        def body(c, carry):
            for u in range(GATHER_UNROLL):
                r = c * GATHER_UNROLL + u
                row_copy(idx_ref[0, 0, r], r).start(priority=u % 2)
            return carry
        lax.fori_loop(0, tm // GATHER_UNROLL, body, 0)

    def wait_rows():
        def body(c, carry):
            for u in range(GATHER_UNROLL):
                row_copy(0, c * GATHER_UNROLL + u).wait()
            return carry
        lax.fori_loop(0, tm // GATHER_UNROLL, body, 0)

    @pl.when(f == 0)
    def _():
        o_ref[...] = jnp.zeros_like(o_ref)

    @pl.when((f == 0) & (i < n_used))
    def _():
        @pl.when(i == 0)
        def _():
            start_rows(idx0_ref)

        wait_rows()
        xb_ref[...] = xg_ref[...].astype(BF16)

    @pl.when((f == jnp.minimum(1, pl.num_programs(1) - 1)) & (i + 1 < n_used))
    def _():
        start_rows(idxn_ref)

    @pl.when(i < n_used)
    def _():
        o_ref[...] += _swiglu_partial(xb_ref[...], w1_ref, w3_ref, w2_ref)


def _ffn_weight_specs(layer, D, tf):
    last = lambda i, nu: jnp.minimum(i, nu[0] - 1)
    w13 = lambda i, f, e, nu: (layer, e[last(i, nu)], 0, jnp.where(i < nu[0], f, 0))
    w2m = lambda i, f, e, nu: (layer, e[last(i, nu)], jnp.where(i < nu[0], f, 0), 0)
    return [pl.BlockSpec((1, 1, D, tf), w13), pl.BlockSpec((1, 1, D, tf), w13), pl.BlockSpec((1, 1, tf, D), w2m)]


def _ffn_dense(x, eid, n_used, w1, w3, w2, layer, tm, resid, g, b, alpha, tf=512):
    M, D = x.shape
    Fdim = w1.shape[3]
    tf = _tile(Fdim, tf, LANES)
    row = lambda i, f, e, nu: (i, 0)
    full = lambda i, f, e, nu: (0, 0)
    return pl.pallas_call(
        functools.partial(_ffn_dense_kernel, alpha=alpha),
        grid_spec=pltpu.PrefetchScalarGridSpec(
            num_scalar_prefetch=2, grid=(M // tm, Fdim // tf),
            in_specs=[pl.BlockSpec((tm, D), row)] + _ffn_weight_specs(layer, D, tf)
            + [pl.BlockSpec((tm, D), row), pl.BlockSpec((1, D), full), pl.BlockSpec((1, D), full)],
            out_specs=[pl.BlockSpec((tm, D), row), pl.BlockSpec((tm, D), row)],
            scratch_shapes=[pltpu.VMEM((tm, D), F32)]),
        out_shape=[jax.ShapeDtypeStruct((M, D), F32), jax.ShapeDtypeStruct((M, D), BF16)],
        compiler_params=_cp("arbitrary", "arbitrary"),
        name="ffn_dense",
    )(eid, n_used, x, w1, w3, w2, resid, g, b)


def _ffn_moe(x, slot_tok, eid, n_used, w1, w3, w2, layer, tm, tf=512):
    D = x.shape[1]
    n_slots = slot_tok.shape[0]
    nb = n_slots // tm
    Fdim = w1.shape[3]
    tf = _tile(Fdim, tf, LANES)
    idx3 = slot_tok.reshape(nb, 1, tm)
    smem = functools.partial(pl.BlockSpec, (1, 1, tm), memory_space=pltpu.SMEM)
    return pl.pallas_call(
        functools.partial(_ffn_moe_kernel, tm=tm),
        grid_spec=pltpu.PrefetchScalarGridSpec(
            num_scalar_prefetch=2, grid=(nb, Fdim // tf),
            in_specs=[smem(lambda i, f, e, nu: (0, 0, 0)),
                      smem(lambda i, f, e, nu: (jnp.minimum(i + 1, nb - 1), 0, 0)),
                      pl.BlockSpec(memory_space=pl.ANY)] + _ffn_weight_specs(layer, D, tf),
            out_specs=pl.BlockSpec((tm, D), lambda i, f, e, nu: (i, 0)),
            scratch_shapes=[pltpu.VMEM((tm, D), F32), pltpu.VMEM((tm, D), BF16),
                            pltpu.SemaphoreType.DMA(())]),
        out_shape=jax.ShapeDtypeStruct((n_slots, D), F32),
        compiler_params=_cp("arbitrary", "arbitrary"),
        name="ffn_moe",
    )(eid, n_used, idx3, idx3, x, w1, w3, w2)


def _router_kernel(x_ref, r_ref, o_ref, *, n_experts):
    logits = jnp.dot(x_ref[...], r_ref[...], preferred_element_type=F32,
                     precision=lax.Precision.HIGHEST)
    col = lax.broadcasted_iota(jnp.int32, logits.shape, 1)
    lg = jnp.where(col < n_experts, logits, -jnp.inf)
    m1 = jnp.max(lg, axis=-1, keepdims=True)
    i1 = jnp.min(jnp.where(lg == m1, col, LANES), axis=-1, keepdims=True)
    lg2 = jnp.where(col == i1, -jnp.inf, lg)
    m2 = jnp.max(lg2, axis=-1, keepdims=True)
    i2 = jnp.min(jnp.where(lg2 == m2, col, LANES), axis=-1, keepdims=True)
    e = jnp.exp(m2 - m1)
    g1 = 1.0 / (1.0 + e)
    g2 = e * g1
    out = jnp.where(col == 0, i1.astype(F32),
                    jnp.where(col == 1, i2.astype(F32),
                              jnp.where(col == 2, g1, jnp.where(col == 3, g2, 0.0))))
    o_ref[...] = out


def _router(x, router_pad, n_experts, tm=512):
    M, D = x.shape
    tm = _tile(M, tm)
    return pl.pallas_call(
        functools.partial(_router_kernel, n_experts=n_experts),
        grid=(M // tm,),
        in_specs=[pl.BlockSpec((tm, D), lambda i: (i, 0)), pl.BlockSpec((D, LANES), lambda i: (0, 0))],
        out_specs=pl.BlockSpec((tm, LANES), lambda i: (i, 0)),
        out_shape=jax.ShapeDtypeStruct((M, LANES), F32),
        compiler_params=_cp("parallel"),
        name="router",
    )(x, router_pad)


def _combine_kernel(idx0_ref, idxn_ref, x_ref, r_ref, g_ref, b_ref, y_hbm, o_ref, ob_ref, yg_ref, sem,
                    *, tm, alpha):
    i = pl.program_id(0)
    slot = lax.rem(i, 2)

    def row_copy(t, k, r, s):
        return pltpu.make_async_copy(y_hbm.at[pl.ds(t, 1), :], yg_ref.at[s, k, pl.ds(r, 1), :], sem.at[s])

    def start_rows(idx_ref, s):
        def body(c, carry):
            for u in range(GATHER_UNROLL):
                r = c * GATHER_UNROLL + u
                for k in range(TOP_K):
                    row_copy(idx_ref[0, k, r], k, r, s).start(priority=(u + k) % 2)
            return carry
        lax.fori_loop(0, tm // GATHER_UNROLL, body, 0)

    def wait_rows(s):
        def body(c, carry):
            for u in range(GATHER_UNROLL):
                for k in range(TOP_K):
                    row_copy(0, k, c * GATHER_UNROLL + u, s).wait()
            return carry
        lax.fori_loop(0, tm // GATHER_UNROLL, body, 0)

    @pl.when(i == 0)
    def _():
        start_rows(idx0_ref, 0)

    wait_rows(slot)

    @pl.when(i + 1 < pl.num_programs(0))
    def _():
        start_rows(idxn_ref, 1 - slot)

    r = r_ref[...]
    y = alpha * x_ref[...] + r[:, 2:3] * yg_ref[slot, 0] + r[:, 3:4] * yg_ref[slot, 1]
    y = _ln(y, g_ref[...], b_ref[...])
    o_ref[...] = y
    ob_ref[...] = y.astype(ob_ref.dtype)


def _combine(x, y, dest, route, g, b, alpha, tm=256):
    M, D = x.shape
    tm = _tile(M, tm)
    nt = M // tm
    idx3 = jnp.transpose(dest.reshape(nt, tm, TOP_K), (0, 2, 1))
    smem = functools.partial(pl.BlockSpec, (1, TOP_K, tm), memory_space=pltpu.SMEM)
    row = lambda i: (i, 0)
    full = lambda i: (0, 0)
    return pl.pallas_call(
        functools.partial(_combine_kernel, tm=tm, alpha=alpha),
        grid=(nt,),
        in_specs=[smem(lambda i: (0, 0, 0)), smem(lambda i: (jnp.minimum(i + 1, nt - 1), 0, 0)),
                  pl.BlockSpec((tm, D), row), pl.BlockSpec((tm, LANES), row),
                  pl.BlockSpec((1, D), full), pl.BlockSpec((1, D), full),
                  pl.BlockSpec(memory_space=pl.ANY)],
        out_specs=[pl.BlockSpec((tm, D), row), pl.BlockSpec((tm, D), row)],
        out_shape=[jax.ShapeDtypeStruct((M, D), F32), jax.ShapeDtypeStruct((M, D), BF16)],
        scratch_shapes=[pltpu.VMEM((2, TOP_K, tm, D), F32), pltpu.SemaphoreType.DMA((2,))],
        compiler_params=_cp("arbitrary"),
        name="combine",
    )(idx3, idx3, x, route, g, b, y)


def _moe_plan(route, n_experts, tm):
    T = route.shape[0]
    n_assign = T * TOP_K
    e_flat = route[:, :TOP_K].astype(jnp.int32).reshape(-1)
    onehot = (e_flat[:, None] == jnp.arange(n_experts, dtype=jnp.int32)[None, :]).astype(jnp.int32)
    csum = jnp.cumsum(onehot, axis=0)
    rank = jnp.take_along_axis(csum, e_flat[:, None], axis=1)[:, 0] - 1
    counts = csum[-1]
    padded = ((counts + tm - 1) // tm) * tm
    pad_end = jnp.cumsum(padded)
    pad_start = pad_end - padded
    dest = (pad_start[e_flat] + rank).astype(jnp.int32)
    n_blocks = -(-n_assign // tm) + n_experts
    n_slots = n_blocks * tm
    slot_tok = jnp.zeros((n_slots,), jnp.int32).at[dest].set(jnp.arange(n_assign, dtype=jnp.int32) // TOP_K)
    block_e = jnp.minimum(jnp.searchsorted(pad_end, jnp.arange(n_blocks, dtype=jnp.int32) * tm, side="right"),
                          n_experts - 1).astype(jnp.int32)
    n_used = (pad_end[-1:] // tm).astype(jnp.int32)
    return slot_tok, block_e, n_used, dest.reshape(T, TOP_K)


def _rot_half_cols(w):
    half = w.shape[-1] // 2
    return jnp.concatenate([-w[..., half:], w[..., :half]], axis=-1)


def kernel(x_prompt, x_sample, cache_mla_latent, cache_mla_krope, cache_conv, w_in, b_gate, gm_ln_g, gm_ln_b, gm_w_s, gm_b_s, gm_w_p, cv_w_dw, cv_b_dw, cv_ln_g, cv_ln_b, cv_w_pw, mla_q_norm_g, mla_kv_norm_g, mla_w_uq, mla_w_uk, mla_w_uv, mla_w_o, w_out, ln1_g, ln1_b, ln2_g, ln2_b, ffn_w1, ffn_w3, ffn_w2, moe_router, moe_w1, moe_w3, moe_w2):
    B, S, D = x_prompt.shape
    NB, L, _ = x_sample.shape
    depth = w_in.shape[0]
    past = cache_mla_latent.shape[2]
    GW = gm_ln_g.shape[1]
    CW = cv_ln_g.shape[1]
    RQ = mla_q_norm_g.shape[1]
    RKV = mla_kv_norm_g.shape[1]
    H = N_HEADS
    n_experts = moe_router.shape[2]
    alpha = float((2 * depth) ** 0.25)
    TP, TS = B * S, NB * L
    T = TP + TS
    off_cv = 2 * GW
    off_q = off_cv + 2 * CW
    off_kv = off_q + RQ
    off_kr = off_kv + RKV
    off_g = off_kr + QK_ROPE
    assert L <= GM_CHUNK and GM_CHUNK % L == 0 and S % GM_CHUNK == 0 and TS % GM_CHUNK == 0
    assert L >= CV_KERNEL - 1 and L % 8 == 0 and CV_KERNEL - 1 <= CONV_HALO

    w_in_t = jnp.swapaxes(w_in, 1, 2).astype(BF16)
    wkr_t = w_in_t[:, off_kr:off_g, :]
    half_r = QK_ROPE // 2
    wkr2_t = jnp.concatenate([wkr_t, -wkr_t[:, half_r:], wkr_t[:, :half_r]], axis=1)
    wg_t = w_in_t[:, off_g:, :]
    gm_w_p_b, cv_w_pw_b = gm_w_p.astype(BF16), cv_w_pw.astype(BF16)
    w_o_b, w_out_b = mla_w_o.astype(BF16), w_out.astype(BF16)
    ffn_w1_b, ffn_w3_b, ffn_w2_b = (w.astype(BF16)[:, None] for w in (ffn_w1, ffn_w3, ffn_w2))
    moe_w1_b, moe_w3_b, moe_w2_b = moe_w1.astype(BF16), moe_w3.astype(BF16), moe_w2.astype(BF16)
    zero_uv = jnp.zeros((1, off_cv), F32)

    uq = mla_w_uq.reshape(depth, RQ, H, QK_NOPE + QK_ROPE)
    uq_rope = uq[..., QK_NOPE:]
    wq3 = jnp.concatenate([uq[..., :QK_NOPE], uq_rope, _rot_half_cols(uq_rope)], axis=-1)
    wq3 = jnp.transpose(wq3, (0, 2, 1, 3)).astype(BF16)
    wkv3 = jnp.concatenate([mla_w_uk.reshape(depth, RKV, H, QK_NOPE),
                            mla_w_uv.reshape(depth, RKV, H, V_DIM)], axis=-1)
    wkv3 = jnp.transpose(wkv3, (0, 2, 1, 3)).astype(BF16)
    wq3t = jnp.swapaxes(wq3, 2, 3)

    causal = jnp.tril(jnp.ones((GM_CHUNK, GM_CHUNK), bool))
    ws_p = jnp.where(causal, gm_w_s, 0.0)
    reps = GM_CHUNK // L
    ws_l = jnp.where(causal[:L, :L], gm_w_s[:, :, :L, :L], 0.0)
    ws_s = jnp.einsum("ab,lgij->lgaibj", jnp.eye(reps, dtype=F32), ws_l).reshape(depth, GM_GROUPS, GM_CHUNK, GM_CHUNK)
    ws2 = jnp.stack([ws_p, ws_s], axis=1).astype(BF16)
    gd = GW // GM_GROUPS
    bs_p = jnp.repeat(jnp.transpose(gm_b_s, (0, 2, 1)), gd, axis=2)
    bs_s = jnp.repeat(jnp.tile(jnp.transpose(gm_b_s[:, :, :L], (0, 2, 1)), (1, reps, 1)), gd, axis=2)
    bs2 = jnp.stack([bs_p, bs_s], axis=1)

    router_pad = jnp.pad(moe_router, ((0, 0), (0, 0), (0, LANES - n_experts)))

    half = QK_ROPE // 2
    inv = ROPE_THETA ** (-jnp.arange(half, dtype=F32) / half)
    pos = jnp.concatenate([jnp.tile(jnp.arange(S), B), jnp.tile(past + jnp.arange(L), NB)]).astype(F32)
    ang = pos[:, None] * inv[None, :]
    cos = jnp.tile(jnp.cos(ang), (1, 2))
    sin = jnp.tile(jnp.sin(ang), (1, 2))
    cos_t, sin_t = cos[:TP].T, sin[:TP].T
    scale_log2e = ATTN_SCALE * math.log2(math.e)

    cache_kr_t = jnp.swapaxes(cache_mla_krope, 2, 3)
    hist_s = jnp.pad(cache_conv, ((0, 0), (0, 0), (CONV_HALO - (CV_KERNEL - 1), 0), (0, 0)))

    x = jnp.concatenate([x_prompt.reshape(TP, D), x_sample.reshape(TS, D)], axis=0)
    xb = x.astype(BF16)
    row2 = lambda a: a.reshape(1, -1)

    tm_dense = _tile(T, 512)
    eid_dense = jnp.zeros((T // tm_dense,), jnp.int32)
    nu_dense = jnp.full((1,), T // tm_dense, jnp.int32)

    outs = {k: [] for k in ("lat_p", "kr_p", "conv_p", "lat_s", "kr_s", "conv_s", "v_s")}
    for l in range(depth):
        uvg = _mm_act(xb, w_in_t, l, 0, off_cv, zero_uv, "gelu", F32)
        glu = _mm_glu(xb, w_in_t, l, off_cv, off_cv + CW, CW)
        cqn, ckv, kr = _latents(xb, w_in_t, wkr2_t, l, off_q, off_kv, row2(mla_q_norm_g[l]),
                                row2(mla_kv_norm_g[l]), cos, sin)

        us, v_ln = _gmlp(uvg, ws2[l], bs2[l], row2(gm_ln_g[l]), row2(gm_ln_b[l]), TP)

        cv_args = (cv_w_dw[l], row2(cv_b_dw[l]), row2(cv_ln_g[l]), row2(cv_ln_b[l]))
        yb_p = _conv(glu, glu, 0, B, S, *cv_args, zero_first=True)
        yb_s = _conv(glu, hist_s[l], TP, NB, L, *cv_args, zero_first=False)

        qt_p = _qproj_t(cqn, wq3t[l], cos_t, sin_t, B, S, scale_log2e)
        q_s = _qproj(cqn, wq3[l], cos, sin, TP, NB, L)
        k_p, vt_p = _kvproj(ckv, kr, wkv3[l], B, S)
        o_p = _attn_prompt(qt_p, k_p, vt_p)
        o_s = _attn_sample(q_s, cache_mla_latent, cache_kr_t, l, ckv, kr, wkv3[l], TP)

        merged = _merge(xb, us, yb_p, yb_s, o_p, o_s, gm_w_p_b, cv_w_pw_b, w_o_b, wg_t, row2(b_gate[l]), l)
        x, xb = _outproj_ln(merged, w_out_b, l, x, row2(ln1_g[l]), row2(ln1_b[l]), alpha)

        j = l // 2
        if l % 2 == 0:
            x, xb = _ffn_dense(xb, eid_dense, nu_dense, ffn_w1_b, ffn_w3_b, ffn_w2_b, j, tm_dense,
                               x, row2(ln2_g[l]), row2(ln2_b[l]), alpha)
        else:
            route = _router(x, router_pad[j], n_experts)
            slot_tok, block_e, n_used, dest = _moe_plan(route, n_experts, MOE_TM)
            y = _ffn_moe(x, slot_tok, block_e, n_used, moe_w1_b, moe_w3_b, moe_w2_b, j, MOE_TM)
            x, xb = _combine(x, y, dest, route, row2(ln2_g[l]), row2(ln2_b[l]), alpha)

        outs["lat_p"].append(ckv[:TP].reshape(B, S, RKV))
        outs["kr_p"].append(kr[:TP].reshape(B, S, QK_ROPE))
        outs["conv_p"].append(jnp.stack([glu[(s + 1) * S - (CV_KERNEL - 1):(s + 1) * S] for s in range(B)]))
        outs["lat_s"].append(ckv[TP:].reshape(NB, L, RKV))
        outs["kr_s"].append(kr[TP:].reshape(NB, L, QK_ROPE))
        outs["conv_s"].append(glu[TP:].reshape(NB, L, CW)[:, L - (CV_KERNEL - 1):])
        outs["v_s"].append(v_ln[TP:].reshape(NB, L, GW))

    return (x[:TP].reshape(B, S, D), x[TP:].reshape(NB, L, D),
            jnp.stack(outs["lat_p"]), jnp.stack(outs["kr_p"]), jnp.stack(outs["conv_p"]),
            jnp.stack(outs["lat_s"]), jnp.stack(outs["kr_s"]), jnp.stack(outs["conv_s"]),
            jnp.stack(outs["v_s"]))
```
